```python
import math
import jax, jax.numpy as jnp
from jax import lax
import numpy as np

D_MODEL = 1024
BATCH = 8
SEQ = 2048
DEPTH = 4
DEC_BATCH = 32
DEC_SEQ = 8
PAST_LEN = 8192
PAGE_SIZE = 128

N_HA = 4
HD_A = 64
VD_A = 2 * HD_A
N_HB = 8
N_KV = 2
HD_B = 64
CMP_STRIDE = 16
CMP_BLOCK = 2 * CMP_STRIDE
CMP_HIDDEN = 256
SEL_BLOCK = 64
SEL_TOPN = 16
WINDOW = 512
NUM_BUCKETS = 32
MAX_DISTANCE = 128
Q_BLOCK = 128
NSA_Q_BLOCK = 64
EPS = 1e-6
NEG = -1e30
FORCE = 1e4

COL_SIZES = (
    N_HA * 2 * HD_A,
    N_HA * 2 * HD_A,
    N_HA * VD_A,
    N_HB * HD_B,
    N_KV * HD_B, N_KV * HD_B,
    N_KV * HD_B, N_KV * HD_B,
    N_KV * HD_B, N_KV * HD_B,
    N_HA * VD_A,
    N_HB * HD_B,
    N_HB * 3,
    2 * D_MODEL,
)
IN_COLS = sum(COL_SIZES)

kernel_name = 'hybrid_diffattn_nsa_gated_merge_step'


def _rmsnorm(x, g):
    xf = x.astype(jnp.float32)
    y = xf * lax.rsqrt(jnp.mean(xf * xf, axis=-1, keepdims=True) + EPS)
    return (y * g.astype(jnp.float32)).astype(x.dtype)


def _t5_bucket(rel):
    n = jnp.maximum(rel, 0)
    exact = NUM_BUCKETS // 2
    nf = jnp.maximum(n, 1).astype(jnp.float32)
    large = exact + (jnp.log(nf / exact) / math.log(MAX_DISTANCE / exact) * (NUM_BUCKETS - exact)).astype(jnp.int32)
    large = jnp.minimum(large, NUM_BUCKETS - 1)
    return jnp.where(n < exact, n, large)


def _masked_softmax(s, valid):
    return jax.nn.softmax(jnp.where(valid, s, NEG), axis=-1)


def _gather_pages(pool, page_table):
    g = pool[page_table]
    return g.reshape((g.shape[0], g.shape[1] * g.shape[2]) + g.shape[3:])


def _gather_blocks(blocks, idx):
    return jax.vmap(jax.vmap(lambda blk, ix: blk[ix]))(blocks, idx)


def _diff_attention(q, k, v, pos0, lam_vecs, subln_g, lam_init, tbl):
    B, Tq, H, _, D = q.shape
    T = k.shape[1]
    f32 = jnp.float32
    qb = math.gcd(Tq, Q_BLOCK)
    nb = Tq // qb
    lv = lam_vecs.astype(f32)
    lam = jnp.exp(jnp.sum(lv[0] * lv[1])) - jnp.exp(jnp.sum(lv[2] * lv[3])) + lam_init
    kf = k.astype(f32)
    vf = v.astype(f32)
    tblf = tbl.astype(f32)
    kpos = jnp.arange(T, dtype=jnp.int32)
    qs = q.astype(f32).reshape(B, nb, qb, H, 2, D).transpose(1, 0, 3, 4, 2, 5)
    qpos = (pos0 + jnp.arange(Tq, dtype=jnp.int32)).reshape(nb, qb)
    scale = D ** -0.5

    def block(args):
        qblk, qp = args
        rel = qp[:, None] - kpos[None, :]
        s = jnp.einsum('bhmqd,bshmd->bhmqs', qblk, kf) * scale + tblf[:, _t5_bucket(rel)][None, :, None]
        p = _masked_softmax(s, rel >= 0)
        a = p[:, :, 0] - lam * p[:, :, 1]
        return jnp.einsum('bhqs,bshe->bqhe', a, vf)

    o = lax.map(block, (qs, qpos))
    o = o.transpose(1, 0, 2, 3, 4).reshape(B, Tq, H, v.shape[-1])
    return (_rmsnorm(o, subln_g) * (1.0 - lam_init)).astype(q.dtype)


def _compress(k, pos_emb, w1, b1, w2, b2):
    B, T, G, D = k.shape
    nch = T // CMP_STRIDE
    ch = k[:, : nch * CMP_STRIDE].reshape(B, nch, CMP_STRIDE, G, D)
    blk = jnp.concatenate([ch[:, :-1], ch[:, 1:]], axis=2) + pos_emb[None, None, :, None, :]
    flat = blk.transpose(0, 1, 3, 2, 4).reshape(B, nch - 1, G, CMP_BLOCK * D)
    hid = jax.nn.silu(jnp.einsum('bngi,ih->bngh', flat, w1) + b1)
    return jnp.einsum('bngh,hd->bngd', hid, w2) + b2


def _nsa(q, ck, cv, sk, sv, wk, wv, pos0, win_off, gates, cmp_kn, cmp_pos, w1, b1, w2, b2, tbl):
    B, Tq, HB, D = q.shape
    T = ck.shape[1]
    G = N_KV
    R = HB // G
    f32 = jnp.float32
    scale = D ** -0.5
    ckc = _rmsnorm(_compress(ck, cmp_pos[0], w1[0], b1[0], w2[0], b2[0]), cmp_kn).astype(f32)
    cvc = _compress(cv, cmp_pos[1], w1[1], b1[1], w2[1], b2[1]).astype(f32)
    n_cmp = ckc.shape[1]
    cmp_start = jnp.arange(n_cmp, dtype=jnp.int32) * CMP_STRIDE
    cmp_end = cmp_start + CMP_BLOCK - 1
    n_sel = -(-T // SEL_BLOCK)
    top = min(SEL_TOPN, n_sel)
    pad = n_sel * SEL_BLOCK - T

    def to_blocks(t):
        t = jnp.pad(t.astype(f32), ((0, 0), (0, pad), (0, 0), (0, 0)))
        return t.reshape(B, n_sel, SEL_BLOCK, G, D).transpose(0, 3, 1, 2, 4)

    skb = to_blocks(sk)
    svb = to_blocks(sv)
    sel_start = jnp.arange(n_sel, dtype=jnp.int32) * SEL_BLOCK
    overlap = ((cmp_start[:, None] < sel_start[None, :] + SEL_BLOCK)
               & (cmp_start[:, None] + CMP_BLOCK > sel_start[None, :])).astype(f32)
    zpad = jnp.zeros((B, WINDOW, G, D), f32)
    wkp = jnp.concatenate([zpad, wk.astype(f32)], axis=1)
    wvp = jnp.concatenate([zpad, wv.astype(f32)], axis=1)
    tbl_g = tbl.astype(f32).reshape(G, R, NUM_BUCKETS)
    g_ix = jnp.arange(G)[None, :, None, None, None]
    r_ix = jnp.arange(R)[None, None, :, None, None]
    qb = math.gcd(Tq, NSA_Q_BLOCK)
    nb = Tq // qb
    qs = q.astype(f32).reshape(B, nb, qb, G, R, D).transpose(1, 0, 3, 4, 2, 5)
    gs = jax.nn.sigmoid(gates.astype(f32)).reshape(B, nb, qb, G, R, 3).transpose(1, 0, 3, 4, 2, 5)
    qpos = (pos0 + jnp.arange(Tq, dtype=jnp.int32)).reshape(nb, qb)
    j = jnp.arange(n_sel, dtype=jnp.int32)
    sb_off = jnp.arange(SEL_BLOCK, dtype=jnp.int32)
    w_off = jnp.arange(WINDOW + qb, dtype=jnp.int32)

    def block(args):
        qblk, g, qp = args
        valid_c = cmp_end[None, :] <= qp[:, None]
        s_c = jnp.einsum('bgrqd,bngd->bgrqn', qblk, ckc) * scale
        p_c = _masked_softmax(s_c, valid_c) * jnp.any(valid_c, axis=-1)[:, None].astype(f32)
        o_cmp = jnp.einsum('bgrqn,bngd->bgrqd', p_c, cvc)
        imp = jnp.einsum('bgrqn,nj->bgqj', p_c, overlap)
        own = qp[:, None] // SEL_BLOCK
        forced = (j[None, :] == own) | (j[None, :] == 0)
        imp = jnp.where(j[None, :] > own, -1.0, jnp.where(forced, FORCE, imp))
        _, idx = lax.top_k(imp, top)
        gk = _gather_blocks(skb, idx).reshape(B, G, qb, top * SEL_BLOCK, D)
        gv = _gather_blocks(svb, idx).reshape(B, G, qb, top * SEL_BLOCK, D)
        kpos = (idx[..., None] * SEL_BLOCK + sb_off).reshape(B, G, qb, top * SEL_BLOCK)
        rel = qp[None, None, :, None] - kpos
        s_s = jnp.einsum('bgrqd,bgqsd->bgrqs', qblk, gk) * scale + tbl_g[g_ix, r_ix, _t5_bucket(rel)[:, :, None]]
        p_s = _masked_softmax(s_s, (rel >= 0)[:, :, None])
        o_sel = jnp.einsum('bgrqs,bgqsd->bgrqd', p_s, gv)
        st = qp[0] - win_off
        kw = lax.dynamic_slice_in_dim(wkp, st, WINDOW + qb, axis=1)
        vw = lax.dynamic_slice_in_dim(wvp, st, WINDOW + qb, axis=1)
        kposw = qp[0] - WINDOW + w_off
        rel_w = qp[:, None] - kposw[None, :]
        valid_w = (rel_w >= 0) & (rel_w < WINDOW) & (kposw >= win_off)[None, :]
        s_w = jnp.einsum('bgrqd,bsgd->bgrqs', qblk, kw) * scale + tbl_g[:, :, _t5_bucket(rel_w)][None]
        p_w = _masked_softmax(s_w, valid_w)
        o_win = jnp.einsum('bgrqs,bsgd->bgrqd', p_w, vw)
        return g[..., 0:1] * o_cmp + g[..., 1:2] * o_sel + g[..., 2:3] * o_win

    o = lax.map(block, (qs, gs, qpos))
    return o.transpose(1, 0, 4, 2, 3, 5).reshape(B, Tq, HB, D).astype(q.dtype)


def _mixer_layer(x, pos0, past_dk, past_dv, past_ck, past_cv, past_sk, past_sv, buf_wk, buf_wv,
                 norm_g, w_in, b_in, d_qn, d_kn, d_lam, d_subln, n_qn, n_kn,
                 cmp_pos, cmp_w1, cmp_b1, cmp_w2, cmp_b2, w_a, w_b, w_out, rel_bias, lam_init):
    B, Tq, _ = x.shape
    h = _rmsnorm(x, norm_g)
    z = jnp.einsum('btd,dc->btc', h, w_in) + b_in
    splits = np.cumsum(COL_SIZES)[:-1].tolist()
    dq, dk, dv, nq, ck, cv, sk, sv, wk, wv, ga, gb, gn, gm = jnp.split(z, splits, axis=-1)
    dq = _rmsnorm(dq.reshape(B, Tq, N_HA, 2, HD_A), d_qn)
    dk = _rmsnorm(dk.reshape(B, Tq, N_HA, 2, HD_A), d_kn)
    dv = dv.reshape(B, Tq, N_HA, VD_A)
    nq = _rmsnorm(nq.reshape(B, Tq, N_HB, HD_B), n_qn)
    kvshape = (B, Tq, N_KV, HD_B)
    ck = ck.reshape(kvshape)
    cv = cv.reshape(kvshape)
    sk = _rmsnorm(sk.reshape(kvshape), n_kn[1])
    sv = sv.reshape(kvshape)
    wk = _rmsnorm(wk.reshape(kvshape), n_kn[2])
    wv = wv.reshape(kvshape)
    o_a = _diff_attention(dq, jnp.concatenate([past_dk, dk], axis=1), jnp.concatenate([past_dv, dv], axis=1),
                          pos0, d_lam, d_subln, lam_init, rel_bias[:N_HA])
    buf_k = jnp.concatenate([buf_wk, wk], axis=1)
    buf_v = jnp.concatenate([buf_wv, wv], axis=1)
    win_off = pos0 + Tq - buf_k.shape[1]
    o_b = _nsa(nq, jnp.concatenate([past_ck, ck], axis=1), jnp.concatenate([past_cv, cv], axis=1),
               jnp.concatenate([past_sk, sk], axis=1), jnp.concatenate([past_sv, sv], axis=1),
               buf_k, buf_v, pos0, win_off, gn, n_kn[0], cmp_pos, cmp_w1, cmp_b1, cmp_w2, cmp_b2,
               rel_bias[N_HA:])
    o_a = o_a.reshape(B, Tq, N_HA * VD_A) * jax.nn.silu(ga)
    o_b = o_b.reshape(B, Tq, N_HB * HD_B) * jax.nn.silu(gb)
    ya = jnp.einsum('btc,cd->btd', o_a, w_a)
    yb = jnp.einsum('btc,cd->btd', o_b, w_b)
    m = jax.nn.sigmoid(gm[..., :D_MODEL]) * ya + jax.nn.sigmoid(gm[..., D_MODEL:]) * yb
    y = x + jnp.einsum('btd,de->bte', m, w_out)
    n_win = min(WINDOW, buf_k.shape[1])
    return y, (dk, dv, ck, cv, sk, sv, buf_k[:, -n_win:], buf_v[:, -n_win:])


def setup_inputs(seed: int = 0) -> dict:
    key = jax.random.key(seed)
    ks = jax.random.split(key, 32)
    f32 = jnp.float32
    n_pages = PAST_LEN // PAGE_SIZE
    n_pool = (DEC_BATCH * n_pages * 5) // 4
    wb = min(WINDOW, PAST_LEN)

    def nrm(k, shape, scale=1.0):
        return scale * jax.random.normal(k, shape, f32)

    page_table = jax.random.permutation(ks[0], n_pool)[: DEC_BATCH * n_pages].reshape(DEC_BATCH, n_pages).astype(jnp.int32)
    pool_kv = (DEPTH, n_pool, PAGE_SIZE, N_KV, HD_B)
    win_kv = (DEPTH, DEC_BATCH, wb, N_KV, HD_B)
    return {
        'x_prompt': nrm(ks[1], (BATCH, SEQ, D_MODEL)),
        'x_sample': nrm(ks[2], (DEC_BATCH, DEC_SEQ, D_MODEL)),
        'cache_diff_k': nrm(ks[3], (DEPTH, n_pool, PAGE_SIZE, N_HA, 2, HD_A)),
        'cache_diff_v': nrm(ks[4], (DEPTH, n_pool, PAGE_SIZE, N_HA, VD_A)),
        'cache_cmp_k': nrm(ks[5], pool_kv),
        'cache_cmp_v': nrm(ks[6], pool_kv),
        'cache_sel_k': nrm(ks[7], pool_kv),
        'cache_sel_v': nrm(ks[8], pool_kv),
        'state_win_k': nrm(ks[9], win_kv),
        'state_win_v': nrm(ks[10], win_kv),
        'page_table': page_table,
        'rel_bias': nrm(ks[11], (N_HA + N_HB, NUM_BUCKETS), 0.5),
        'norm_g': 1.0 + nrm(ks[12], (DEPTH, D_MODEL), 0.01),
        'w_in': nrm(ks[13], (DEPTH, D_MODEL, IN_COLS), D_MODEL ** -0.5),
        'b_in': nrm(ks[14], (DEPTH, IN_COLS), 0.01),
        'diff_q_norm': 1.0 + nrm(ks[15], (DEPTH, HD_A), 0.01),
        'diff_k_norm': 1.0 + nrm(ks[16], (DEPTH, HD_A), 0.01),
        'diff_lambda': nrm(ks[17], (DEPTH, 4, HD_A), 0.1),
        'diff_subln': 1.0 + nrm(ks[18], (DEPTH, VD_A), 0.01),
        'nsa_q_norm': 1.0 + nrm(ks[19], (DEPTH, HD_B), 0.01),
        'nsa_k_norm': 1.0 + nrm(ks[20], (DEPTH, 3, HD_B), 0.01),
        'cmp_pos': nrm(ks[21], (DEPTH, 2, CMP_BLOCK, HD_B), 0.02),
        'cmp_w1': nrm(ks[22], (DEPTH, 2, CMP_BLOCK * HD_B, CMP_HIDDEN), (CMP_BLOCK * HD_B) ** -0.5),
        'cmp_b1': nrm(ks[23], (DEPTH, 2, CMP_HIDDEN), 0.01),
        'cmp_w2': nrm(ks[24], (DEPTH, 2, CMP_HIDDEN, HD_B), CMP_HIDDEN ** -0.5),
        'cmp_b2': nrm(ks[25], (DEPTH, 2, HD_B), 0.01),
        'w_branch_a': nrm(ks[26], (DEPTH, N_HA * VD_A, D_MODEL), (N_HA * VD_A) ** -0.5),
        'w_branch_b': nrm(ks[27], (DEPTH, N_HB * HD_B, D_MODEL), (N_HB * HD_B) ** -0.5),
        'w_out': nrm(ks[28], (DEPTH, D_MODEL, D_MODEL), D_MODEL ** -0.5),
    }


def reference(x_prompt, x_sample, cache_diff_k, cache_diff_v, cache_cmp_k, cache_cmp_v, cache_sel_k, cache_sel_v,
              state_win_k, state_win_v, page_table, rel_bias, norm_g, w_in, b_in, diff_q_norm, diff_k_norm,
              diff_lambda, diff_subln, nsa_q_norm, nsa_k_norm, cmp_pos, cmp_w1, cmp_b1, cmp_w2, cmp_b2,
              w_branch_a, w_branch_b, w_out):
    past_len = page_table.shape[1] * cache_diff_k.shape[2]
    bp = x_prompt.shape[0]
    dt = x_prompt.dtype

    def empty(t):
        return jnp.zeros((bp, 0) + t.shape[3:], dt)

    yp = x_prompt
    ys = x_sample
    rows_p = []
    rows_s = []
    for l in range(DEPTH):
        lam_init = 0.8 - 0.6 * math.exp(-0.3 * l)
        lw = (norm_g[l], w_in[l], b_in[l], diff_q_norm[l], diff_k_norm[l], diff_lambda[l], diff_subln[l],
              nsa_q_norm[l], nsa_k_norm[l], cmp_pos[l], cmp_w1[l], cmp_b1[l], cmp_w2[l], cmp_b2[l],
              w_branch_a[l], w_branch_b[l], w_out[l], rel_bias, lam_init)
        yp, rp = _mixer_layer(yp, 0, empty(cache_diff_k), empty(cache_diff_v), empty(cache_cmp_k),
                              empty(cache_cmp_v), empty(cache_sel_k), empty(cache_sel_v),
                              empty(state_win_k), empty(state_win_v), *lw)
        ys, rs = _mixer_layer(ys, past_len,
                              _gather_pages(cache_diff_k[l], page_table), _gather_pages(cache_diff_v[l], page_table),
                              _gather_pages(cache_cmp_k[l], page_table), _gather_pages(cache_cmp_v[l], page_table),
                              _gather_pages(cache_sel_k[l], page_table), _gather_pages(cache_sel_v[l], page_table),
                              state_win_k[l], state_win_v[l], *lw)
        rows_p.append(rp)
        rows_s.append(rs)
    p_diff_k, p_diff_v, p_cmp_k, p_cmp_v, p_sel_k, p_sel_v, p_win_k, p_win_v = [jnp.stack(r, axis=0) for r in zip(*rows_p)]
    s_diff_k, s_diff_v, s_cmp_k, s_cmp_v, s_sel_k, s_sel_v, s_win_k, s_win_v = [jnp.stack(r, axis=0) for r in zip(*rows_s)]
    return (yp, ys, p_diff_k, p_diff_v, p_cmp_k, p_cmp_v, p_sel_k, p_sel_v, p_win_k, p_win_v,
            s_diff_k, s_diff_v, s_cmp_k, s_cmp_v, s_sel_k, s_sel_v, s_win_k, s_win_v)
```

```python
import functools
import math

import numpy as np
import jax
import jax.numpy as jnp
from jax import lax
from jax.experimental import pallas as pl
from jax.experimental.pallas import tpu as pltpu

F32 = jnp.float32
BF16 = jnp.bfloat16

N_HA, HD_A, VD_A = 4, 64, 128
N_HB, N_KV, HD_B = 8, 2, 64
R_B = N_HB // N_KV
CMP_STRIDE, CMP_BLOCK = 16, 32
SEL_BLOCK, SEL_TOPN, WINDOW = 64, 16, 512
NUM_BUCKETS, MAX_DISTANCE = 32, 128
EPS, NEG, FORCE = 1e-6, -1e30, 1e4
MASKED_BUCKET = NUM_BUCKETS
SEL_SHIFT = SEL_BLOCK.bit_length() - 1
HD_SHIFT = HD_A.bit_length() - 1

LANE = 128
TQ = 128
VMEM_LIMIT = 56 * 1024 * 1024

C_KV = 2048
C_GA = 2816
C_GN = 3840
N_GN = N_HB * 3


def _mm(a, b):
    return jnp.dot(a, b, preferred_element_type=F32)


def _mm_nt(a, b):
    return lax.dot_general(a, b, (((1,), (1,)), ((), ())), preferred_element_type=F32)


def _split(x):
    hi = x.astype(BF16)
    lo = (x - hi.astype(F32)).astype(BF16)
    return hi, lo


def _sigmoid(z):
    return 1.0 / (1.0 + jnp.exp(-z))


def _group_rms(z, bd, gain):
    hi, lo = _split(z * z)
    ss = _mm(hi, bd) + _mm(lo, bd)
    return z * lax.rsqrt(ss * (1.0 / HD_A) + EPS) * gain


def _flash_update(s, v, m_ref, l_ref, acc_ref):
    m_old = m_ref[...]
    m_new = jnp.maximum(m_old, jnp.max(s, axis=-1, keepdims=True))
    alpha = jnp.exp(m_old - m_new)
    p = jnp.exp(s - m_new)
    l_ref[...] = alpha * l_ref[...] + jnp.sum(p, axis=-1, keepdims=True)
    acc_ref[...] = alpha * acc_ref[...] + _mm(p.astype(BF16), v)
    m_ref[...] = m_new


def _flash_init(m_ref, l_ref, acc_ref):
    m_ref[...] = jnp.full(m_ref.shape, -3e38, F32)
    l_ref[...] = jnp.zeros(l_ref.shape, F32)
    acc_ref[...] = jnp.zeros(acc_ref.shape, F32)


def _params(*sem):
    return pltpu.CompilerParams(dimension_semantics=sem, vmem_limit_bytes=VMEM_LIMIT)


def _bucket_np(rel):
    n = np.maximum(rel, 0)
    exact = NUM_BUCKETS // 2
    nf = np.maximum(n, 1).astype(np.float32)
    large = exact + (np.log(nf / np.float32(exact)) / np.float32(math.log(MAX_DISTANCE / exact))
                     * np.float32(NUM_BUCKETS - exact)).astype(np.int32)
    large = np.minimum(large, NUM_BUCKETS - 1)
    return np.where(n < exact, n, large).astype(np.int32)


def _prompt_bucket_tiles():
    r = np.arange(TQ)[:, None]
    c = np.arange(TQ)[None, :]
    d0 = np.where(r - c >= 0, _bucket_np(r - c), MASKED_BUCKET)
    d1 = _bucket_np(TQ + r - c)
    return np.stack([d0, d1]).astype(np.int32)


def _decode_bucket_tiles(dec_seq, page, wb):
    q = np.arange(dec_seq)[:, None]
    c = np.arange(page)[None, :]
    last = _bucket_np(page + q - c)
    tail = np.where((c < dec_seq) & (c <= q), _bucket_np(q - c), MASKED_BUCKET)
    i = np.arange(wb)[None, :]
    relw = wb + q - i
    win = np.where(relw < WINDOW, _bucket_np(relw), MASKED_BUCKET)
    return np.concatenate([last, tail, win], axis=1).astype(np.int32)


def _overlap_np(n_rows, n_cols, n_cmp, n_sel):
    n = np.arange(n_rows)[:, None]
    j = np.arange(n_cols)[None, :]
    ov = (n * CMP_STRIDE < j * SEL_BLOCK + SEL_BLOCK) & (n * CMP_STRIDE + CMP_BLOCK > j * SEL_BLOCK)
    ov &= (n < n_cmp) & (j < n_sel)
    return ov.astype(np.float32)


def _blockdiag_np(n, group):
    i = np.arange(n)
    return (i[:, None] // group == i[None, :] // group).astype(np.float32)


def _bias_body(tbl_ref, idxp_ref, idxd_ref, p_ref, d_ref):
    h = pl.program_id(0)
    far = tbl_ref[h, NUM_BUCKETS - 1]

    def expand(idx):
        acc = jnp.zeros(idx.shape, F32)
        for b in range(NUM_BUCKETS - 1):
            acc = jnp.where(idx == b, tbl_ref[h, b] - far, acc)
        return jnp.where(idx == MASKED_BUCKET, NEG, acc)

    p_ref[0] = expand(idxp_ref[0])
    p_ref[1] = expand(idxp_ref[1])
    d_ref[...] = expand(idxd_ref[...])


def _bias_tiles(rel_bias, idx_p, idx_d):
    nh = rel_bias.shape[0]
    dq, wd = idx_d.shape
    return pl.pallas_call(
        _bias_body,
        grid=(nh,),
        in_specs=[pl.BlockSpec(memory_space=pltpu.SMEM),
                  pl.BlockSpec((2, TQ, TQ), lambda h: (0, 0, 0)),
                  pl.BlockSpec((dq, wd), lambda h: (0, 0))],
        out_specs=[pl.BlockSpec((None, 2, TQ, TQ), lambda h: (h, 0, 0, 0)),
                   pl.BlockSpec((None, dq, wd), lambda h: (h, 0, 0))],
        out_shape=[jax.ShapeDtypeStruct((nh, 2, TQ, TQ), F32),
                   jax.ShapeDtypeStruct((nh, dq, wd), F32)],
        compiler_params=_params("arbitrary"),
        name="bias_tiles",
    )(rel_bias, idx_p, idx_d)


def _in_proj_body(x_ref, g_ref, w_ref, b_ref, bd_ref, gain_ref,
                  dq_o, dk_o, dv_o, nq_o, ck_o, cv_o, sk_o, sv_o, wk_o, wv_o,
                  kvb_o, gs_o, gm_o, gn_o, *, d_model):
    x = x_ref[...]
    ms = jnp.mean(x * x, axis=-1, keepdims=True)
    h = (x * lax.rsqrt(ms + EPS) * g_ref[...]).astype(BF16)

    def seg(a, b):
        return _mm(h, w_ref[:, a:b]) + b_ref[:, a:b]

    bd = bd_ref[...]
    bd2 = bd[0:LANE, 0:LANE]
    dq_o[...] = _group_rms(seg(0, 512), bd, gain_ref[0:1, :]).astype(BF16)
    y = _group_rms(seg(512, 1024), bd, gain_ref[1:2, :])
    dk_o[...] = y
    kvb_o[:, 0:512] = y.astype(BF16)
    y = seg(1024, 1536)
    dv_o[...] = y
    kvb_o[:, 512:1024] = y.astype(BF16)
    nq_o[...] = _group_rms(seg(1536, 2048), bd, gain_ref[2:3, :]).astype(BF16)
    z = seg(C_KV, C_KV + 768)
    ck_o[...] = z[:, 0:128]
    cv_o[...] = z[:, 128:256]
    sk = _group_rms(z[:, 256:384], bd2, gain_ref[3:4, 0:128])
    sv = z[:, 384:512]
    wk = _group_rms(z[:, 512:640], bd2, gain_ref[3:4, 128:256])
    wv = z[:, 640:768]
    sk_o[...] = sk
    sv_o[...] = sv
    wk_o[...] = wk
    wv_o[...] = wv
    kvb_o[:, 1024:1152] = sk.astype(BF16)
    kvb_o[:, 1152:1280] = sv.astype(BF16)
    kvb_o[:, 1280:1408] = wk.astype(BF16)
    kvb_o[:, 1408:1536] = wv.astype(BF16)
    z = seg(C_GA, C_GA + 1024)
    gs_o[...] = z * _sigmoid(z)
    c_gm = C_GN
    gm_o[...] = _sigmoid(seg(c_gm, c_gm + 2 * d_model))
    gn_o[...] = _sigmoid(seg(c_gm + 2 * d_model, c_gm + 2 * d_model + LANE))


def _in_proj(x2d, l, prep):
    n, d = x2d.shape
    tm = min(256, n)
    c = prep["w_in"].shape[-1]

    def row(w):
        return pl.BlockSpec((tm, w), lambda i: (i, 0))

    def lay(*shape):
        return pl.BlockSpec((None,) + shape, lambda i: (l,) + (0,) * len(shape))

    widths = [(512, BF16), (512, F32), (512, F32), (512, BF16)] + [(128, F32)] * 6 + \
             [(1536, BF16), (1024, F32), (2 * d, F32), (LANE, F32)]
    return pl.pallas_call(
        functools.partial(_in_proj_body, d_model=d),
        grid=(n // tm,),
        in_specs=[row(d), lay(1, d), lay(d, c), lay(1, c),
                  pl.BlockSpec((512, 512), lambda i: (0, 0)), lay(8, 512)],
        out_specs=[row(w) for w, _ in widths],
        out_shape=[jax.ShapeDtypeStruct((n, w), dt) for w, dt in widths],
        compiler_params=_params("arbitrary"),
        name="in_proj",
    )(x2d, prep["norm_g"], prep["w_in"], prep["b_in"], prep["bd512"], prep["gains"])


def _out_proj_body(x_ref, oa_ref, ob_ref, gm_ref, wa_ref, wb_ref, wo_ref, y_ref, *, d_model):
    ya = _mm(oa_ref[...].astype(BF16), wa_ref[...])
    yb = _mm(ob_ref[...].astype(BF16), wb_ref[...])
    gm = gm_ref[...]
    m = gm[:, 0:d_model] * ya + gm[:, d_model:2 * d_model] * yb
    y_ref[...] = x_ref[...] + _mm(m.astype(BF16), wo_ref[...])


def _out_proj(x2d, oa, ob, gm, l, prep):
    n, d = x2d.shape
    tm = min(512, n)

    def row(w):
        return pl.BlockSpec((tm, w), lambda i: (i, 0))

    def lay(*shape):
        return pl.BlockSpec((None,) + shape, lambda i: (l,) + (0,) * len(shape))

    return pl.pallas_call(
        functools.partial(_out_proj_body, d_model=d),
        grid=(n // tm,),
        in_specs=[row(d), row(512), row(512), row(2 * d), lay(512, d), lay(512, d), lay(d, d)],
        out_specs=row(d),
        out_shape=jax.ShapeDtypeStruct((n, d), F32),
        compiler_params=_params("arbitrary"),
        name="out_proj",
    )(x2d, oa, ob, gm, prep["w_a"], prep["w_b"], prep["w_out"])


def _lambda(lam_ref, lam_init):
    lv = lam_ref[...]
    a = jnp.sum(lv[0:1] * lv[1:2], axis=-1, keepdims=True)
    b = jnp.sum(lv[2:3] * lv[3:4], axis=-1, keepdims=True)
    return jnp.exp(a) - jnp.exp(b) + lam_init


def _diff_finish(o1, o2, lam, sub, lam_init):
    o = o1 - lam * o2
    ms = jnp.mean(o * o, axis=-1, keepdims=True)
    return o * lax.rsqrt(ms + EPS) * sub * (1.0 - lam_init)


def _diff_prompt_body(q_ref, k_ref, v_ref, gs_ref, bias_ref, lam_ref, sub_ref, o_ref,
                      m_sc, l_sc, acc_sc, *, lam_init):
    i = pl.program_id(2)
    q = q_ref[...]
    lane = lax.broadcasted_iota(jnp.int32, q.shape, 1)
    zero = jnp.zeros_like(q)
    qq = jnp.concatenate([jnp.where(lane < HD_A, q, zero), jnp.where(lane >= HD_A, q, zero)], axis=0)
    _flash_init(m_sc, l_sc, acc_sc)

    def tile(j, bias):
        off = pl.multiple_of(j * TQ, TQ)
        s = _mm_nt(qq, k_ref[pl.ds(off, TQ), :])
        if bias is not None:
            s = s + jnp.concatenate([bias, bias], axis=0)
        _flash_update(s, v_ref[pl.ds(off, TQ), :], m_sc, l_sc, acc_sc)

    def far(j, carry):
        tile(j, None)
        return carry

    lax.fori_loop(0, jnp.maximum(i - 1, 0), far, 0)

    @pl.when(i >= 1)
    def _():
        tile(i - 1, bias_ref[1])

    tile(i, bias_ref[0])
    o = acc_sc[...] / l_sc[...]
    y = _diff_finish(o[0:TQ], o[TQ:2 * TQ], _lambda(lam_ref, lam_init), sub_ref[...], lam_init)
    o_ref[...] = (y * gs_ref[...]).astype(BF16)


def _diff_prompt(dq_b, kvb, gs, bias_p, l, prep, bsz, t, lam_init):
    nq = t // TQ
    return pl.pallas_call(
        functools.partial(_diff_prompt_body, lam_init=lam_init),
        grid=(bsz, N_HA, nq),
        in_specs=[pl.BlockSpec((TQ, LANE), lambda b, h, i: (b * nq + i, h)),
                  pl.BlockSpec((t, LANE), lambda b, h, i: (b, h)),
                  pl.BlockSpec((t, LANE), lambda b, h, i: (b, N_HA + h)),
                  pl.BlockSpec((TQ, LANE), lambda b, h, i: (b * nq + i, h)),
                  pl.BlockSpec((None, 2, TQ, TQ), lambda b, h, i: (h, 0, 0, 0)),
                  pl.BlockSpec((None, 4, HD_A), lambda b, h, i: (l, 0, 0)),
                  pl.BlockSpec((None, 1, VD_A), lambda b, h, i: (l, 0, 0))],
        out_specs=pl.BlockSpec((TQ, LANE), lambda b, h, i: (b * nq + i, h)),
        out_shape=jax.ShapeDtypeStruct((bsz * t, N_HA * VD_A), BF16),
        scratch_shapes=[pltpu.VMEM((2 * TQ, 1), F32), pltpu.VMEM((2 * TQ, 1), F32),
                        pltpu.VMEM((2 * TQ, VD_A), F32)],
        compiler_params=_params("arbitrary", "arbitrary", "arbitrary"),
        name="diff_prompt",
    )(dq_b, kvb, kvb, gs, bias_p, prep["lam"], prep["subln"])


def _deinterleave(a):
    lane = lax.broadcasted_iota(jnp.int32, (a.shape[0], LANE), 1)
    low = lane < HD_B
    g0, g1 = [], []
    for k in range(CMP_STRIDE // 2):
        xe = a[:, 2 * LANE * k:2 * LANE * k + LANE]
        xo = a[:, 2 * LANE * k + LANE:2 * LANE * (k + 1)]
        g0.append(jnp.where(low, xe, pltpu.roll(xo, HD_B, 1)))
        g1.append(jnp.where(low, pltpu.roll(xe, HD_B, 1), xo))
    return jnp.concatenate(g0, axis=1).astype(BF16), jnp.concatenate(g1, axis=1).astype(BF16)


def _compress_partial(a, w1_ref):
    a0, a1 = _deinterleave(a)
    half = CMP_STRIDE * HD_B
    w1a = w1_ref[0:half, :]
    w1b = w1_ref[half:2 * half, :]
    return _mm(a0, w1a), _mm(a1, w1a), _mm(a0, w1b), _mm(a1, w1b)


def _compress_finish(parts, w1_ref, pos_ref, b1_ref, w2_ref, b2_ref):
    ha0, ha1, hb0, hb1 = parts
    m = ha0.shape[0]
    pos = jnp.broadcast_to(pos_ref[...], (8, pos_ref.shape[-1])).astype(BF16)
    c = _mm(pos, w1_ref[...])[0:1] + b1_ref[...]
    h0 = ha0 + pltpu.roll(hb0, m - 1, 0) + c
    h1 = ha1 + pltpu.roll(hb1, m - 1, 0) + c
    hid = jnp.concatenate([h0 * _sigmoid(h0), h1 * _sigmoid(h1)], axis=1).astype(BF16)
    return _mm(hid, w2_ref[...]) + b2_ref[...]


def _compress_prompt_body(ak_ref, av_ref, w1_ref, pos_ref, b1_ref, w2_ref, b2_ref, kn_ref, bd_ref,
                          ck_o, cv_o):
    outs = []
    for kv, a_ref in enumerate((ak_ref, av_ref)):
        parts = _compress_partial(a_ref[...], w1_ref.at[kv])
        outs.append(_compress_finish(parts, w1_ref.at[kv], pos_ref.at[kv], b1_ref.at[kv],
                                     w2_ref.at[kv], b2_ref.at[kv]))
    ck_o[...] = _group_rms(outs[0], bd_ref[...], kn_ref[...])
    cv_o[...] = outs[1]


def _cmp_weight_specs(l, nidx):
    def lay(*shape):
        return pl.BlockSpec((None,) + shape, lambda *a: (l,) + (0,) * len(shape))
    flat = CMP_BLOCK * HD_B
    return [lay(2, flat, 256), lay(2, 1, flat), lay(2, 1, 256), lay(2, 512, LANE), lay(2, 1, LANE),
            lay(1, LANE), pl.BlockSpec((LANE, LANE), lambda *a: (0, 0))]


def _cmp_weights(prep):
    return (prep["w1"], prep["pos"], prep["b1"], prep["w2bd"], prep["b2"], prep["cmp_kn"], prep["bd128"])


def _compress_prompt(ck, cv, l, prep, bsz, t):
    nch = t // CMP_STRIDE
    wdt = CMP_STRIDE * LANE
    a_spec = pl.BlockSpec((nch, wdt), lambda b: (b, 0))
    o_spec = pl.BlockSpec((nch, LANE), lambda b: (b, 0))
    return pl.pallas_call(
        _compress_prompt_body,
        grid=(bsz,),
        in_specs=[a_spec, a_spec] + _cmp_weight_specs(l, 1),
        out_specs=[o_spec, o_spec],
        out_shape=[jax.ShapeDtypeStruct((bsz * nch, LANE), F32)] * 2,
        compiler_params=_params("arbitrary"),
        name="compress_prompt",
    )(ck.reshape(bsz * nch, wdt), cv.reshape(bsz * nch, wdt), *_cmp_weights(prep))


def _nsa_q_rows(qf, g):
    m = qf.shape[0]
    zero = jnp.zeros((m, HD_B), F32)
    rows = []
    for r in range(R_B):
        c0 = (g * R_B + r) * HD_B
        piece = qf[:, c0:c0 + HD_B]
        rows.append(jnp.concatenate([piece, zero] if g == 0 else [zero, piece], axis=1))
    return jnp.concatenate(rows, axis=0).astype(BF16)


def _cmp_attention(qrows, ckc, cvc, valid):
    hi, lo = _split(ckc)
    s = _mm_nt(qrows, hi) + _mm_nt(qrows, lo)
    s = jnp.where(valid, s, NEG)
    m = jnp.max(s, axis=-1, keepdims=True)
    e = jnp.exp(s - m)
    anyv = jnp.max(jnp.where(valid, 1.0, 0.0), axis=-1, keepdims=True)
    p = e / jnp.sum(e, axis=-1, keepdims=True) * anyv
    return p, _mm(p.astype(BF16), cvc.astype(BF16))


def _select_blocks(imp_sum, ov, own, n_sel, top):
    hi, lo = _split(imp_sum)
    imp = _mm(hi, ov) + _mm(lo, ov)
    j = lax.broadcasted_iota(jnp.int32, imp.shape, 1)
    imp = jnp.where(j > own, -1.0, jnp.where((j == own) | (j == 0), FORCE, imp))
    imp = jnp.where(j >= n_sel, -2.0, imp)
    jf = j.astype(F32)
    rank = jnp.zeros(imp.shape, F32)
    for k in range(n_sel):
        col = imp[:, k:k + 1]
        rank = rank + jnp.where(col > imp, 1.0, jnp.where(col == imp, jnp.where(jf > k, 1.0, 0.0), 0.0))
    return jnp.where(rank < top, 1.0, 0.0)


def _block_mask(sel_bf16, first_block, per_tile):
    nb = sel_bf16.shape[1]
    width = per_tile * SEL_BLOCK
    j = lax.broadcasted_iota(jnp.int32, (nb, width), 0)
    c = lax.broadcasted_iota(jnp.int32, (nb, width), 1)
    e = jnp.where(j == first_block + (c >> SEL_SHIFT), 1.0, 0.0).astype(BF16)
    return jnp.where(_mm(sel_bf16, e) > 0.5, 0.0, NEG)


def _nsa_prompt_body(q_ref, ckc_ref, cvc_ref, sk_ref, sv_ref, wk_ref, wv_ref, gn_ref, gs_ref,
                     pb_ref, ov_ref, o_ref, m_sc, l_sc, acc_sc, *, n_cmp, n_sel, top):
    i = pl.program_id(1)
    qf = q_ref[...].astype(F32)
    gn = gn_ref[...]
    ckc = ckc_ref[...]
    cvc = cvc_ref[...]
    ncp = ckc.shape[0]
    qp = i * TQ + lax.broadcasted_iota(jnp.int32, (TQ, 1), 0)
    qp4 = i * TQ + (lax.broadcasted_iota(jnp.int32, (R_B * TQ, 1), 0) & (TQ - 1))
    nidx = lax.broadcasted_iota(jnp.int32, (R_B * TQ, ncp), 1)
    valid = (nidx * CMP_STRIDE + (CMP_BLOCK - 1) <= qp4) & (nidx < n_cmp)
    own = qp >> SEL_SHIFT
    rr = lax.broadcasted_iota(jnp.int32, (TQ, TQ), 0)
    cc = lax.broadcasted_iota(jnp.int32, (TQ, TQ), 1)
    anti = jnp.where(cc > rr, 0.0, NEG)
    anti4 = jnp.concatenate([anti] * R_B, axis=0)
    wt = WINDOW // TQ
    pieces = []
    for g in range(N_KV):
        qrows = _nsa_q_rows(qf, g)
        bias0 = jnp.concatenate([pb_ref[g * R_B + r, 0] for r in range(R_B)], axis=0)
        bias1 = jnp.concatenate([pb_ref[g * R_B + r, 1] for r in range(R_B)], axis=0)
        p_c, o_cmp = _cmp_attention(qrows, ckc, cvc, valid)
        imp_sum = p_c[0:TQ]
        for r in range(1, R_B):
            imp_sum = imp_sum + p_c[r * TQ:(r + 1) * TQ]
        sel = _select_blocks(imp_sum, ov_ref[...], own, n_sel, top).astype(BF16)

        _flash_init(m_sc, l_sc, acc_sc)

        def sel_tile(j, bias):
            off = pl.multiple_of(j * TQ, TQ)
            mask = _block_mask(sel, j * (TQ // SEL_BLOCK), TQ // SEL_BLOCK)
            s = _mm_nt(qrows, sk_ref[pl.ds(off, TQ), :]) + jnp.concatenate([mask] * R_B, axis=0)
            if bias is not None:
                s = s + bias
            _flash_update(s, sv_ref[pl.ds(off, TQ), :], m_sc, l_sc, acc_sc)

        def sel_far(j, carry):
            sel_tile(j, None)
            return carry

        lax.fori_loop(0, jnp.maximum(i - 1, 0), sel_far, 0)

        @pl.when(i >= 1)
        def _():
            sel_tile(i - 1, bias1)

        sel_tile(i, bias0)
        o_sel = acc_sc[...] / l_sc[...]

        _flash_init(m_sc, l_sc, acc_sc)

        def win_tile(j, extra):
            off = pl.multiple_of(j * TQ, TQ)
            s = _mm_nt(qrows, wk_ref[pl.ds(off, TQ), :])
            if extra is not None:
                s = s + extra
            _flash_update(s, wv_ref[pl.ds(off, TQ), :], m_sc, l_sc, acc_sc)

        for dt in range(wt + 1):
            extra = {0: anti4, wt - 1: bias1, wt: bias0}.get(dt)
            if dt == wt:
                win_tile(i, extra)
            else:
                @pl.when(i - wt + dt >= 0)
                def _(dt=dt, extra=extra):
                    win_tile(i - wt + dt, extra)
        o_win = acc_sc[...] / l_sc[...]

        for r in range(R_B):
            hh = g * R_B + r
            rows = slice(r * TQ, (r + 1) * TQ)
            o = (gn[:, 3 * hh:3 * hh + 1] * o_cmp[rows] + gn[:, 3 * hh + 1:3 * hh + 2] * o_sel[rows]
                 + gn[:, 3 * hh + 2:3 * hh + 3] * o_win[rows])
            pieces.append(o[:, g * HD_B:(g + 1) * HD_B])
    o_ref[...] = (jnp.concatenate(pieces, axis=1) * gs_ref[...]).astype(BF16)


def _nsa_prompt(nq_b, ckc, cvc, kvb, gn, gs, bias_p, l, bsz, t):
    nq = t // TQ
    nch = t // CMP_STRIDE
    n_cmp = nch - 1
    n_sel = -(-t // SEL_BLOCK)
    top = min(SEL_TOPN, n_sel)
    nsp = -(-n_sel // LANE) * LANE
    ov = jnp.asarray(_overlap_np(nch, nsp, n_cmp, n_sel), BF16)
    rows = R_B * TQ

    def kv(col):
        return pl.BlockSpec((t, LANE), lambda b, i: (b, col))

    return pl.pallas_call(
        functools.partial(_nsa_prompt_body, n_cmp=n_cmp, n_sel=n_sel, top=top),
        grid=(bsz, nq),
        in_specs=[pl.BlockSpec((TQ, 512), lambda b, i: (b * nq + i, 0)),
                  pl.BlockSpec((nch, LANE), lambda b, i: (b, 0)),
                  pl.BlockSpec((nch, LANE), lambda b, i: (b, 0)),
                  kv(8), kv(9), kv(10), kv(11),
                  pl.BlockSpec((TQ, LANE), lambda b, i: (b * nq + i, 0)),
                  pl.BlockSpec((TQ, 512), lambda b, i: (b * nq + i, 1)),
                  pl.BlockSpec((N_HB, 2, TQ, TQ), lambda b, i: (0, 0, 0, 0)),
                  pl.BlockSpec((nch, nsp), lambda b, i: (0, 0))],
        out_specs=pl.BlockSpec((TQ, 512), lambda b, i: (b * nq + i, 0)),
        out_shape=jax.ShapeDtypeStruct((bsz * t, N_HB * HD_B), BF16),
        scratch_shapes=[pltpu.VMEM((rows, 1), F32), pltpu.VMEM((rows, 1), F32),
                        pltpu.VMEM((rows, LANE), F32)],
        compiler_params=_params("arbitrary", "arbitrary"),
        name="nsa_prompt",
    )(nq_b, ckc, cvc, kvb, kvb, kvb, kvb, gn, gs, bias_p[N_HA:], ov)


def _diff_decode_body(pt_ref, *refs, n_pg, dec_seq, lam_init):
    k_pages = refs[0:n_pg]
    v_pages = refs[n_pg:2 * n_pg]
    kt_ref, vt_ref, q_ref, gs_ref, bias_ref, lam_ref, sub_ref, o_ref, m_sc, l_sc, acc_sc = refs[2 * n_pg:]
    c = pl.program_id(1)
    last = pl.num_programs(1) - 1
    q = q_ref[...]
    blk = lax.broadcasted_iota(jnp.int32, q.shape, 1) >> HD_SHIFT
    qrows = jnp.concatenate([jnp.where(blk == hm, q, 0.0) for hm in range(2 * N_HA)], axis=0).astype(BF16)

    @pl.when(c == 0)
    def _():
        _flash_init(m_sc, l_sc, acc_sc)

    for k in range(n_pg):
        s = _mm_nt(qrows, k_pages[k][...].astype(BF16))
        if k == n_pg - 1:
            s = s + jnp.where(c == last, bias_ref[:, 0:LANE], 0.0)
        _flash_update(s, v_pages[k][...].astype(BF16), m_sc, l_sc, acc_sc)

    @pl.when(c == last)
    def _():
        s = _mm_nt(qrows, kt_ref[...].astype(BF16)) + bias_ref[:, LANE:2 * LANE]
        _flash_update(s, vt_ref[...].astype(BF16), m_sc, l_sc, acc_sc)
        o = acc_sc[...] / l_sc[...]
        lam = _lambda(lam_ref, lam_init)
        outs = []
        for h in range(N_HA):
            r0 = 2 * h * dec_seq
            cols = slice(h * VD_A, (h + 1) * VD_A)
            outs.append(_diff_finish(o[r0:r0 + dec_seq, cols], o[r0 + dec_seq:r0 + 2 * dec_seq, cols],
                                     lam, sub_ref[...], lam_init))
        o_ref[...] = jnp.concatenate(outs, axis=1) * gs_ref[...]


def _page_specs(n_pg, l, shape):
    return [pl.BlockSpec((None, None) + shape, (lambda b, c, pt, k=k: (l, pt[b, c * n_pg + k], 0, 0)))
            for k in range(n_pg)]


def _diff_decode(pt, pool_k, pool_v, k_tail, v_tail, dq, gs, bias_d, l, prep, lam_init):
    db, n_pages = pt.shape
    page = pool_k.shape[2]
    dec_seq = dq.shape[0] // db
    n_pg = min(8, n_pages)
    rows = 2 * N_HA * dec_seq
    wd = bias_d.shape[1]
    grid_spec = pltpu.PrefetchScalarGridSpec(
        num_scalar_prefetch=1,
        grid=(db, n_pages // n_pg),
        in_specs=_page_specs(n_pg, l, (page, 512)) + _page_specs(n_pg, l, (page, 512)) + [
            pl.BlockSpec((None, page, 512), lambda b, c, pt: (b, 0, 0)),
            pl.BlockSpec((None, page, 512), lambda b, c, pt: (b, 0, 0)),
            pl.BlockSpec((dec_seq, 512), lambda b, c, pt: (b, 0)),
            pl.BlockSpec((dec_seq, 512), lambda b, c, pt: (b, 0)),
            pl.BlockSpec((rows, wd), lambda b, c, pt: (0, 0)),
            pl.BlockSpec((None, 4, HD_A), lambda b, c, pt: (l, 0, 0)),
            pl.BlockSpec((None, 1, VD_A), lambda b, c, pt: (l, 0, 0))],
        out_specs=pl.BlockSpec((dec_seq, 512), lambda b, c, pt: (b, 0)),
        scratch_shapes=[pltpu.VMEM((rows, 1), F32), pltpu.VMEM((rows, 1), F32),
                        pltpu.VMEM((rows, N_HA * VD_A), F32)])
    return pl.pallas_call(
        functools.partial(_diff_decode_body, n_pg=n_pg, dec_seq=dec_seq, lam_init=lam_init),
        grid_spec=grid_spec,
        out_shape=jax.ShapeDtypeStruct((db * dec_seq, N_HA * VD_A), F32),
        compiler_params=_params("arbitrary", "arbitrary"),
        name="diff_decode",
    )(pt, *([pool_k] * n_pg), *([pool_v] * n_pg), k_tail, v_tail, dq, gs, bias_d,
      prep["lam"], prep["subln"])


def _cmp_partial_decode_body(pt_ref, *refs, n_pg):
    pages = refs[0:n_pg]
    w1_ref, u_ref = refs[n_pg:]
    a = jnp.concatenate([p[...] for p in pages], axis=0)
    parts = _compress_partial(a, w1_ref)
    for k, part in enumerate(parts):
        u_ref[:, 256 * k:256 * (k + 1)] = part


def _cmp_partial_decode(pt, pool, kv, l, prep):
    db, n_pages = pt.shape
    cpp = pool.shape[2]
    wdt = pool.shape[3]
    n_pg = min(32, n_pages)
    steps = n_pages // n_pg
    grid_spec = pltpu.PrefetchScalarGridSpec(
        num_scalar_prefetch=1,
        grid=(db, steps),
        in_specs=_page_specs(n_pg, l, (cpp, wdt)) + [
            pl.BlockSpec((None, None, wdt, 256), lambda b, c, pt: (l, kv, 0, 0))],
        out_specs=pl.BlockSpec((n_pg * cpp, 1024), lambda b, c, pt: (b * steps + c, 0)))
    return pl.pallas_call(
        functools.partial(_cmp_partial_decode_body, n_pg=n_pg),
        grid_spec=grid_spec,
        out_shape=jax.ShapeDtypeStruct((db * n_pages * cpp, 1024), F32),
        compiler_params=_params("arbitrary", "arbitrary"),
        name="cmp_partial_decode",
    )(pt, *([pool] * n_pg), prep["w1"])


def _nsa_select_decode_body(uk_ref, uv_ref, w1_ref, pos_ref, b1_ref, w2_ref, b2_ref, kn_ref, bd_ref,
                            q_ref, ov_ref, ocmp_o, sel_o, *, n_cmp, n_sel, top, pos0, dec_seq):
    outs = []
    for kv, u_ref in enumerate((uk_ref, uv_ref)):
        parts = [u_ref[:, 256 * k:256 * (k + 1)] for k in range(4)]
        outs.append(_compress_finish(parts, w1_ref.at[kv], pos_ref.at[kv], b1_ref.at[kv],
                                     w2_ref.at[kv], b2_ref.at[kv]))
    ckc = _group_rms(outs[0], bd_ref[...], kn_ref[...])
    cvc = outs[1]
    ncp = ckc.shape[0]
    qf = q_ref[...]
    rows = R_B * dec_seq
    qp = pos0 + lax.broadcasted_iota(jnp.int32, (dec_seq, 1), 0)
    qp4 = pos0 + (lax.broadcasted_iota(jnp.int32, (rows, 1), 0) & (dec_seq - 1))
    nidx = lax.broadcasted_iota(jnp.int32, (rows, ncp), 1)
    valid = (nidx * CMP_STRIDE + (CMP_BLOCK - 1) <= qp4) & (nidx < n_cmp)
    own = qp >> SEL_SHIFT
    for g in range(N_KV):
        qrows = _nsa_q_rows(qf, g)
        p_c, o_cmp = _cmp_attention(qrows, ckc, cvc, valid)
        imp_sum = p_c[0:dec_seq]
        for r in range(1, R_B):
            imp_sum = imp_sum + p_c[r * dec_seq:(r + 1) * dec_seq]
        sel = _select_blocks(imp_sum, ov_ref[...], own, n_sel, top)
        ocmp_o[g * rows:(g + 1) * rows, :] = o_cmp
        sel_o[g * rows:(g + 1) * rows, :] = jnp.concatenate([sel] * R_B, axis=0)


def _nsa_select_decode(uk, uv, nq, l, prep, db, pos0, dec_seq):
    nch = uk.shape[0] // db
    n_cmp = nch - 1
    n_sel = -(-(pos0 + dec_seq) // SEL_BLOCK)
    top = min(SEL_TOPN, n_sel)
    nsp = -(-n_sel // LANE) * LANE
    ov = jnp.asarray(_overlap_np(nch, nsp, n_cmp, n_sel), BF16)
    rows = N_HB * dec_seq
    u_spec = pl.BlockSpec((nch, 1024), lambda b: (b, 0))
    return pl.pallas_call(
        functools.partial(_nsa_select_decode_body, n_cmp=n_cmp, n_sel=n_sel, top=top, pos0=pos0,
                          dec_seq=dec_seq),
        grid=(db,),
        in_specs=[u_spec, u_spec] + _cmp_weight_specs(l, 1) + [
            pl.BlockSpec((dec_seq, 512), lambda b: (b, 0)),
            pl.BlockSpec((nch, nsp), lambda b: (0, 0))],
        out_specs=[pl.BlockSpec((rows, LANE), lambda b: (b, 0)),
                   pl.BlockSpec((rows, nsp), lambda b: (b, 0))],
        out_shape=[jax.ShapeDtypeStruct((db * rows, LANE), F32),
                   jax.ShapeDtypeStruct((db * rows, nsp), F32)],
        compiler_params=_params("arbitrary"),
        name="nsa_select_decode",
    )(uk, uv, *_cmp_weights(prep), nq, ov)


def _nsa_decode_body(pt_ref, *refs, n_pg, n_pages, dec_seq):
    k_pages = refs[0:n_pg]
    v_pages = refs[n_pg:2 * n_pg]
    (kt_ref, vt_ref, wks_ref, wvs_ref, wkt_ref, wvt_ref, q_ref, sel_ref, ocmp_ref, gn_ref, gs_ref,
     bias_ref, o_ref, m_sc, l_sc, acc_sc) = refs[2 * n_pg:]
    c = pl.program_id(1)
    last = pl.num_programs(1) - 1
    page = k_pages[0].shape[0]
    bpp = page // SEL_BLOCK
    qf = q_ref[...]
    qrows = jnp.concatenate([_nsa_q_rows(qf, g) for g in range(N_KV)], axis=0)
    sel = sel_ref[...].astype(BF16)

    @pl.when(c == 0)
    def _():
        _flash_init(m_sc, l_sc, acc_sc)

    for k in range(n_pg):
        s = _mm_nt(qrows, k_pages[k][...].astype(BF16)) + _block_mask(sel, (c * n_pg + k) * bpp, bpp)
        if k == n_pg - 1:
            s = s + jnp.where(c == last, bias_ref[:, 0:page], 0.0)
        _flash_update(s, v_pages[k][...].astype(BF16), m_sc, l_sc, acc_sc)

    @pl.when(c == last)
    def _():
        tail_bias = bias_ref[:, page:2 * page]
        s = _mm_nt(qrows, kt_ref[...].astype(BF16)) + _block_mask(sel, n_pages * bpp, bpp) + tail_bias
        _flash_update(s, vt_ref[...].astype(BF16), m_sc, l_sc, acc_sc)
        o_sel = acc_sc[...] / l_sc[...]
        s_w = _mm_nt(qrows, wks_ref[...].astype(BF16)) + bias_ref[:, 2 * page:]
        s_t = _mm_nt(qrows, wkt_ref[...].astype(BF16)) + tail_bias
        m = jnp.maximum(jnp.max(s_w, axis=-1, keepdims=True), jnp.max(s_t, axis=-1, keepdims=True))
        p_w = jnp.exp(s_w - m)
        p_t = jnp.exp(s_t - m)
        den = jnp.sum(p_w, axis=-1, keepdims=True) + jnp.sum(p_t, axis=-1, keepdims=True)
        o_win = (_mm(p_w.astype(BF16), wvs_ref[...].astype(BF16))
                 + _mm(p_t.astype(BF16), wvt_ref[...].astype(BF16))) / den
        o_cmp = ocmp_ref[...]
        gn = gn_ref[...]
        pieces = []
        for hh in range(N_HB):
            g = hh // R_B
            rows = slice(hh * dec_seq, (hh + 1) * dec_seq)
            o = (gn[:, 3 * hh:3 * hh + 1] * o_cmp[rows] + gn[:, 3 * hh + 1:3 * hh + 2] * o_sel[rows]
                 + gn[:, 3 * hh + 2:3 * hh + 3] * o_win[rows])
            pieces.append(o[:, g * HD_B:(g + 1) * HD_B])
        o_ref[...] = jnp.concatenate(pieces, axis=1) * gs_ref[...]


def _nsa_decode(pt, pool_k, pool_v, k_tail, v_tail, win_k, win_v, wk_tail, wv_tail, nq, sel, ocmp,
                gn, gs, bias_d, l):
    db, n_pages = pt.shape
    page = pool_k.shape[2]
    dec_seq = nq.shape[0] // db
    n_pg = min(8, n_pages)
    rows = N_HB * dec_seq
    wd = bias_d.shape[1]
    wb = win_k.shape[2]
    nsp = sel.shape[1]

    def per_b(r, w):
        return pl.BlockSpec((None, r, w), lambda b, c, pt: (b, 0, 0))

    grid_spec = pltpu.PrefetchScalarGridSpec(
        num_scalar_prefetch=1,
        grid=(db, n_pages // n_pg),
        in_specs=_page_specs(n_pg, l, (page, LANE)) + _page_specs(n_pg, l, (page, LANE)) + [
            per_b(page, LANE), per_b(page, LANE),
            pl.BlockSpec((None, None, wb, LANE), lambda b, c, pt: (l, b, 0, 0)),
            pl.BlockSpec((None, None, wb, LANE), lambda b, c, pt: (l, b, 0, 0)),
            per_b(page, LANE), per_b(page, LANE),
            pl.BlockSpec((dec_seq, 512), lambda b, c, pt: (b, 0)),
            pl.BlockSpec((rows, nsp), lambda b, c, pt: (b, 0)),
            pl.BlockSpec((rows, LANE), lambda b, c, pt: (b, 0)),
            pl.BlockSpec((dec_seq, LANE), lambda b, c, pt: (b, 0)),
            pl.BlockSpec((dec_seq, 512), lambda b, c, pt: (b, 1)),
            pl.BlockSpec((rows, wd), lambda b, c, pt: (0, 0))],
        out_specs=pl.BlockSpec((dec_seq, 512), lambda b, c, pt: (b, 0)),
        scratch_shapes=[pltpu.VMEM((rows, 1), F32), pltpu.VMEM((rows, 1), F32),
                        pltpu.VMEM((rows, LANE), F32)])
    return pl.pallas_call(
        functools.partial(_nsa_decode_body, n_pg=n_pg, n_pages=n_pages, dec_seq=dec_seq),
        grid_spec=grid_spec,
        out_shape=jax.ShapeDtypeStruct((db * dec_seq, N_HB * HD_B), F32),
        compiler_params=_params("arbitrary", "arbitrary"),
        name="nsa_decode",
    )(pt, *([pool_k] * n_pg), *([pool_v] * n_pg), k_tail, v_tail, win_k, win_v, wk_tail, wv_tail,
      nq, sel, ocmp, gn, gs, bias_d)


def _prepare(norm_g, w_in, b_in, diff_q_norm, diff_k_norm, diff_lambda, diff_subln, nsa_q_norm,
             nsa_k_norm, cmp_pos, cmp_w1, cmp_b1, cmp_w2, cmp_b2, w_branch_a, w_branch_b, w_out):
    depth, d, _ = w_in.shape
    c_gm = C_GN + N_GN
    pad = LANE - N_GN

    def reorder(a):
        z = jnp.zeros(a.shape[:-1] + (pad,), a.dtype)
        return jnp.concatenate([a[..., :C_GN], a[..., c_gm:c_gm + 2 * d], a[..., C_GN:c_gm], z], axis=-1)

    scale_a = HD_A ** -0.5
    scale_b = HD_B ** -0.5
    gains = jnp.zeros((depth, 8, 512), F32)
    gains = gains.at[:, 0].set(jnp.tile(diff_q_norm, (1, 8)) * scale_a)
    gains = gains.at[:, 1].set(jnp.tile(diff_k_norm, (1, 8)))
    gains = gains.at[:, 2].set(jnp.tile(nsa_q_norm, (1, 8)) * scale_b)
    gains = gains.at[:, 3, 0:128].set(jnp.tile(nsa_k_norm[:, 1], (1, 2)))
    gains = gains.at[:, 3, 128:256].set(jnp.tile(nsa_k_norm[:, 2], (1, 2)))
    zero = jnp.zeros_like(cmp_w2)
    w2bd = jnp.concatenate([jnp.concatenate([cmp_w2, zero], axis=-1),
                            jnp.concatenate([zero, cmp_w2], axis=-1)], axis=-2)
    return {
        "norm_g": norm_g.reshape(depth, 1, d),
        "w_in": reorder(w_in).astype(BF16),
        "b_in": reorder(b_in).reshape(depth, 1, -1),
        "gains": gains,
        "bd512": jnp.asarray(_blockdiag_np(512, HD_A), BF16),
        "bd128": jnp.asarray(_blockdiag_np(LANE, HD_B), BF16),
        "lam": diff_lambda,
        "subln": diff_subln.reshape(depth, 1, VD_A),
        "w1": cmp_w1.astype(BF16),
        "pos": cmp_pos.reshape(depth, 2, 1, CMP_BLOCK * HD_B),
        "b1": cmp_b1.reshape(depth, 2, 1, -1),
        "w2bd": w2bd.astype(BF16),
        "b2": jnp.tile(cmp_b2, (1, 1, 2)).reshape(depth, 2, 1, LANE),
        "cmp_kn": jnp.tile(nsa_k_norm[:, 0], (1, 2)).reshape(depth, 1, LANE),
        "w_a": w_branch_a.astype(BF16),
        "w_b": w_branch_b.astype(BF16),
        "w_out": w_out.astype(BF16),
    }


def kernel(x_prompt, x_sample, cache_diff_k, cache_diff_v, cache_cmp_k, cache_cmp_v, cache_sel_k, cache_sel_v,
           state_win_k, state_win_v, page_table, rel_bias, norm_g, w_in, b_in, diff_q_norm, diff_k_norm,
           diff_lambda, diff_subln, nsa_q_norm, nsa_k_norm, cmp_pos, cmp_w1, cmp_b1, cmp_w2, cmp_b2,
           w_branch_a, w_branch_b, w_out):
    bsz, t, d = x_prompt.shape
    db, dec_seq, _ = x_sample.shape
    depth, n_pool, page = cache_diff_k.shape[:3]
    n_pages = page_table.shape[1]
    past = n_pages * page
    wb = state_win_k.shape[2]
    assert t % TQ == 0 and WINDOW % TQ == 0 and page == LANE and wb == WINDOW and past % SEL_BLOCK == 0
    assert dec_seq <= CMP_STRIDE and dec_seq % 8 == 0

    prep = _prepare(norm_g, w_in, b_in, diff_q_norm, diff_k_norm, diff_lambda, diff_subln, nsa_q_norm,
                    nsa_k_norm, cmp_pos, cmp_w1, cmp_b1, cmp_w2, cmp_b2, w_branch_a, w_branch_b, w_out)
    bias_p, bias_d = _bias_tiles(rel_bias, jnp.asarray(_prompt_bucket_tiles()),
                                 jnp.asarray(_decode_bucket_tiles(dec_seq, page, wb)))
    wd = bias_d.shape[-1]
    bias_d_diff = jnp.repeat(bias_d[:N_HA], 2, axis=0).reshape(2 * N_HA * dec_seq, wd)
    bias_d_nsa = bias_d[N_HA:].reshape(N_HB * dec_seq, wd)

    pool_dk = cache_diff_k.reshape(depth, n_pool, page, 512)
    pool_dv = cache_diff_v.reshape(depth, n_pool, page, 512)
    cpp = page // CMP_STRIDE
    pool_ck = cache_cmp_k.reshape(depth, n_pool, cpp, CMP_STRIDE * LANE)
    pool_cv = cache_cmp_v.reshape(depth, n_pool, cpp, CMP_STRIDE * LANE)
    pool_sk = cache_sel_k.reshape(depth, n_pool, page, LANE)
    pool_sv = cache_sel_v.reshape(depth, n_pool, page, LANE)
    win_k = state_win_k.reshape(depth, db, wb, LANE)
    win_v = state_win_v.reshape(depth, db, wb, LANE)

    def tail(a):
        return jnp.pad(a.reshape(db, dec_seq, a.shape[-1]), ((0, 0), (0, page - dec_seq), (0, 0)))

    xp = x_prompt.reshape(bsz * t, d)
    xs = x_sample.reshape(db * dec_seq, d)
    rows_p, rows_s = [], []
    for l in range(depth):
        lam_init = 0.8 - 0.6 * math.exp(-0.3 * l)

        (dq_b, dk, dv, nq_b, ck, cv, sk, sv, wk, wv, kvb, gs, gm, gn) = _in_proj(xp, l, prep)
        oa = _diff_prompt(dq_b, kvb, gs, bias_p, l, prep, bsz, t, lam_init)
        ckc, cvc = _compress_prompt(ck, cv, l, prep, bsz, t)
        ob = _nsa_prompt(nq_b, ckc, cvc, kvb, gn, gs, bias_p, l, bsz, t)
        xp = _out_proj(xp, oa, ob, gm, l, prep)
        nw = min(WINDOW, t)
        rows_p.append((dk.reshape(bsz, t, N_HA, 2, HD_A), dv.reshape(bsz, t, N_HA, VD_A),
                       ck.reshape(bsz, t, N_KV, HD_B), cv.reshape(bsz, t, N_KV, HD_B),
                       sk.reshape(bsz, t, N_KV, HD_B), sv.reshape(bsz, t, N_KV, HD_B),
                       wk.reshape(bsz, t, N_KV, HD_B)[:, t - nw:], wv.reshape(bsz, t, N_KV, HD_B)[:, t - nw:]))

        (dq_b, dk, dv, nq_b, ck, cv, sk, sv, wk, wv, kvb, gs, gm, gn) = _in_proj(xs, l, prep)
        oa = _diff_decode(page_table, pool_dk, pool_dv, tail(dk), tail(dv), dq_b.astype(F32), gs,
                          bias_d_diff, l, prep, lam_init)
        uk = _cmp_partial_decode(page_table, pool_ck, 0, l, prep)
        uv = _cmp_partial_decode(page_table, pool_cv, 1, l, prep)
        nq = nq_b.astype(F32)
        ocmp, sel = _nsa_select_decode(uk, uv, nq, l, prep, db, past, dec_seq)
        ob = _nsa_decode(page_table, pool_sk, pool_sv, tail(sk), tail(sv), win_k, win_v, tail(wk), tail(wv),
                         nq, sel, ocmp, gn, gs, bias_d_nsa, l)
        xs = _out_proj(xs, oa, ob, gm, l, prep)
        wk3 = wk.reshape(db, dec_seq, N_KV, HD_B)
        wv3 = wv.reshape(db, dec_seq, N_KV, HD_B)
        buf_k = jnp.concatenate([state_win_k[l], wk3], axis=1)
        buf_v = jnp.concatenate([state_win_v[l], wv3], axis=1)
        nw = min(WINDOW, buf_k.shape[1])
        rows_s.append((dk.reshape(db, dec_seq, N_HA, 2, HD_A), dv.reshape(db, dec_seq, N_HA, VD_A),
                       ck.reshape(db, dec_seq, N_KV, HD_B), cv.reshape(db, dec_seq, N_KV, HD_B),
                       sk.reshape(db, dec_seq, N_KV, HD_B), sv.reshape(db, dec_seq, N_KV, HD_B),
                       buf_k[:, -nw:], buf_v[:, -nw:]))

    outs_p = [jnp.stack(r, axis=0) for r in zip(*rows_p)]
    outs_s = [jnp.stack(r, axis=0) for r in zip(*rows_s)]
    return (xp.reshape(bsz, t, d), xs.reshape(db, dec_seq, d), *outs_p, *outs_s)
```

```python
import functools
import math

import numpy as np
import jax
import jax.numpy as jnp
from jax import lax
from jax.experimental import pallas as pl
from jax.experimental.pallas import tpu as pltpu

F32 = jnp.float32
BF16 = jnp.bfloat16

N_HA, HD_A, VD_A = 4, 64, 128
N_HB, N_KV, HD_B = 8, 2, 64
R_B = N_HB // N_KV
CMP_STRIDE, CMP_BLOCK = 16, 32
SEL_BLOCK, SEL_TOPN, WINDOW = 64, 16, 512
NUM_BUCKETS, MAX_DISTANCE = 32, 128
EPS, NEG, FORCE = 1e-6, -1e30, 1e4
MASKED_BUCKET = NUM_BUCKETS
SEL_SHIFT = SEL_BLOCK.bit_length() - 1
HD_SHIFT = HD_A.bit_length() - 1

LANE = 128
TQ = 256
VMEM_LIMIT = 56 * 1024 * 1024
DIFF_PAGES_PER_STEP = 8
SEL_PAGES_PER_STEP = 16
CMP_PAGES_PER_STEP = 64

C_DK, C_DV, C_NQ, C_KV, C_GA, C_GN = 512, 1024, 1536, 2048, 2816, 3840
N_GN = N_HB * 3
N_T = 512 + 768


def _mm(a, b):
    return jnp.dot(a, b, preferred_element_type=F32)


def _mm_nt(a, b):
    return lax.dot_general(a, b, (((1,), (1,)), ((), ())), preferred_element_type=F32)


def _split(x):
    hi = x.astype(BF16)
    lo = (x - hi.astype(F32)).astype(BF16)
    return hi, lo


def _sigmoid(z):
    return 1.0 / (1.0 + jnp.exp(-z))


def _group_rms(z, bd, gain):
    hi, lo = _split(z * z)
    ss = _mm(hi, bd) + _mm(lo, bd)
    return z * lax.rsqrt(ss * (1.0 / HD_A) + EPS) * gain


def _group_rms_t(z, gain):
    r, n = z.shape
    z3 = z.reshape(r // HD_A, HD_A, n)
    ss = jnp.sum(z3 * z3, axis=1, keepdims=True)
    return (z3 * lax.rsqrt(ss * (1.0 / HD_A) + EPS)).reshape(r, n) * gain


def _params(*sem):
    return pltpu.CompilerParams(dimension_semantics=sem, vmem_limit_bytes=VMEM_LIMIT)


def _lane_fold(x, op):
    out = x[:, 0:LANE]
    for k in range(1, x.shape[1] // LANE):
        out = op(out, x[:, k * LANE:(k + 1) * LANE])
    return out


def _score_tile(s, slot, s_sc, mp_sc):
    s_sc[slot] = s
    mp_sc[...] = jnp.maximum(mp_sc[...], _lane_fold(s, jnp.maximum))


def _prob_tile(slot, m, pv, s_sc, lp_sc, acc_sc):
    p = jnp.exp(s_sc[slot] - m)
    lp_sc[...] += _lane_fold(p, jnp.add)
    acc_sc[...] += pv(p.astype(BF16))


def _softmax_begin(mp_sc):
    mp_sc[...] = jnp.full(mp_sc.shape, -3e38, F32)


def _softmax_mid(mp_sc, lp_sc, acc_sc):
    lp_sc[...] = jnp.zeros(lp_sc.shape, F32)
    acc_sc[...] = jnp.zeros(acc_sc.shape, F32)
    return jnp.max(mp_sc[...], axis=-1, keepdims=True)


def _softmax_end(lp_sc, acc_sc):
    return acc_sc[...] / jnp.sum(lp_sc[...], axis=-1, keepdims=True)


def _flash_update(s, pv, m_ref, l_ref, acc_ref):
    m_old = m_ref[...]
    m_new = jnp.maximum(m_old, jnp.max(s, axis=-1, keepdims=True))
    alpha = jnp.exp(m_old - m_new)
    p = jnp.exp(s - m_new)
    l_ref[...] = alpha * l_ref[...] + jnp.sum(p, axis=-1, keepdims=True)
    acc_ref[...] = alpha * acc_ref[...] + pv(p.astype(BF16))
    m_ref[...] = m_new


def _flash_init(m_ref, l_ref, acc_ref):
    m_ref[...] = jnp.full(m_ref.shape, -3e38, F32)
    l_ref[...] = jnp.zeros(l_ref.shape, F32)
    acc_ref[...] = jnp.zeros(acc_ref.shape, F32)


def _bucket_np(rel):
    n = np.maximum(rel, 0)
    exact = NUM_BUCKETS // 2
    nf = np.maximum(n, 1).astype(np.float32)
    large = exact + (np.log(nf / np.float32(exact)) / np.float32(math.log(MAX_DISTANCE / exact))
                     * np.float32(NUM_BUCKETS - exact)).astype(np.int32)
    large = np.minimum(large, NUM_BUCKETS - 1)
    return np.where(n < exact, n, large).astype(np.int32)


def _prompt_bucket_tiles():
    r = np.arange(TQ)[:, None]
    c = np.arange(TQ)[None, :]
    d0 = np.where(r - c >= 0, _bucket_np(r - c), MASKED_BUCKET)
    d1 = _bucket_np(TQ + r - c)
    return np.stack([d0, d1]).astype(np.int32)


def _decode_bucket_tiles(dec_seq, page, wb):
    q = np.arange(dec_seq)[:, None]
    c = np.arange(page)[None, :]
    last = _bucket_np(page + q - c)
    tail = np.where((c < dec_seq) & (c <= q), _bucket_np(q - c), MASKED_BUCKET)
    i = np.arange(wb)[None, :]
    relw = wb + q - i
    win = np.where(relw < WINDOW, _bucket_np(relw), MASKED_BUCKET)
    return np.concatenate([last, tail, win], axis=1).astype(np.int32)


def _overlap_np(n_rows, n_cols, n_cmp, n_sel):
    n = np.arange(n_rows)[:, None]
    j = np.arange(n_cols)[None, :]
    ov = (n * CMP_STRIDE < j * SEL_BLOCK + SEL_BLOCK) & (n * CMP_STRIDE + CMP_BLOCK > j * SEL_BLOCK)
    ov &= (n < n_cmp) & (j < n_sel)
    return ov.astype(np.float32)


def _blockdiag_np(n, group):
    i = np.arange(n)
    return (i[:, None] // group == i[None, :] // group).astype(np.float32)


def _chunk_perm_np(page):
    rows = np.arange(page)
    tok = (rows % (page // CMP_STRIDE)) * CMP_STRIDE + rows // (page // CMP_STRIDE)
    return (tok[:, None] == np.arange(page)[None, :]).astype(np.float32)


def _bias_body(tbl_ref, idxp_ref, idxd_ref, p_ref, d_ref):
    h = pl.program_id(0)
    far = tbl_ref[h, NUM_BUCKETS - 1]

    def expand(idx):
        acc = jnp.zeros(idx.shape, F32)
        for b in range(NUM_BUCKETS - 1):
            acc = jnp.where(idx == b, tbl_ref[h, b] - far, acc)
        return jnp.where(idx == MASKED_BUCKET, NEG, acc)

    p_ref[0] = expand(idxp_ref[0])
    p_ref[1] = expand(idxp_ref[1])
    d_ref[...] = expand(idxd_ref[...])


def _bias_tiles(rel_bias, idx_p, idx_d):
    nh = rel_bias.shape[0]
    dq, wd = idx_d.shape
    return pl.pallas_call(
        _bias_body,
        grid=(nh,),
        in_specs=[pl.BlockSpec(memory_space=pltpu.SMEM),
                  pl.BlockSpec((2, TQ, TQ), lambda h: (0, 0, 0)),
                  pl.BlockSpec((dq, wd), lambda h: (0, 0))],
        out_specs=[pl.BlockSpec((None, 2, TQ, TQ), lambda h: (h, 0, 0, 0)),
                   pl.BlockSpec((None, dq, wd), lambda h: (h, 0, 0))],
        out_shape=[jax.ShapeDtypeStruct((nh, 2, TQ, TQ), F32),
                   jax.ShapeDtypeStruct((nh, dq, wd), F32)],
        compiler_params=_params("arbitrary"),
        name="bias_tiles",
    )(rel_bias, idx_p, idx_d)


def _normed_input(x_ref, g_ref):
    x = x_ref[...]
    ms = jnp.mean(x * x, axis=-1, keepdims=True)
    return (x * lax.rsqrt(ms + EPS) * g_ref[...]).astype(BF16)


def _in_proj_sample_body(x_ref, g_ref, w_ref, b_ref, bd_ref, gain_ref,
                         dq_o, dk_o, dv_o, nq_o, kv_o, gs_o, gm_o, gn_o, *, d_model):
    h = _normed_input(x_ref, g_ref)

    def seg(a, b):
        return _mm(h, w_ref[:, a:b]) + b_ref[:, a:b]

    bd = bd_ref[...]
    bd2 = bd[0:LANE, 0:LANE]
    dq_o[...] = _group_rms(seg(0, C_DK), bd, gain_ref[0:1, :])
    dk_o[...] = _group_rms(seg(C_DK, C_DV), bd, gain_ref[1:2, :])
    dv_o[...] = seg(C_DV, C_NQ)
    nq_o[...] = _group_rms(seg(C_NQ, C_KV), bd, gain_ref[2:3, :])
    z = seg(C_KV, C_GA)
    kv_o[:, 0:256] = z[:, 0:256]
    kv_o[:, 256:384] = _group_rms(z[:, 256:384], bd2, gain_ref[3:4, 0:128])
    kv_o[:, 384:512] = z[:, 384:512]
    kv_o[:, 512:640] = _group_rms(z[:, 512:640], bd2, gain_ref[3:4, 128:256])
    kv_o[:, 640:768] = z[:, 640:768]
    z = seg(C_GA, C_GN)
    gs_o[...] = z * _sigmoid(z)
    gm_o[...] = _sigmoid(seg(C_GN, C_GN + 2 * d_model))
    gn_o[...] = _sigmoid(seg(C_GN + 2 * d_model, C_GN + 2 * d_model + LANE))


def _in_proj_sample(x2d, l, prep):
    n, d = x2d.shape
    tm = min(256, n)
    c = prep["w_in"].shape[-1]

    def row(w):
        return pl.BlockSpec((tm, w), lambda i: (i, 0))

    def lay(*shape):
        return pl.BlockSpec((None,) + shape, lambda i: (l,) + (0,) * len(shape))

    widths = [512, 512, 512, 512, 768, 1024, 2 * d, LANE]
    return pl.pallas_call(
        functools.partial(_in_proj_sample_body, d_model=d),
        grid=(n // tm,),
        in_specs=[row(d), lay(1, d), lay(d, c), lay(1, c),
                  pl.BlockSpec((512, 512), lambda i: (0, 0)), lay(8, 512)],
        out_specs=[row(w) for w in widths],
        out_shape=[jax.ShapeDtypeStruct((n, w), F32) for w in widths],
        compiler_params=_params("arbitrary"),
        name="in_proj_sample",
    )(x2d, prep["norm_g"], prep["w_in"], prep["b_in"], prep["bd512"], prep["gains"])


N_STACKED = 8


def _in_proj_prompt_body(*refs, d_model, n_alias):
    x_ref, g_ref, w_ref, b_ref, wt_ref, bt_ref, bd_ref, gain_ref, gaint_ref = refs[0:9]
    (dq_o, nq_o, dvb_o, ckn_o, cvn_o, gs_o, gm_o, gn_o, ktb_o, kvtb_o,
     dkt_o, dv_o, ckt_o, cvt_o, skt_o, svt_o, wkt_o, wvt_o) = refs[9 + n_alias:]
    h = _normed_input(x_ref, g_ref)
    tm = h.shape[0]

    def seg(a, b):
        return _mm(h, w_ref[:, a:b]) + b_ref[:, a:b]

    def seg_t(a, b):
        return _mm_nt(wt_ref[a:b, :], h) + bt_ref[a:b, :]

    bd = bd_ref[...]
    dq_o[...] = _group_rms(seg(0, C_DK), bd, gain_ref[0:1, :]).astype(BF16)
    nq_o[...] = _group_rms(seg(C_NQ, C_KV), bd, gain_ref[2:3, :]).astype(BF16)
    y = seg(C_DV, C_NQ)
    dvb_o[...] = y.astype(BF16)
    for hh in range(N_HA):
        dv_o[pl.ds(hh, tm, stride=N_HA), :] = y[:, hh * VD_A:(hh + 1) * VD_A]
    z = seg(C_KV, C_KV + 256)
    ckn_o[...] = z[:, 0:128]
    cvn_o[...] = z[:, 128:256]
    z = seg(C_GA, C_GN)
    gs_o[...] = z * _sigmoid(z)
    gm_o[...] = _sigmoid(seg(C_GN, C_GN + 2 * d_model))
    gn_o[...] = _sigmoid(seg(C_GN + 2 * d_model, C_GN + 2 * d_model + LANE))

    yt = _group_rms_t(seg_t(0, 512), gaint_ref[0:512, :])
    dkt_o[...] = yt
    ktb_o[...] = yt.astype(BF16)
    zt = seg_t(512, N_T)
    ckt_o[...] = zt[0:128]
    cvt_o[...] = zt[128:256]
    sk = _group_rms_t(zt[256:384], gaint_ref[512:640, :])
    sv = zt[384:512]
    wk = _group_rms_t(zt[512:640], gaint_ref[640:768, :])
    wv = zt[640:768]
    skt_o[...] = sk
    svt_o[...] = sv
    wkt_o[...] = wk
    wvt_o[...] = wv
    kvtb_o[0:128, :] = sk.astype(BF16)
    kvtb_o[128:256, :] = sv.astype(BF16)
    kvtb_o[256:384, :] = wk.astype(BF16)
    kvtb_o[384:512, :] = wv.astype(BF16)


def _in_proj_prompt(x2d, l, prep, bsz, t, depth, stacked):
    n, d = x2d.shape
    tm = TQ
    nt = t // tm
    c = prep["w_in"].shape[-1]
    nw = min(WINDOW, t)
    nwt = nw // tm

    def row(w):
        return pl.BlockSpec((tm, w), lambda i: (i, 0))

    def lay(*shape):
        return pl.BlockSpec((None,) + shape, lambda i: (l,) + (0,) * len(shape))

    def tile_t(rows):
        return pl.BlockSpec((None, None, rows, tm), lambda i: (i // nt, i % nt, 0, 0))

    def stack_t(rows):
        return pl.BlockSpec((None, None, rows, tm), lambda i: (l, i // nt, 0, i % nt))

    win_t = pl.BlockSpec((None, None, LANE, tm),
                         lambda i: (l, i // nt, 0, jnp.maximum(i % nt - (nt - nwt), 0)))
    out_specs = [row(512), row(512), row(512), row(LANE), row(LANE), row(1024), row(2 * d), row(LANE),
                 tile_t(512), tile_t(512),
                 stack_t(512), pl.BlockSpec((None, tm * N_HA, VD_A), lambda i: (l, i, 0)),
                 stack_t(LANE), stack_t(LANE), stack_t(LANE), stack_t(LANE), win_t, win_t]
    kv_t = jax.ShapeDtypeStruct((depth, bsz, LANE, t), F32)
    win_shape = jax.ShapeDtypeStruct((depth, bsz, LANE, nw), F32)
    out_shape = [jax.ShapeDtypeStruct((n, 512), BF16), jax.ShapeDtypeStruct((n, 512), BF16),
                 jax.ShapeDtypeStruct((n, 512), BF16), jax.ShapeDtypeStruct((n, LANE), F32),
                 jax.ShapeDtypeStruct((n, LANE), F32), jax.ShapeDtypeStruct((n, 1024), F32),
                 jax.ShapeDtypeStruct((n, 2 * d), F32), jax.ShapeDtypeStruct((n, LANE), F32),
                 jax.ShapeDtypeStruct((bsz, nt, 512, tm), BF16), jax.ShapeDtypeStruct((bsz, nt, 512, tm), BF16),
                 jax.ShapeDtypeStruct((depth, bsz, 512, t), F32),
                 jax.ShapeDtypeStruct((depth, n * N_HA, VD_A), F32),
                 kv_t, kv_t, kv_t, kv_t, win_shape, win_shape]
    n_alias = 0 if stacked is None else N_STACKED
    n_in = 9
    n_plain = len(out_shape) - N_STACKED
    aliases = {n_in + k: n_plain + k for k in range(n_alias)}
    return pl.pallas_call(
        functools.partial(_in_proj_prompt_body, d_model=d, n_alias=n_alias),
        grid=(n // tm,),
        in_specs=[row(d), lay(1, d), lay(d, c), lay(1, c), lay(N_T, d), lay(N_T, 1),
                  pl.BlockSpec((512, 512), lambda i: (0, 0)), lay(8, 512), lay(N_T, 1)]
                 + [pl.BlockSpec(memory_space=pl.ANY)] * n_alias,
        out_specs=out_specs,
        out_shape=out_shape,
        input_output_aliases=aliases,
        compiler_params=_params("arbitrary"),
        name="in_proj_prompt",
    )(x2d, prep["norm_g"], prep["w_in"], prep["b_in"], prep["w_t"], prep["b_t"], prep["bd512"],
      prep["gains"], prep["gains_t"], *(stacked or ()))


def _out_proj_body(x_ref, oa_ref, ob_ref, gm_ref, wa_ref, wb_ref, wo_ref, y_ref, *, d_model):
    ya = _mm(oa_ref[...].astype(BF16), wa_ref[...])
    yb = _mm(ob_ref[...].astype(BF16), wb_ref[...])
    gm = gm_ref[...]
    m = gm[:, 0:d_model] * ya + gm[:, d_model:2 * d_model] * yb
    y_ref[...] = x_ref[...] + _mm(m.astype(BF16), wo_ref[...])


def _out_proj(x2d, oa, ob, gm, l, prep):
    n, d = x2d.shape
    tm = min(512, n)

    def row(w):
        return pl.BlockSpec((tm, w), lambda i: (i, 0))

    def lay(*shape):
        return pl.BlockSpec((None,) + shape, lambda i: (l,) + (0,) * len(shape))

    return pl.pallas_call(
        functools.partial(_out_proj_body, d_model=d),
        grid=(n // tm,),
        in_specs=[row(d), row(512), row(512), row(2 * d), lay(512, d), lay(512, d), lay(d, d)],
        out_specs=row(d),
        out_shape=jax.ShapeDtypeStruct((n, d), F32),
        compiler_params=_params("arbitrary"),
        name="out_proj",
    )(x2d, oa, ob, gm, prep["w_a"], prep["w_b"], prep["w_out"])


def _lambda(lam_ref, lam_init):
    lv = lam_ref[...]
    a = jnp.sum(lv[0:1] * lv[1:2], axis=-1, keepdims=True)
    b = jnp.sum(lv[2:3] * lv[3:4], axis=-1, keepdims=True)
    return jnp.exp(a) - jnp.exp(b) + lam_init


def _diff_finish(o1, o2, lam, sub, lam_init):
    o = o1 - lam * o2
    ms = jnp.mean(o * o, axis=-1, keepdims=True)
    return o * lax.rsqrt(ms + EPS) * sub * (1.0 - lam_init)


def _diff_prompt_body(q_ref, kt_ref, v_ref, gs_ref, bias_ref, lam_ref, sub_ref, o_ref,
                      s_sc, mp_sc, lp_sc, acc_sc, *, lam_init):
    i = pl.program_id(2)
    q = q_ref[...]
    lane = lax.broadcasted_iota(jnp.int32, q.shape, 1)
    zero = jnp.zeros_like(q)
    qq = jnp.concatenate([jnp.where(lane < HD_A, q, zero), jnp.where(lane >= HD_A, q, zero)], axis=0)

    _softmax_begin(mp_sc)

    def far(j, carry):
        _score_tile(_mm(qq, kt_ref[j]), j, s_sc, mp_sc)
        return carry

    lax.fori_loop(0, jnp.maximum(i - 1, 0), far, 0)

    @pl.when(i >= 1)
    def _():
        b1 = bias_ref[1]
        _score_tile(_mm(qq, kt_ref[i - 1]) + jnp.concatenate([b1, b1], axis=0), i - 1, s_sc, mp_sc)

    b0 = bias_ref[0]
    _score_tile(_mm(qq, kt_ref[i]) + jnp.concatenate([b0, b0], axis=0), i, s_sc, mp_sc)

    m = _softmax_mid(mp_sc, lp_sc, acc_sc)

    def probs(j, carry):
        off = pl.multiple_of(j * TQ, TQ)
        _prob_tile(j, m, lambda p: _mm(p, v_ref[pl.ds(off, TQ), :]), s_sc, lp_sc, acc_sc)
        return carry

    lax.fori_loop(0, i + 1, probs, 0)
    o = _softmax_end(lp_sc, acc_sc)
    y = _diff_finish(o[0:TQ], o[TQ:2 * TQ], _lambda(lam_ref, lam_init), sub_ref[...], lam_init)
    o_ref[...] = (y * gs_ref[...]).astype(BF16)


def _diff_prompt(dq_b, ktb, dv_b, gs, bias_p, l, prep, bsz, t, lam_init):
    nt = t // TQ
    return pl.pallas_call(
        functools.partial(_diff_prompt_body, lam_init=lam_init),
        grid=(bsz, N_HA, nt),
        in_specs=[pl.BlockSpec((TQ, LANE), lambda b, h, i: (b * nt + i, h)),
                  pl.BlockSpec((None, nt, 2 * HD_A, TQ), lambda b, h, i: (b, 0, h, 0)),
                  pl.BlockSpec((t, LANE), lambda b, h, i: (b, h)),
                  pl.BlockSpec((TQ, LANE), lambda b, h, i: (b * nt + i, h)),
                  pl.BlockSpec((None, 2, TQ, TQ), lambda b, h, i: (h, 0, 0, 0)),
                  pl.BlockSpec((None, 4, HD_A), lambda b, h, i: (l, 0, 0)),
                  pl.BlockSpec((None, 1, VD_A), lambda b, h, i: (l, 0, 0))],
        out_specs=pl.BlockSpec((TQ, LANE), lambda b, h, i: (b * nt + i, h)),
        out_shape=jax.ShapeDtypeStruct((bsz * t, N_HA * VD_A), BF16),
        scratch_shapes=[pltpu.VMEM((nt, 2 * TQ, TQ), F32), pltpu.VMEM((2 * TQ, LANE), F32),
                        pltpu.VMEM((2 * TQ, LANE), F32), pltpu.VMEM((2 * TQ, VD_A), F32)],
        compiler_params=_params("arbitrary", "arbitrary", "arbitrary"),
        name="diff_prompt",
    )(dq_b, ktb, dv_b, gs, bias_p, prep["lam"], prep["subln"])


def _deinterleave(a):
    lane = lax.broadcasted_iota(jnp.int32, (a.shape[0], LANE), 1)
    low = lane < HD_B
    g0, g1 = [], []
    for k in range(CMP_STRIDE // 2):
        xe = a[:, 2 * LANE * k:2 * LANE * k + LANE]
        xo = a[:, 2 * LANE * k + LANE:2 * LANE * (k + 1)]
        g0.append(jnp.where(low, xe, pltpu.roll(xo, HD_B, 1)))
        g1.append(jnp.where(low, pltpu.roll(xe, HD_B, 1), xo))
    return jnp.concatenate(g0, axis=1).astype(BF16), jnp.concatenate(g1, axis=1).astype(BF16)


def _compress_partial(a, w1_ref):
    a0, a1 = _deinterleave(a)
    half = CMP_STRIDE * HD_B
    w1a = w1_ref[0:half, :]
    w1b = w1_ref[half:2 * half, :]
    return _mm(a0, w1a), _mm(a1, w1a), _mm(a0, w1b), _mm(a1, w1b)


def _compress_finish(parts, w1_ref, pos_ref, b1_ref, w2_ref, b2_ref):
    ha0, ha1, hb0, hb1 = parts
    m = ha0.shape[0]
    pos = jnp.broadcast_to(pos_ref[...], (8, pos_ref.shape[-1])).astype(BF16)
    c = _mm(pos, w1_ref[...])[0:1] + b1_ref[...]
    h0 = ha0 + pltpu.roll(hb0, m - 1, 0) + c
    h1 = ha1 + pltpu.roll(hb1, m - 1, 0) + c
    hid = jnp.concatenate([h0 * _sigmoid(h0), h1 * _sigmoid(h1)], axis=1).astype(BF16)
    return _mm(hid, w2_ref[...]) + b2_ref[...]


def _compress_prompt_body(ak_ref, av_ref, w1_ref, pos_ref, b1_ref, w2_ref, b2_ref, kn_ref, bd_ref,
                          ck_o, cv_o):
    outs = []
    for kv, a_ref in enumerate((ak_ref, av_ref)):
        parts = _compress_partial(a_ref[...], w1_ref.at[kv])
        outs.append(_compress_finish(parts, w1_ref.at[kv], pos_ref.at[kv], b1_ref.at[kv],
                                     w2_ref.at[kv], b2_ref.at[kv]))
    ck_o[...] = _group_rms(outs[0], bd_ref[...], kn_ref[...])
    cv_o[...] = outs[1]


def _cmp_weight_specs(l):
    def lay(*shape):
        return pl.BlockSpec((None,) + shape, lambda *a: (l,) + (0,) * len(shape))
    flat = CMP_BLOCK * HD_B
    return [lay(2, flat, 256), lay(2, 1, flat), lay(2, 1, 256), lay(2, 512, LANE), lay(2, 1, LANE),
            lay(1, LANE), pl.BlockSpec((LANE, LANE), lambda *a: (0, 0))]


def _cmp_weights(prep):
    return (prep["w1"], prep["pos"], prep["b1"], prep["w2bd"], prep["b2"], prep["cmp_kn"], prep["bd128"])


def _compress_prompt(ck, cv, l, prep, bsz, t):
    nch = t // CMP_STRIDE
    wdt = CMP_STRIDE * LANE
    a_spec = pl.BlockSpec((nch, wdt), lambda b: (b, 0))
    o_spec = pl.BlockSpec((nch, LANE), lambda b: (b, 0))
    return pl.pallas_call(
        _compress_prompt_body,
        grid=(bsz,),
        in_specs=[a_spec, a_spec] + _cmp_weight_specs(l),
        out_specs=[o_spec, o_spec],
        out_shape=[jax.ShapeDtypeStruct((bsz * nch, LANE), F32)] * 2,
        compiler_params=_params("arbitrary"),
        name="compress_prompt",
    )(ck.reshape(bsz * nch, wdt), cv.reshape(bsz * nch, wdt), *_cmp_weights(prep))


def _nsa_q_rows(qf, g):
    m = qf.shape[0]
    zero = jnp.zeros((m, HD_B), F32)
    rows = []
    for r in range(R_B):
        c0 = (g * R_B + r) * HD_B
        piece = qf[:, c0:c0 + HD_B]
        rows.append(jnp.concatenate([piece, zero] if g == 0 else [zero, piece], axis=1))
    return jnp.concatenate(rows, axis=0).astype(BF16)


def _cmp_attention(qrows, ckc, cvc, valid):
    hi, lo = _split(ckc)
    s = _mm_nt(qrows, hi) + _mm_nt(qrows, lo)
    s = jnp.where(valid, s, NEG)
    m = jnp.max(s, axis=-1, keepdims=True)
    e = jnp.exp(s - m)
    anyv = jnp.max(jnp.where(valid, 1.0, 0.0), axis=-1, keepdims=True)
    p = e / jnp.sum(e, axis=-1, keepdims=True) * anyv
    return p, _mm(p.astype(BF16), cvc.astype(BF16))


def _select_blocks(imp_sum, ov, own, n_sel, top):
    hi, lo = _split(imp_sum)
    imp = _mm(hi, ov) + _mm(lo, ov)
    j = lax.broadcasted_iota(jnp.int32, imp.shape, 1)
    imp = jnp.where(j > own, -1.0, jnp.where((j == own) | (j == 0), FORCE, imp))
    imp = jnp.where(j >= n_sel, -2.0, imp)
    jf = j.astype(F32)
    rank = jnp.zeros(imp.shape, F32)
    for k in range(n_sel):
        col = imp[:, k:k + 1]
        rank = rank + jnp.where(col > imp, 1.0, jnp.where(col == imp, jnp.where(jf > k, 1.0, 0.0), 0.0))
    return jnp.where(rank < top, 1.0, 0.0)


def _select_blocks_t(imp_sum, ov_t, own_row, n_sel, top):
    m = imp_sum.shape[0]
    nsp = ov_t.shape[0]
    nsr = -(-n_sel // 8) * 8
    hi, lo = _split(imp_sum)
    imp = (_mm_nt(ov_t, hi) + _mm_nt(ov_t, lo))[0:nsr]
    j = lax.broadcasted_iota(jnp.int32, imp.shape, 0)
    imp = jnp.where(j > own_row, -1.0, jnp.where((j == own_row) | (j == 0), FORCE, imp))
    imp = jnp.where(j >= n_sel, -2.0, imp)
    jf = j.astype(F32)
    rank = jnp.zeros(imp.shape, F32)
    for k in range(n_sel):
        row = imp[k:k + 1, :]
        rank = rank + jnp.where(row > imp, 1.0, jnp.where(row == imp, jnp.where(jf > k, 1.0, 0.0), 0.0))
    sel_t = jnp.where(rank < top, 1.0, 0.0)
    if nsp > nsr:
        sel_t = jnp.concatenate([sel_t, jnp.zeros((nsp - nsr, m), F32)], axis=0)
    r = lax.broadcasted_iota(jnp.int32, (m, m), 0)
    c = lax.broadcasted_iota(jnp.int32, (m, m), 1)
    eye = jnp.where(r == c, 1.0, 0.0).astype(BF16)
    return _mm_nt(eye, sel_t.astype(BF16)).astype(BF16)


def _block_mask(sel_bf16, first_block, n_blocks):
    nb = sel_bf16.shape[1]
    width = n_blocks * SEL_BLOCK
    j = lax.broadcasted_iota(jnp.int32, (nb, width), 0)
    c = lax.broadcasted_iota(jnp.int32, (nb, width), 1)
    e = jnp.where(j == first_block + (c >> SEL_SHIFT), 1.0, 0.0).astype(BF16)
    return jnp.where(_mm(sel_bf16, e) > 0.5, 0.0, NEG)


def _nsa_prompt_body(q_ref, ckc_ref, cvc_ref, kv_ref, gn_ref, gs_ref, pb_ref, ov_ref, o_ref,
                     s_sc, mp_sc, lp_sc, acc_sc, *, n_cmp, n_sel, top):
    i = pl.program_id(1)
    rows = R_B * TQ
    qf = q_ref[...].astype(F32)
    gn = gn_ref[...]
    ckc = ckc_ref[...]
    cvc = cvc_ref[...]
    ncp = ckc.shape[0]
    qp = i * TQ + lax.broadcasted_iota(jnp.int32, (TQ, 1), 0)
    qp4 = i * TQ + (lax.broadcasted_iota(jnp.int32, (rows, 1), 0) & (TQ - 1))
    nidx = lax.broadcasted_iota(jnp.int32, (rows, ncp), 1)
    valid = (nidx * CMP_STRIDE + (CMP_BLOCK - 1) <= qp4) & (nidx < n_cmp)
    own_row = (i * TQ + lax.broadcasted_iota(jnp.int32, (1, TQ), 1)) >> SEL_SHIFT
    rr = lax.broadcasted_iota(jnp.int32, (rows, TQ), 0) & (TQ - 1)
    cc = lax.broadcasted_iota(jnp.int32, (rows, TQ), 1)
    anti = jnp.where(cc > rr, 0.0, NEG)
    wt = WINDOW // TQ
    bpt = TQ // SEL_BLOCK
    pieces = []
    for g in range(N_KV):
        qrows = _nsa_q_rows(qf, g)

        def bias(d, g=g):
            return jnp.concatenate([pb_ref[g * R_B + r, d] for r in range(R_B)], axis=0)

        p_c, o_cmp = _cmp_attention(qrows, ckc, cvc, valid)
        imp_sum = p_c[0:TQ]
        for r in range(1, R_B):
            imp_sum = imp_sum + p_c[r * TQ:(r + 1) * TQ]
        sel = _select_blocks_t(imp_sum, ov_ref[...], own_row, n_sel, top)

        def sel_scores(j, extra, qrows=qrows, sel=sel):
            mask = _block_mask(sel, j * bpt, bpt)
            s = _mm(qrows, kv_ref[j, 0:LANE, :]) + jnp.concatenate([mask] * R_B, axis=0)
            if extra is not None:
                s = s + extra
            _score_tile(s, j, s_sc, mp_sc)

        _softmax_begin(mp_sc)

        def sel_far(j, carry, sel_scores=sel_scores):
            sel_scores(j, None)
            return carry

        lax.fori_loop(0, jnp.maximum(i - 1, 0), sel_far, 0)

        @pl.when(i >= 1)
        def _(sel_scores=sel_scores, bias=bias):
            sel_scores(i - 1, bias(1))

        sel_scores(i, bias(0))
        m = _softmax_mid(mp_sc, lp_sc, acc_sc)

        def sel_probs(j, carry, m=m):
            _prob_tile(j, m, lambda p: _mm_nt(p, kv_ref[j, LANE:2 * LANE, :]), s_sc, lp_sc, acc_sc)
            return carry

        lax.fori_loop(0, i + 1, sel_probs, 0)
        o_sel = _softmax_end(lp_sc, acc_sc)

        _softmax_begin(mp_sc)
        for dt in range(wt + 1):
            def win_scores(dt=dt, qrows=qrows, bias=bias):
                j = i - wt + dt
                s = _mm(qrows, kv_ref[j, 2 * LANE:3 * LANE, :])
                if dt == 0:
                    s = s + anti
                if dt == wt - 1:
                    s = s + bias(1)
                if dt == wt:
                    s = s + bias(0)
                _score_tile(s, dt, s_sc, mp_sc)
            if dt == wt:
                win_scores()
            else:
                pl.when(i - wt + dt >= 0)(win_scores)
        m = _softmax_mid(mp_sc, lp_sc, acc_sc)
        for dt in range(wt + 1):
            def win_probs(dt=dt, m=m):
                j = i - wt + dt
                _prob_tile(dt, m, lambda p: _mm_nt(p, kv_ref[j, 3 * LANE:4 * LANE, :]), s_sc, lp_sc, acc_sc)
            if dt == wt:
                win_probs()
            else:
                pl.when(i - wt + dt >= 0)(win_probs)
        o_win = _softmax_end(lp_sc, acc_sc)

        for r in range(R_B):
            hh = g * R_B + r
            rs = slice(r * TQ, (r + 1) * TQ)
            o = (gn[:, 3 * hh:3 * hh + 1] * o_cmp[rs] + gn[:, 3 * hh + 1:3 * hh + 2] * o_sel[rs]
                 + gn[:, 3 * hh + 2:3 * hh + 3] * o_win[rs])
            pieces.append(o[:, g * HD_B:(g + 1) * HD_B])
    o_ref[...] = (jnp.concatenate(pieces, axis=1) * gs_ref[...]).astype(BF16)


def _nsa_prompt(nq_b, ckc, cvc, kvtb, gn, gs, bias_p, l, bsz, t):
    nt = t // TQ
    nch = t // CMP_STRIDE
    n_cmp = nch - 1
    n_sel = -(-t // SEL_BLOCK)
    top = min(SEL_TOPN, n_sel)
    nsp = -(-n_sel // LANE) * LANE
    ov = jnp.asarray(_overlap_np(nch, nsp, n_cmp, n_sel).T, BF16)
    rows = R_B * TQ
    return pl.pallas_call(
        functools.partial(_nsa_prompt_body, n_cmp=n_cmp, n_sel=n_sel, top=top),
        grid=(bsz, nt),
        in_specs=[pl.BlockSpec((TQ, 512), lambda b, i: (b * nt + i, 0)),
                  pl.BlockSpec((nch, LANE), lambda b, i: (b, 0)),
                  pl.BlockSpec((nch, LANE), lambda b, i: (b, 0)),
                  pl.BlockSpec((None, nt, 512, TQ), lambda b, i: (b, 0, 0, 0)),
                  pl.BlockSpec((TQ, LANE), lambda b, i: (b * nt + i, 0)),
                  pl.BlockSpec((TQ, 512), lambda b, i: (b * nt + i, 1)),
                  pl.BlockSpec((N_HB, 2, TQ, TQ), lambda b, i: (0, 0, 0, 0)),
                  pl.BlockSpec((nsp, nch), lambda b, i: (0, 0))],
        out_specs=pl.BlockSpec((TQ, 512), lambda b, i: (b * nt + i, 0)),
        out_shape=jax.ShapeDtypeStruct((bsz * t, N_HB * HD_B), BF16),
        scratch_shapes=[pltpu.VMEM((nt, rows, TQ), F32), pltpu.VMEM((rows, LANE), F32),
                        pltpu.VMEM((rows, LANE), F32), pltpu.VMEM((rows, LANE), F32)],
        compiler_params=_params("arbitrary", "arbitrary"),
        name="nsa_prompt",
    )(nq_b, ckc, cvc, kvtb, gn, gs, bias_p[N_HA:], ov)


def _diff_decode_body(pt_ref, *refs, n_pg, dec_seq, lam_init):
    kt_pages = refs[0:n_pg]
    v_pages = refs[n_pg:2 * n_pg]
    ktt_ref, vt_ref, q_ref, gs_ref, bias_ref, lam_ref, sub_ref, o_ref, m_sc, l_sc, acc_sc = refs[2 * n_pg:]
    c = pl.program_id(1)
    last = pl.num_programs(1) - 1
    page = kt_pages[0].shape[1]
    hrows = 2 * dec_seq
    q = q_ref[...]
    blk = lax.broadcasted_iota(jnp.int32, q.shape, 1) >> HD_SHIFT
    qrows = jnp.concatenate([jnp.where(blk == hm, q, 0.0) for hm in range(2 * N_HA)], axis=0).astype(BF16)

    def values(refs_):
        def pv(p):
            outs = []
            for h in range(N_HA):
                vh = jnp.concatenate([r[pl.ds(h, page, stride=N_HA), :].astype(BF16) for r in refs_], axis=0)
                outs.append(_mm(p[h * hrows:(h + 1) * hrows], vh))
            return jnp.concatenate(outs, axis=0)
        return pv

    @pl.when(c == 0)
    def _():
        _flash_init(m_sc, l_sc, acc_sc)

    parts = [_mm(qrows, r[...].astype(BF16)) for r in kt_pages]
    parts[-1] = parts[-1] + jnp.where(c == last, bias_ref[:, 0:page], 0.0)
    _flash_update(jnp.concatenate(parts, axis=1), values(v_pages), m_sc, l_sc, acc_sc)

    @pl.when(c == last)
    def _():
        s = _mm(qrows, ktt_ref[...].astype(BF16)) + bias_ref[:, page:2 * page]
        _flash_update(s, values([vt_ref]), m_sc, l_sc, acc_sc)
        o = acc_sc[...] / l_sc[...]
        lam = _lambda(lam_ref, lam_init)
        outs = []
        for h in range(N_HA):
            r0 = h * hrows
            outs.append(_diff_finish(o[r0:r0 + dec_seq], o[r0 + dec_seq:r0 + hrows], lam, sub_ref[...],
                                     lam_init))
        o_ref[...] = jnp.concatenate(outs, axis=1) * gs_ref[...]


def _page_specs(n_pg, l, shape):
    return [pl.BlockSpec((None, None) + shape, (lambda b, c, pt, k=k: (l, pt[b, c * n_pg + k], 0, 0)))
            for k in range(n_pg)]


def _diff_decode(pt, pool_kt, pool_v, kt_tail, v_tail, dq, gs, bias_d, l, prep, lam_init):
    db, n_pages = pt.shape
    page = pool_kt.shape[3]
    dec_seq = dq.shape[0] // db
    n_pg = min(DIFF_PAGES_PER_STEP, n_pages)
    rows = 2 * N_HA * dec_seq
    wd = bias_d.shape[1]
    grid_spec = pltpu.PrefetchScalarGridSpec(
        num_scalar_prefetch=1,
        grid=(db, n_pages // n_pg),
        in_specs=_page_specs(n_pg, l, (512, page)) + _page_specs(n_pg, l, (page * N_HA, VD_A)) + [
            pl.BlockSpec((None, 512, page), lambda b, c, pt: (b, 0, 0)),
            pl.BlockSpec((None, page * N_HA, VD_A), lambda b, c, pt: (b, 0, 0)),
            pl.BlockSpec((dec_seq, 512), lambda b, c, pt: (b, 0)),
            pl.BlockSpec((dec_seq, 512), lambda b, c, pt: (b, 0)),
            pl.BlockSpec((rows, wd), lambda b, c, pt: (0, 0)),
            pl.BlockSpec((None, 4, HD_A), lambda b, c, pt: (l, 0, 0)),
            pl.BlockSpec((None, 1, VD_A), lambda b, c, pt: (l, 0, 0))],
        out_specs=pl.BlockSpec((dec_seq, 512), lambda b, c, pt: (b, 0)),
        scratch_shapes=[pltpu.VMEM((rows, 1), F32), pltpu.VMEM((rows, 1), F32),
                        pltpu.VMEM((rows, VD_A), F32)])
    return pl.pallas_call(
        functools.partial(_diff_decode_body, n_pg=n_pg, dec_seq=dec_seq, lam_init=lam_init),
        grid_spec=grid_spec,
        out_shape=jax.ShapeDtypeStruct((db * dec_seq, N_HA * VD_A), F32),
        compiler_params=_params("arbitrary", "arbitrary"),
        name="diff_decode",
    )(pt, *([pool_kt] * n_pg), *([pool_v] * n_pg), kt_tail, v_tail, dq, gs, bias_d,
      prep["lam"], prep["subln"])


def _cmp_partial_decode_body(pt_ref, *refs, n_pg):
    pages = refs[0:n_pg]
    perm_ref, w1_ref, u_ref = refs[n_pg:]
    perm = perm_ref[...]
    cpp = perm.shape[0] // CMP_STRIDE
    rows = []
    for p in pages:
        t = _mm_nt(perm, p[...].astype(BF16))
        rows.append(jnp.concatenate([t[cpp * r:cpp * (r + 1)] for r in range(CMP_STRIDE)], axis=1))
    parts = _compress_partial(jnp.concatenate(rows, axis=0), w1_ref)
    for k, part in enumerate(parts):
        u_ref[:, 256 * k:256 * (k + 1)] = part


def _cmp_partial_decode(pt, pool_t, kv, l, prep):
    db, n_pages = pt.shape
    page = pool_t.shape[3]
    cpp = page // CMP_STRIDE
    n_pg = min(CMP_PAGES_PER_STEP, n_pages)
    steps = n_pages // n_pg
    flat = CMP_BLOCK * HD_B
    grid_spec = pltpu.PrefetchScalarGridSpec(
        num_scalar_prefetch=1,
        grid=(db, steps),
        in_specs=_page_specs(n_pg, l, (LANE, page)) + [
            pl.BlockSpec((page, page), lambda b, c, pt: (0, 0)),
            pl.BlockSpec((None, None, flat, 256), lambda b, c, pt: (l, kv, 0, 0))],
        out_specs=pl.BlockSpec((n_pg * cpp, 1024), lambda b, c, pt: (b * steps + c, 0)))
    return pl.pallas_call(
        functools.partial(_cmp_partial_decode_body, n_pg=n_pg),
        grid_spec=grid_spec,
        out_shape=jax.ShapeDtypeStruct((db * n_pages * cpp, 1024), F32),
        compiler_params=_params("arbitrary", "arbitrary"),
        name="cmp_partial_decode",
    )(pt, *([pool_t] * n_pg), prep["perm"], prep["w1"])


def _nsa_select_decode_body(uk_ref, uv_ref, w1_ref, pos_ref, b1_ref, w2_ref, b2_ref, kn_ref, bd_ref,
                            q_ref, ov_ref, ocmp_o, sel_o, *, n_cmp, n_sel, top, pos0, dec_seq):
    outs = []
    for kv, u_ref in enumerate((uk_ref, uv_ref)):
        parts = [u_ref[:, 256 * k:256 * (k + 1)] for k in range(4)]
        outs.append(_compress_finish(parts, w1_ref.at[kv], pos_ref.at[kv], b1_ref.at[kv],
                                     w2_ref.at[kv], b2_ref.at[kv]))
    ckc = _group_rms(outs[0], bd_ref[...], kn_ref[...])
    cvc = outs[1]
    ncp = ckc.shape[0]
    qf = q_ref[...]
    rows = R_B * dec_seq
    qp = pos0 + lax.broadcasted_iota(jnp.int32, (dec_seq, 1), 0)
    qp4 = pos0 + (lax.broadcasted_iota(jnp.int32, (rows, 1), 0) & (dec_seq - 1))
    nidx = lax.broadcasted_iota(jnp.int32, (rows, ncp), 1)
    valid = (nidx * CMP_STRIDE + (CMP_BLOCK - 1) <= qp4) & (nidx < n_cmp)
    own = qp >> SEL_SHIFT
    for g in range(N_KV):
        qrows = _nsa_q_rows(qf, g)
        p_c, o_cmp = _cmp_attention(qrows, ckc, cvc, valid)
        imp_sum = p_c[0:dec_seq]
        for r in range(1, R_B):
            imp_sum = imp_sum + p_c[r * dec_seq:(r + 1) * dec_seq]
        sel = _select_blocks(imp_sum, ov_ref[...], own, n_sel, top)
        ocmp_o[g * rows:(g + 1) * rows, :] = o_cmp
        sel_o[g * rows:(g + 1) * rows, :] = jnp.concatenate([sel] * R_B, axis=0)


def _nsa_select_decode(uk, uv, nq, l, prep, db, pos0, dec_seq):
    nch = uk.shape[0] // db
    n_cmp = nch - 1
    n_sel = -(-(pos0 + dec_seq) // SEL_BLOCK)
    top = min(SEL_TOPN, n_sel)
    nsp = -(-n_sel // LANE) * LANE
    ov = jnp.asarray(_overlap_np(nch, nsp, n_cmp, n_sel), BF16)
    rows = N_HB * dec_seq
    u_spec = pl.BlockSpec((nch, 1024), lambda b: (b, 0))
    return pl.pallas_call(
        functools.partial(_nsa_select_decode_body, n_cmp=n_cmp, n_sel=n_sel, top=top, pos0=pos0,
                          dec_seq=dec_seq),
        grid=(db,),
        in_specs=[u_spec, u_spec] + _cmp_weight_specs(l) + [
            pl.BlockSpec((dec_seq, 512), lambda b: (b, 0)),
            pl.BlockSpec((nch, nsp), lambda b: (0, 0))],
        out_specs=[pl.BlockSpec((rows, LANE), lambda b: (b, 0)),
                   pl.BlockSpec((rows, nsp), lambda b: (b, 0))],
        out_shape=[jax.ShapeDtypeStruct((db * rows, LANE), F32),
                   jax.ShapeDtypeStruct((db * rows, nsp), F32)],
        compiler_params=_params("arbitrary"),
        name="nsa_select_decode",
    )(uk, uv, *_cmp_weights(prep), nq, ov)


def _nsa_decode_body(pt_ref, *refs, n_pg, n_pages, dec_seq):
    kt_pages = refs[0:n_pg]
    vt_pages = refs[n_pg:2 * n_pg]
    (ktt_ref, vtt_ref, wks_ref, wvs_ref, wkt_ref, wvt_ref, q_ref, sel_ref, ocmp_ref, gn_ref, gs_ref,
     bias_ref, o_ref, m_sc, l_sc, acc_sc) = refs[2 * n_pg:]
    c = pl.program_id(1)
    last = pl.num_programs(1) - 1
    page = kt_pages[0].shape[1]
    bpp = page // SEL_BLOCK
    qf = q_ref[...]
    qrows = jnp.concatenate([_nsa_q_rows(qf, g) for g in range(N_KV)], axis=0)
    sel = sel_ref[...].astype(BF16)

    def values(refs_):
        def pv(p):
            out = _mm_nt(p[:, 0:page], refs_[0][...].astype(BF16))
            for k in range(1, len(refs_)):
                out = out + _mm_nt(p[:, k * page:(k + 1) * page], refs_[k][...].astype(BF16))
            return out
        return pv

    @pl.when(c == 0)
    def _():
        _flash_init(m_sc, l_sc, acc_sc)

    parts = [_mm(qrows, r[...].astype(BF16)) for r in kt_pages]
    parts[-1] = parts[-1] + jnp.where(c == last, bias_ref[:, 0:page], 0.0)
    s = jnp.concatenate(parts, axis=1) + _block_mask(sel, c * (n_pg * bpp), n_pg * bpp)
    _flash_update(s, values(vt_pages), m_sc, l_sc, acc_sc)

    @pl.when(c == last)
    def _():
        tail_bias = bias_ref[:, page:2 * page]
        s = _mm(qrows, ktt_ref[...].astype(BF16)) + _block_mask(sel, n_pages * bpp, bpp) + tail_bias
        _flash_update(s, values([vtt_ref]), m_sc, l_sc, acc_sc)
        o_sel = acc_sc[...] / l_sc[...]
        s_w = _mm(qrows, wks_ref[...].astype(BF16)) + bias_ref[:, 2 * page:]
        s_t = _mm(qrows, wkt_ref[...].astype(BF16)) + tail_bias
        m = jnp.maximum(jnp.max(s_w, axis=-1, keepdims=True), jnp.max(s_t, axis=-1, keepdims=True))
        p_w = jnp.exp(s_w - m)
        p_t = jnp.exp(s_t - m)
        den = jnp.sum(p_w, axis=-1, keepdims=True) + jnp.sum(p_t, axis=-1, keepdims=True)
        o_win = (_mm_nt(p_w.astype(BF16), wvs_ref[...].astype(BF16))
                 + _mm_nt(p_t.astype(BF16), wvt_ref[...].astype(BF16))) / den
        o_cmp = ocmp_ref[...]
        gn = gn_ref[...]
        pieces = []
        for hh in range(N_HB):
            g = hh // R_B
            rs = slice(hh * dec_seq, (hh + 1) * dec_seq)
            o = (gn[:, 3 * hh:3 * hh + 1] * o_cmp[rs] + gn[:, 3 * hh + 1:3 * hh + 2] * o_sel[rs]
                 + gn[:, 3 * hh + 2:3 * hh + 3] * o_win[rs])
            pieces.append(o[:, g * HD_B:(g + 1) * HD_B])
        o_ref[...] = jnp.concatenate(pieces, axis=1) * gs_ref[...]


def _nsa_decode(pt, pool_kt, pool_vt, kt_tail, vt_tail, win_kt, win_vt, wkt_tail, wvt_tail, nq, sel, ocmp,
                gn, gs, bias_d, l):
    db, n_pages = pt.shape
    page = pool_kt.shape[3]
    dec_seq = nq.shape[0] // db
    n_pg = min(SEL_PAGES_PER_STEP, n_pages)
    rows = N_HB * dec_seq
    wd = bias_d.shape[1]
    wb = win_kt.shape[3]
    nsp = sel.shape[1]
    per_b = pl.BlockSpec((None, LANE, page), lambda b, c, pt: (b, 0, 0))
    win = pl.BlockSpec((None, None, LANE, wb), lambda b, c, pt: (l, b, 0, 0))
    grid_spec = pltpu.PrefetchScalarGridSpec(
        num_scalar_prefetch=1,
        grid=(db, n_pages // n_pg),
        in_specs=_page_specs(n_pg, l, (LANE, page)) + _page_specs(n_pg, l, (LANE, page)) + [
            per_b, per_b, win, win, per_b, per_b,
            pl.BlockSpec((dec_seq, 512), lambda b, c, pt: (b, 0)),
            pl.BlockSpec((rows, nsp), lambda b, c, pt: (b, 0)),
            pl.BlockSpec((rows, LANE), lambda b, c, pt: (b, 0)),
            pl.BlockSpec((dec_seq, LANE), lambda b, c, pt: (b, 0)),
            pl.BlockSpec((dec_seq, 512), lambda b, c, pt: (b, 1)),
            pl.BlockSpec((rows, wd), lambda b, c, pt: (0, 0))],
        out_specs=pl.BlockSpec((dec_seq, 512), lambda b, c, pt: (b, 0)),
        scratch_shapes=[pltpu.VMEM((rows, 1), F32), pltpu.VMEM((rows, 1), F32),
                        pltpu.VMEM((rows, LANE), F32)])
    return pl.pallas_call(
        functools.partial(_nsa_decode_body, n_pg=n_pg, n_pages=n_pages, dec_seq=dec_seq),
        grid_spec=grid_spec,
        out_shape=jax.ShapeDtypeStruct((db * dec_seq, N_HB * HD_B), F32),
        compiler_params=_params("arbitrary", "arbitrary"),
        name="nsa_decode",
    )(pt, *([pool_kt] * n_pg), *([pool_vt] * n_pg), kt_tail, vt_tail, win_kt, win_vt, wkt_tail, wvt_tail,
      nq, sel, ocmp, gn, gs, bias_d)


def _prepare(norm_g, w_in, b_in, diff_q_norm, diff_k_norm, diff_lambda, diff_subln, nsa_q_norm,
             nsa_k_norm, cmp_pos, cmp_w1, cmp_b1, cmp_w2, cmp_b2, w_branch_a, w_branch_b, w_out, page):
    depth, d, _ = w_in.shape
    c_gm = C_GN + N_GN
    pad = LANE - N_GN

    def reorder(a):
        z = jnp.zeros(a.shape[:-1] + (pad,), a.dtype)
        return jnp.concatenate([a[..., :C_GN], a[..., c_gm:c_gm + 2 * d], a[..., C_GN:c_gm], z], axis=-1)

    def t_rows(a):
        return jnp.concatenate([a[..., C_DK:C_DV], a[..., C_KV:C_GA]], axis=-1)

    scale_a = HD_A ** -0.5
    scale_b = HD_B ** -0.5
    gains = jnp.zeros((depth, 8, 512), F32)
    gains = gains.at[:, 0].set(jnp.tile(diff_q_norm, (1, 8)) * scale_a)
    gains = gains.at[:, 1].set(jnp.tile(diff_k_norm, (1, 8)))
    gains = gains.at[:, 2].set(jnp.tile(nsa_q_norm, (1, 8)) * scale_b)
    gains = gains.at[:, 3, 0:128].set(jnp.tile(nsa_k_norm[:, 1], (1, 2)))
    gains = gains.at[:, 3, 128:256].set(jnp.tile(nsa_k_norm[:, 2], (1, 2)))
    gains_t = jnp.concatenate([jnp.tile(diff_k_norm, (1, 8)), jnp.tile(nsa_k_norm[:, 1], (1, 2)),
                               jnp.tile(nsa_k_norm[:, 2], (1, 2)), jnp.zeros((depth, N_T - 768), F32)], axis=-1)
    zero = jnp.zeros_like(cmp_w2)
    w2bd = jnp.concatenate([jnp.concatenate([cmp_w2, zero], axis=-1),
                            jnp.concatenate([zero, cmp_w2], axis=-1)], axis=-2)
    return {
        "norm_g": norm_g.reshape(depth, 1, d),
        "w_in": reorder(w_in).astype(BF16),
        "b_in": reorder(b_in).reshape(depth, 1, -1),
        "w_t": jnp.swapaxes(t_rows(w_in), 1, 2).astype(BF16),
        "b_t": t_rows(b_in).reshape(depth, N_T, 1),
        "gains": gains,
        "gains_t": gains_t.reshape(depth, N_T, 1),
        "bd512": jnp.asarray(_blockdiag_np(512, HD_A), BF16),
        "bd128": jnp.asarray(_blockdiag_np(LANE, HD_B), BF16),
        "perm": jnp.asarray(_chunk_perm_np(page), BF16),
        "lam": diff_lambda,
        "subln": diff_subln.reshape(depth, 1, VD_A),
        "w1": cmp_w1.astype(BF16),
        "pos": cmp_pos.reshape(depth, 2, 1, CMP_BLOCK * HD_B),
        "b1": cmp_b1.reshape(depth, 2, 1, -1),
        "w2bd": w2bd.astype(BF16),
        "b2": jnp.tile(cmp_b2, (1, 1, 2)).reshape(depth, 2, 1, LANE),
        "cmp_kn": jnp.tile(nsa_k_norm[:, 0], (1, 2)).reshape(depth, 1, LANE),
        "w_a": w_branch_a.astype(BF16),
        "w_b": w_branch_b.astype(BF16),
        "w_out": w_out.astype(BF16),
    }


def kernel(x_prompt, x_sample, cache_diff_k, cache_diff_v, cache_cmp_k, cache_cmp_v, cache_sel_k, cache_sel_v,
           state_win_k, state_win_v, page_table, rel_bias, norm_g, w_in, b_in, diff_q_norm, diff_k_norm,
           diff_lambda, diff_subln, nsa_q_norm, nsa_k_norm, cmp_pos, cmp_w1, cmp_b1, cmp_w2, cmp_b2,
           w_branch_a, w_branch_b, w_out):
    bsz, t, d = x_prompt.shape
    db, dec_seq, _ = x_sample.shape
    depth, n_pool, page = cache_diff_k.shape[:3]
    n_pages = page_table.shape[1]
    past = n_pages * page
    wb = state_win_k.shape[2]
    assert t % TQ == 0 and WINDOW % TQ == 0 and page == LANE and wb == WINDOW and past % SEL_BLOCK == 0
    assert dec_seq <= CMP_STRIDE and dec_seq % 8 == 0 and min(WINDOW, t) % TQ == 0

    prep = _prepare(norm_g, w_in, b_in, diff_q_norm, diff_k_norm, diff_lambda, diff_subln, nsa_q_norm,
                    nsa_k_norm, cmp_pos, cmp_w1, cmp_b1, cmp_w2, cmp_b2, w_branch_a, w_branch_b, w_out, page)
    bias_p, bias_d = _bias_tiles(rel_bias, jnp.asarray(_prompt_bucket_tiles()),
                                 jnp.asarray(_decode_bucket_tiles(dec_seq, page, wb)))
    wd = bias_d.shape[-1]
    bias_d_diff = jnp.repeat(bias_d[:N_HA], 2, axis=0).reshape(2 * N_HA * dec_seq, wd)
    bias_d_nsa = bias_d[N_HA:].reshape(N_HB * dec_seq, wd)

    pool_dkt = jnp.transpose(cache_diff_k, (0, 1, 3, 4, 5, 2)).reshape(depth, n_pool, 512, page)
    pool_dv = cache_diff_v.reshape(depth, n_pool, page * N_HA, VD_A)
    def pool_t(a):
        return jnp.transpose(a, (0, 1, 3, 4, 2)).reshape(depth, a.shape[1], LANE, a.shape[2])
    pool_ckt, pool_cvt, pool_skt, pool_svt = (pool_t(a) for a in (cache_cmp_k, cache_cmp_v, cache_sel_k,
                                                                 cache_sel_v))
    win_kt, win_vt = pool_t(state_win_k), pool_t(state_win_v)

    def tail_t(a):
        a = jnp.swapaxes(a.reshape(db, dec_seq, a.shape[-1]), 1, 2)
        return jnp.pad(a, ((0, 0), (0, 0), (0, page - dec_seq)))

    xp = x_prompt.reshape(bsz * t, d)
    xs = x_sample.reshape(db * dec_seq, d)
    stacked = None
    rows_s = []
    for l in range(depth):
        lam_init = 0.8 - 0.6 * math.exp(-0.3 * l)

        outs = _in_proj_prompt(xp, l, prep, bsz, t, depth, stacked)
        dq_b, nq_b, dv_b, ck_n, cv_n, gs, gm, gn, ktb, kvtb = outs[:10]
        stacked = tuple(outs[10:])
        oa = _diff_prompt(dq_b, ktb, dv_b, gs, bias_p, l, prep, bsz, t, lam_init)
        ckc, cvc = _compress_prompt(ck_n, cv_n, l, prep, bsz, t)
        ob = _nsa_prompt(nq_b, ckc, cvc, kvtb, gn, gs, bias_p, l, bsz, t)
        xp = _out_proj(xp, oa, ob, gm, l, prep)

        dq, dk, dv, nq, kv6, gs, gm, gn = _in_proj_sample(xs, l, prep)
        ck, cv, sk, sv, wk, wv = (kv6[:, LANE * k:LANE * (k + 1)] for k in range(6))
        v_tail = jnp.pad(dv.reshape(db, dec_seq * N_HA, VD_A), ((0, 0), (0, (page - dec_seq) * N_HA), (0, 0)))
        oa = _diff_decode(page_table, pool_dkt, pool_dv, tail_t(dk), v_tail, dq, gs, bias_d_diff, l, prep,
                          lam_init)
        uk = _cmp_partial_decode(page_table, pool_ckt, 0, l, prep)
        uv = _cmp_partial_decode(page_table, pool_cvt, 1, l, prep)
        ocmp, sel = _nsa_select_decode(uk, uv, nq, l, prep, db, past, dec_seq)
        ob = _nsa_decode(page_table, pool_skt, pool_svt, tail_t(sk), tail_t(sv), win_kt, win_vt,
                         tail_t(wk), tail_t(wv), nq, sel, ocmp, gn, gs, bias_d_nsa, l)
        xs = _out_proj(xs, oa, ob, gm, l, prep)
        wk3 = wk.reshape(db, dec_seq, N_KV, HD_B)
        wv3 = wv.reshape(db, dec_seq, N_KV, HD_B)
        buf_k = jnp.concatenate([state_win_k[l], wk3], axis=1)
        buf_v = jnp.concatenate([state_win_v[l], wv3], axis=1)
        nw = min(WINDOW, buf_k.shape[1])
        rows_s.append((dk.reshape(db, dec_seq, N_HA, 2, HD_A), dv.reshape(db, dec_seq, N_HA, VD_A),
                       ck.reshape(db, dec_seq, N_KV, HD_B), cv.reshape(db, dec_seq, N_KV, HD_B),
                       sk.reshape(db, dec_seq, N_KV, HD_B), sv.reshape(db, dec_seq, N_KV, HD_B),
                       buf_k[:, -nw:], buf_v[:, -nw:]))

    dkt, dvr, ckt, cvt, skt, svt, wkt, wvt = stacked

    def untranspose(a):
        return jnp.transpose(a.reshape(depth, bsz, N_KV, HD_B, a.shape[-1]), (0, 1, 4, 2, 3))

    outs_p = [jnp.transpose(dkt.reshape(depth, bsz, N_HA, 2, HD_A, t), (0, 1, 5, 2, 3, 4)),
              dvr.reshape(depth, bsz, t, N_HA, VD_A),
              untranspose(ckt), untranspose(cvt), untranspose(skt), untranspose(svt),
              untranspose(wkt), untranspose(wvt)]
    outs_s = [jnp.stack(r, axis=0) for r in zip(*rows_s)]
    return (xp.reshape(bsz, t, d), xs.reshape(db, dec_seq, d), *outs_p, *outs_s)
```

```python
import functools
import math

import numpy as np
import jax
import jax.numpy as jnp
from jax import lax
from jax.experimental import pallas as pl
from jax.experimental.pallas import tpu as pltpu

F32 = jnp.float32
BF16 = jnp.bfloat16

N_HA, HD_A, VD_A = 4, 64, 128
N_HB, N_KV, HD_B = 8, 2, 64
R_B = N_HB // N_KV
CMP_STRIDE, CMP_BLOCK = 16, 32
SEL_BLOCK, SEL_TOPN, WINDOW = 64, 16, 512
NUM_BUCKETS, MAX_DISTANCE = 32, 128
EPS, NEG, FORCE = 1e-6, -1e30, 1e4
MASKED_BUCKET = NUM_BUCKETS
SEL_SHIFT = SEL_BLOCK.bit_length() - 1
HD_SHIFT = HD_A.bit_length() - 1

LANE = 128
TQ = 256
FAR_CHUNK = 512
VMEM_LIMIT = 56 * 1024 * 1024
DIFF_PAGES_PER_STEP = 16
SEL_PAGES_PER_STEP = 32
CMP_PAGES_PER_STEP = 64

C_DK, C_DV, C_NQ, C_KV, C_GA, C_GN = 512, 1024, 1536, 2048, 2816, 3840
N_GN = N_HB * 3
N_T = 512 + 768


def _mm(a, b):
    return jnp.dot(a, b, preferred_element_type=F32)


def _mm_nt(a, b):
    return lax.dot_general(a, b, (((1,), (1,)), ((), ())), preferred_element_type=F32)


def _split(x):
    hi = x.astype(BF16)
    lo = (x - hi.astype(F32)).astype(BF16)
    return hi, lo


def _sigmoid(z):
    return 1.0 / (1.0 + jnp.exp(-z))


def _group_rms(z, bd, gain):
    hi, lo = _split(z * z)
    ss = _mm(hi, bd) + _mm(lo, bd)
    return z * lax.rsqrt(ss * (1.0 / HD_A) + EPS) * gain


def _group_rms_t(z, gain):
    r, n = z.shape
    z3 = z.reshape(r // HD_A, HD_A, n)
    ss = jnp.sum(z3 * z3, axis=1, keepdims=True)
    return (z3 * lax.rsqrt(ss * (1.0 / HD_A) + EPS)).reshape(r, n) * gain


def _params(*sem):
    return pltpu.CompilerParams(dimension_semantics=sem, vmem_limit_bytes=VMEM_LIMIT)


def _lane_fold(x, op):
    out = x[:, 0:LANE]
    for k in range(1, x.shape[1] // LANE):
        out = op(out, x[:, k * LANE:(k + 1) * LANE])
    return out


def _key_spans(lo_tile, i):
    spans = []
    far_end = max(i - 1, lo_tile) * TQ
    c0 = lo_tile * TQ
    while c0 < far_end:
        spans.append((c0, min(c0 + FAR_CHUNK, far_end)))
        c0 = spans[-1][1]
    spans += [(j * TQ, (j + 1) * TQ) for j in range(max(i - 1, lo_tile), i + 1)]
    return spans


def _two_pass_softmax(score, value, spans, s_sc):
    mp = None
    for c0, c1 in spans:
        s = score(c0, c1)
        s_sc[:, c0:c1] = s
        f = _lane_fold(s, jnp.maximum)
        mp = f if mp is None else jnp.maximum(mp, f)
    m = jnp.max(mp, axis=-1, keepdims=True)
    lp = None
    acc = None
    for c0, c1 in spans:
        p = jnp.exp(s_sc[:, c0:c1] - m)
        f = _lane_fold(p, jnp.add)
        lp = f if lp is None else lp + f
        pv = value(c0, c1, p.astype(BF16))
        acc = pv if acc is None else acc + pv
    return acc / jnp.sum(lp, axis=-1, keepdims=True)


def _flash_update(s, pv, m_ref, l_ref, acc_ref):
    m_old = m_ref[...]
    m_new = jnp.maximum(m_old, jnp.max(s, axis=-1, keepdims=True))
    alpha = jnp.exp(m_old - m_new)
    p = jnp.exp(s - m_new)
    l_ref[...] = alpha * l_ref[...] + jnp.sum(p, axis=-1, keepdims=True)
    acc_ref[...] = alpha * acc_ref[...] + pv(p.astype(BF16))
    m_ref[...] = m_new


def _flash_init(m_ref, l_ref, acc_ref):
    m_ref[...] = jnp.full(m_ref.shape, -3e38, F32)
    l_ref[...] = jnp.zeros(l_ref.shape, F32)
    acc_ref[...] = jnp.zeros(acc_ref.shape, F32)


def _bucket_np(rel):
    n = np.maximum(rel, 0)
    exact = NUM_BUCKETS // 2
    nf = np.maximum(n, 1).astype(np.float32)
    large = exact + (np.log(nf / np.float32(exact)) / np.float32(math.log(MAX_DISTANCE / exact))
                     * np.float32(NUM_BUCKETS - exact)).astype(np.int32)
    large = np.minimum(large, NUM_BUCKETS - 1)
    return np.where(n < exact, n, large).astype(np.int32)


def _prompt_bucket_tiles():
    r = np.arange(TQ)[:, None]
    c = np.arange(TQ)[None, :]
    d0 = np.where(r - c >= 0, _bucket_np(r - c), MASKED_BUCKET)
    d1 = _bucket_np(TQ + r - c)
    return np.stack([d0, d1]).astype(np.int32)


def _decode_bucket_tiles(dec_seq, page, wb):
    q = np.arange(dec_seq)[:, None]
    c = np.arange(page)[None, :]
    last = _bucket_np(page + q - c)
    tail = np.where((c < dec_seq) & (c <= q), _bucket_np(q - c), MASKED_BUCKET)
    i = np.arange(wb)[None, :]
    relw = wb + q - i
    win = np.where(relw < WINDOW, _bucket_np(relw), MASKED_BUCKET)
    return np.concatenate([last, tail, win], axis=1).astype(np.int32)


def _overlap_np(n_rows, n_cols, n_cmp, n_sel):
    n = np.arange(n_rows)[:, None]
    j = np.arange(n_cols)[None, :]
    ov = (n * CMP_STRIDE < j * SEL_BLOCK + SEL_BLOCK) & (n * CMP_STRIDE + CMP_BLOCK > j * SEL_BLOCK)
    ov &= (n < n_cmp) & (j < n_sel)
    return ov.astype(np.float32)


def _blockdiag_np(n, group):
    i = np.arange(n)
    return (i[:, None] // group == i[None, :] // group).astype(np.float32)


def _chunk_perm_np(page):
    rows = np.arange(page)
    tok = (rows % (page // CMP_STRIDE)) * CMP_STRIDE + rows // (page // CMP_STRIDE)
    return (tok[:, None] == np.arange(page)[None, :]).astype(np.float32)


def _bias_body(tbl_ref, idxp_ref, idxd_ref, p_ref, d_ref):
    h = pl.program_id(0)
    far = tbl_ref[h, NUM_BUCKETS - 1]

    def expand(idx):
        acc = jnp.zeros(idx.shape, F32)
        for b in range(NUM_BUCKETS - 1):
            acc = jnp.where(idx == b, tbl_ref[h, b] - far, acc)
        return jnp.where(idx == MASKED_BUCKET, NEG, acc)

    p_ref[0] = expand(idxp_ref[0])
    p_ref[1] = expand(idxp_ref[1])
    d_ref[...] = expand(idxd_ref[...])


def _bias_tiles(rel_bias, idx_p, idx_d):
    nh = rel_bias.shape[0]
    dq, wd = idx_d.shape
    return pl.pallas_call(
        _bias_body,
        grid=(nh,),
        in_specs=[pl.BlockSpec(memory_space=pltpu.SMEM),
                  pl.BlockSpec((2, TQ, TQ), lambda h: (0, 0, 0)),
                  pl.BlockSpec((dq, wd), lambda h: (0, 0))],
        out_specs=[pl.BlockSpec((None, 2, TQ, TQ), lambda h: (h, 0, 0, 0)),
                   pl.BlockSpec((None, dq, wd), lambda h: (h, 0, 0))],
        out_shape=[jax.ShapeDtypeStruct((nh, 2, TQ, TQ), F32),
                   jax.ShapeDtypeStruct((nh, dq, wd), F32)],
        compiler_params=_params("arbitrary"),
        name="bias_tiles",
    )(rel_bias, idx_p, idx_d)


def _normed_input(x_ref, g_ref):
    x = x_ref[...]
    ms = jnp.mean(x * x, axis=-1, keepdims=True)
    return (x * lax.rsqrt(ms + EPS) * g_ref[...]).astype(BF16)


def _in_proj_sample_body(x_ref, g_ref, w_ref, b_ref, bd_ref, gain_ref,
                         dq_o, dk_o, dv_o, nq_o, kv_o, gs_o, gm_o, gn_o, *, d_model):
    h = _normed_input(x_ref, g_ref)

    def seg(a, b):
        return _mm(h, w_ref[:, a:b]) + b_ref[:, a:b]

    bd = bd_ref[...]
    bd2 = bd[0:LANE, 0:LANE]
    dq_o[...] = _group_rms(seg(0, C_DK), bd, gain_ref[0:1, :])
    dk_o[...] = _group_rms(seg(C_DK, C_DV), bd, gain_ref[1:2, :])
    dv_o[...] = seg(C_DV, C_NQ)
    nq_o[...] = _group_rms(seg(C_NQ, C_KV), bd, gain_ref[2:3, :])
    z = seg(C_KV, C_GA)
    kv_o[:, 0:256] = z[:, 0:256]
    kv_o[:, 256:384] = _group_rms(z[:, 256:384], bd2, gain_ref[3:4, 0:128])
    kv_o[:, 384:512] = z[:, 384:512]
    kv_o[:, 512:640] = _group_rms(z[:, 512:640], bd2, gain_ref[3:4, 128:256])
    kv_o[:, 640:768] = z[:, 640:768]
    z = seg(C_GA, C_GN)
    gs_o[...] = z * _sigmoid(z)
    gm_o[...] = _sigmoid(seg(C_GN, C_GN + 2 * d_model))
    gn_o[...] = _sigmoid(seg(C_GN + 2 * d_model, C_GN + 2 * d_model + LANE))


def _in_proj_sample(x2d, l, prep):
    n, d = x2d.shape
    tm = min(256, n)
    c = prep["w_in"].shape[-1]

    def row(w):
        return pl.BlockSpec((tm, w), lambda i: (i, 0))

    def lay(*shape):
        return pl.BlockSpec((None,) + shape, lambda i: (l,) + (0,) * len(shape))

    widths = [512, 512, 512, 512, 768, 1024, 2 * d, LANE]
    return pl.pallas_call(
        functools.partial(_in_proj_sample_body, d_model=d),
        grid=(n // tm,),
        in_specs=[row(d), lay(1, d), lay(d, c), lay(1, c),
                  pl.BlockSpec((512, 512), lambda i: (0, 0)), lay(8, 512)],
        out_specs=[row(w) for w in widths],
        out_shape=[jax.ShapeDtypeStruct((n, w), F32) for w in widths],
        compiler_params=_params("arbitrary"),
        name="in_proj_sample",
    )(x2d, prep["norm_g"], prep["w_in"], prep["b_in"], prep["bd512"], prep["gains"])


N_STACKED = 8


def _in_proj_prompt_body(*refs, d_model, n_alias):
    x_ref, g_ref, w_ref, b_ref, wt_ref, bt_ref, bd_ref, gain_ref, gaint_ref = refs[0:9]
    (dq_o, nq_o, dvb_o, ckn_o, cvn_o, gs_o, gm_o, gn_o, ktb_o, kvtb_o,
     dkt_o, dv_o, ckt_o, cvt_o, skt_o, svt_o, wkt_o, wvt_o) = refs[9 + n_alias:]
    h = _normed_input(x_ref, g_ref)
    tm = h.shape[0]

    def seg(a, b):
        return _mm(h, w_ref[:, a:b]) + b_ref[:, a:b]

    def seg_t(a, b):
        return _mm_nt(wt_ref[a:b, :], h) + bt_ref[a:b, :]

    bd = bd_ref[...]
    dq_o[...] = _group_rms(seg(0, C_DK), bd, gain_ref[0:1, :]).astype(BF16)
    nq_o[...] = _group_rms(seg(C_NQ, C_KV), bd, gain_ref[2:3, :]).astype(BF16)
    y = seg(C_DV, C_NQ)
    dvb_o[...] = y.astype(BF16)
    for hh in range(N_HA):
        dv_o[pl.ds(hh, tm, stride=N_HA), :] = y[:, hh * VD_A:(hh + 1) * VD_A]
    z = seg(C_KV, C_KV + 256)
    ckn_o[...] = z[:, 0:128]
    cvn_o[...] = z[:, 128:256]
    z = seg(C_GA, C_GN)
    gs_o[...] = z * _sigmoid(z)
    gm_o[...] = _sigmoid(seg(C_GN, C_GN + 2 * d_model))
    gn_o[...] = _sigmoid(seg(C_GN + 2 * d_model, C_GN + 2 * d_model + LANE))

    yt = _group_rms_t(seg_t(0, 512), gaint_ref[0:512, :])
    dkt_o[...] = yt
    ktb_o[...] = yt.astype(BF16)
    zt = seg_t(512, N_T)
    ckt_o[...] = zt[0:128]
    cvt_o[...] = zt[128:256]
    sk = _group_rms_t(zt[256:384], gaint_ref[512:640, :])
    sv = zt[384:512]
    wk = _group_rms_t(zt[512:640], gaint_ref[640:768, :])
    wv = zt[640:768]
    skt_o[...] = sk
    svt_o[...] = sv
    wkt_o[...] = wk
    wvt_o[...] = wv
    kvtb_o[0:128, :] = sk.astype(BF16)
    kvtb_o[128:256, :] = sv.astype(BF16)
    kvtb_o[256:384, :] = wk.astype(BF16)
    kvtb_o[384:512, :] = wv.astype(BF16)


def _in_proj_prompt(x2d, l, prep, bsz, t, depth, stacked):
    n, d = x2d.shape
    tm = TQ
    nt = t // tm
    c = prep["w_in"].shape[-1]
    nw = min(WINDOW, t)
    nwt = nw // tm

    def row(w):
        return pl.BlockSpec((tm, w), lambda i: (i, 0))

    def lay(*shape):
        return pl.BlockSpec((None,) + shape, lambda i: (l,) + (0,) * len(shape))

    def tile_t(rows):
        return pl.BlockSpec((None, rows, tm), lambda i: (i // nt, 0, i % nt))

    def stack_t(rows):
        return pl.BlockSpec((None, None, rows, tm), lambda i: (l, i // nt, 0, i % nt))

    win_t = pl.BlockSpec((None, None, LANE, tm),
                         lambda i: (l, i // nt, 0, jnp.maximum(i % nt - (nt - nwt), 0)))
    out_specs = [row(512), row(512), row(512), row(LANE), row(LANE), row(1024), row(2 * d), row(LANE),
                 tile_t(512), tile_t(512),
                 stack_t(512), pl.BlockSpec((None, tm * N_HA, VD_A), lambda i: (l, i, 0)),
                 stack_t(LANE), stack_t(LANE), stack_t(LANE), stack_t(LANE), win_t, win_t]
    kv_t = jax.ShapeDtypeStruct((depth, bsz, LANE, t), F32)
    win_shape = jax.ShapeDtypeStruct((depth, bsz, LANE, nw), F32)
    out_shape = [jax.ShapeDtypeStruct((n, 512), BF16), jax.ShapeDtypeStruct((n, 512), BF16),
                 jax.ShapeDtypeStruct((n, 512), BF16), jax.ShapeDtypeStruct((n, LANE), F32),
                 jax.ShapeDtypeStruct((n, LANE), F32), jax.ShapeDtypeStruct((n, 1024), F32),
                 jax.ShapeDtypeStruct((n, 2 * d), F32), jax.ShapeDtypeStruct((n, LANE), F32),
                 jax.ShapeDtypeStruct((bsz, 512, t), BF16), jax.ShapeDtypeStruct((bsz, 512, t), BF16),
                 jax.ShapeDtypeStruct((depth, bsz, 512, t), F32),
                 jax.ShapeDtypeStruct((depth, n * N_HA, VD_A), F32),
                 kv_t, kv_t, kv_t, kv_t, win_shape, win_shape]
    n_alias = 0 if stacked is None else N_STACKED
    n_in = 9
    n_plain = len(out_shape) - N_STACKED
    aliases = {n_in + k: n_plain + k for k in range(n_alias)}
    return pl.pallas_call(
        functools.partial(_in_proj_prompt_body, d_model=d, n_alias=n_alias),
        grid=(n // tm,),
        in_specs=[row(d), lay(1, d), lay(d, c), lay(1, c), lay(N_T, d), lay(N_T, 1),
                  pl.BlockSpec((512, 512), lambda i: (0, 0)), lay(8, 512), lay(N_T, 1)]
                 + [pl.BlockSpec(memory_space=pl.ANY)] * n_alias,
        out_specs=out_specs,
        out_shape=out_shape,
        input_output_aliases=aliases,
        compiler_params=_params("arbitrary"),
        name="in_proj_prompt",
    )(x2d, prep["norm_g"], prep["w_in"], prep["b_in"], prep["w_t"], prep["b_t"], prep["bd512"],
      prep["gains"], prep["gains_t"], *(stacked or ()))


def _out_proj_body(x_ref, oa_ref, ob_ref, gm_ref, wa_ref, wb_ref, wo_ref, y_ref, *, d_model):
    ya = _mm(oa_ref[...].astype(BF16), wa_ref[...])
    yb = _mm(ob_ref[...].astype(BF16), wb_ref[...])
    gm = gm_ref[...]
    m = gm[:, 0:d_model] * ya + gm[:, d_model:2 * d_model] * yb
    y_ref[...] = x_ref[...] + _mm(m.astype(BF16), wo_ref[...])


def _out_proj(x2d, oa, ob, gm, l, prep):
    n, d = x2d.shape
    tm = min(512, n)

    def row(w):
        return pl.BlockSpec((tm, w), lambda i: (i, 0))

    def lay(*shape):
        return pl.BlockSpec((None,) + shape, lambda i: (l,) + (0,) * len(shape))

    return pl.pallas_call(
        functools.partial(_out_proj_body, d_model=d),
        grid=(n // tm,),
        in_specs=[row(d), row(512), row(512), row(2 * d), lay(512, d), lay(512, d), lay(d, d)],
        out_specs=row(d),
        out_shape=jax.ShapeDtypeStruct((n, d), F32),
        compiler_params=_params("arbitrary"),
        name="out_proj",
    )(x2d, oa, ob, gm, prep["w_a"], prep["w_b"], prep["w_out"])


def _lambda(lam_ref, lam_init):
    lv = lam_ref[...]
    a = jnp.sum(lv[0:1] * lv[1:2], axis=-1, keepdims=True)
    b = jnp.sum(lv[2:3] * lv[3:4], axis=-1, keepdims=True)
    return jnp.exp(a) - jnp.exp(b) + lam_init


def _diff_finish(o1, o2, lam, sub, lam_init):
    o = o1 - lam * o2
    ms = jnp.mean(o * o, axis=-1, keepdims=True)
    return o * lax.rsqrt(ms + EPS) * sub * (1.0 - lam_init)


def _diff_prompt_body(*refs, i, lam_init):
    q_ref, kt_ref, v_ref, gs_ref, bias_ref, lam_ref, sub_ref = refs[0:7]
    o_ref, s_sc = refs[-2:]
    q = q_ref[...]
    lane = lax.broadcasted_iota(jnp.int32, q.shape, 1)
    zero = jnp.zeros_like(q)
    qq = jnp.concatenate([jnp.where(lane < HD_A, q, zero), jnp.where(lane >= HD_A, q, zero)], axis=0)

    def score(c0, c1):
        s = _mm(qq, kt_ref[:, c0:c1])
        if c0 >= (i - 1) * TQ:
            b = bias_ref[0 if c0 == i * TQ else 1]
            s = s + jnp.concatenate([b, b], axis=0)
        return s

    def value(c0, c1, p):
        return _mm(p, v_ref[c0:c1, :])

    o = _two_pass_softmax(score, value, _key_spans(0, i), s_sc)
    y = _diff_finish(o[0:TQ], o[TQ:2 * TQ], _lambda(lam_ref, lam_init), sub_ref[...], lam_init)
    o_ref[...] = (y * gs_ref[...]).astype(BF16)


def _diff_prompt(dq_b, ktb, dv_b, gs, bias_p, l, prep, bsz, t, lam_init):
    nt = t // TQ
    out = None
    for i in range(nt):
        w = (i + 1) * TQ
        alias = [] if out is None else [out]
        out = pl.pallas_call(
            functools.partial(_diff_prompt_body, i=i, lam_init=lam_init),
            grid=(bsz, N_HA),
            in_specs=[pl.BlockSpec((TQ, LANE), lambda b, h, i=i: (b * nt + i, h)),
                      pl.BlockSpec((None, 2 * HD_A, w), lambda b, h: (b, h, 0)),
                      pl.BlockSpec((t, LANE), lambda b, h: (b, h)),
                      pl.BlockSpec((TQ, LANE), lambda b, h, i=i: (b * nt + i, h)),
                      pl.BlockSpec((None, 2, TQ, TQ), lambda b, h: (h, 0, 0, 0)),
                      pl.BlockSpec((None, 4, HD_A), lambda b, h: (l, 0, 0)),
                      pl.BlockSpec((None, 1, VD_A), lambda b, h: (l, 0, 0))]
                     + [pl.BlockSpec(memory_space=pl.ANY)] * len(alias),
            out_specs=pl.BlockSpec((TQ, LANE), lambda b, h, i=i: (b * nt + i, h)),
            out_shape=jax.ShapeDtypeStruct((bsz * t, N_HA * VD_A), BF16),
            input_output_aliases={7: 0} if alias else {},
            scratch_shapes=[pltpu.VMEM((2 * TQ, w), F32)],
            compiler_params=_params("arbitrary", "arbitrary"),
            name=f"diff_prompt_q{i}",
        )(dq_b, ktb, dv_b, gs, bias_p, prep["lam"], prep["subln"], *alias)
    return out


def _deinterleave(a):
    lane = lax.broadcasted_iota(jnp.int32, (a.shape[0], LANE), 1)
    low = lane < HD_B
    g0, g1 = [], []
    for k in range(CMP_STRIDE // 2):
        xe = a[:, 2 * LANE * k:2 * LANE * k + LANE]
        xo = a[:, 2 * LANE * k + LANE:2 * LANE * (k + 1)]
        g0.append(jnp.where(low, xe, pltpu.roll(xo, HD_B, 1)))
        g1.append(jnp.where(low, pltpu.roll(xe, HD_B, 1), xo))
    return jnp.concatenate(g0, axis=1).astype(BF16), jnp.concatenate(g1, axis=1).astype(BF16)


def _compress_partial(a, w1_ref):
    a0, a1 = _deinterleave(a)
    half = CMP_STRIDE * HD_B
    w1a = w1_ref[0:half, :]
    w1b = w1_ref[half:2 * half, :]
    return _mm(a0, w1a), _mm(a1, w1a), _mm(a0, w1b), _mm(a1, w1b)


def _compress_finish(parts, w1_ref, pos_ref, b1_ref, w2_ref, b2_ref):
    ha0, ha1, hb0, hb1 = parts
    m = ha0.shape[0]
    pos = jnp.broadcast_to(pos_ref[...], (8, pos_ref.shape[-1])).astype(BF16)
    c = _mm(pos, w1_ref[...])[0:1] + b1_ref[...]
    h0 = ha0 + pltpu.roll(hb0, m - 1, 0) + c
    h1 = ha1 + pltpu.roll(hb1, m - 1, 0) + c
    hid = jnp.concatenate([h0 * _sigmoid(h0), h1 * _sigmoid(h1)], axis=1).astype(BF16)
    return _mm(hid, w2_ref[...]) + b2_ref[...]


def _compress_prompt_body(ak_ref, av_ref, w1_ref, pos_ref, b1_ref, w2_ref, b2_ref, kn_ref, bd_ref,
                          ck_o, cv_o):
    outs = []
    for kv, a_ref in enumerate((ak_ref, av_ref)):
        parts = _compress_partial(a_ref[...], w1_ref.at[kv])
        outs.append(_compress_finish(parts, w1_ref.at[kv], pos_ref.at[kv], b1_ref.at[kv],
                                     w2_ref.at[kv], b2_ref.at[kv]))
    ck_o[...] = _group_rms(outs[0], bd_ref[...], kn_ref[...])
    cv_o[...] = outs[1]


def _cmp_weight_specs(l):
    def lay(*shape):
        return pl.BlockSpec((None,) + shape, lambda *a: (l,) + (0,) * len(shape))
    flat = CMP_BLOCK * HD_B
    return [lay(2, flat, 256), lay(2, 1, flat), lay(2, 1, 256), lay(2, 512, LANE), lay(2, 1, LANE),
            lay(1, LANE), pl.BlockSpec((LANE, LANE), lambda *a: (0, 0))]


def _cmp_weights(prep):
    return (prep["w1"], prep["pos"], prep["b1"], prep["w2bd"], prep["b2"], prep["cmp_kn"], prep["bd128"])


def _compress_prompt(ck, cv, l, prep, bsz, t):
    nch = t // CMP_STRIDE
    wdt = CMP_STRIDE * LANE
    a_spec = pl.BlockSpec((nch, wdt), lambda b: (b, 0))
    o_spec = pl.BlockSpec((nch, LANE), lambda b: (b, 0))
    return pl.pallas_call(
        _compress_prompt_body,
        grid=(bsz,),
        in_specs=[a_spec, a_spec] + _cmp_weight_specs(l),
        out_specs=[o_spec, o_spec],
        out_shape=[jax.ShapeDtypeStruct((bsz * nch, LANE), F32)] * 2,
        compiler_params=_params("arbitrary"),
        name="compress_prompt",
    )(ck.reshape(bsz * nch, wdt), cv.reshape(bsz * nch, wdt), *_cmp_weights(prep))


def _nsa_q_rows(qf, g):
    m = qf.shape[0]
    zero = jnp.zeros((m, HD_B), F32)
    rows = []
    for r in range(R_B):
        c0 = (g * R_B + r) * HD_B
        piece = qf[:, c0:c0 + HD_B]
        rows.append(jnp.concatenate([piece, zero] if g == 0 else [zero, piece], axis=1))
    return jnp.concatenate(rows, axis=0).astype(BF16)


def _cmp_attention(qrows, ckc, cvc, valid):
    hi, lo = _split(ckc)
    s = _mm_nt(qrows, hi) + _mm_nt(qrows, lo)
    s = jnp.where(valid, s, NEG)
    m = jnp.max(s, axis=-1, keepdims=True)
    e = jnp.exp(s - m)
    anyv = jnp.max(jnp.where(valid, 1.0, 0.0), axis=-1, keepdims=True)
    p = e / jnp.sum(e, axis=-1, keepdims=True) * anyv
    return p, _mm(p.astype(BF16), cvc.astype(BF16))


def _select_blocks(imp_sum, ov, own, n_sel, top):
    hi, lo = _split(imp_sum)
    imp = _mm(hi, ov) + _mm(lo, ov)
    j = lax.broadcasted_iota(jnp.int32, imp.shape, 1)
    imp = jnp.where(j > own, -1.0, jnp.where((j == own) | (j == 0), FORCE, imp))
    imp = jnp.where(j >= n_sel, -2.0, imp)
    jf = j.astype(F32)
    rank = jnp.zeros(imp.shape, F32)
    for k in range(n_sel):
        col = imp[:, k:k + 1]
        rank = rank + jnp.where(col > imp, 1.0, jnp.where(col == imp, jnp.where(jf > k, 1.0, 0.0), 0.0))
    return jnp.where(rank < top, 1.0, 0.0)


def _select_blocks_t(imp_sum, ov_t, own_row, n_sel, top):
    m = imp_sum.shape[0]
    nsp = ov_t.shape[0]
    nsr = -(-n_sel // 8) * 8
    hi, lo = _split(imp_sum)
    imp = (_mm_nt(ov_t, hi) + _mm_nt(ov_t, lo))[0:nsr]
    j = lax.broadcasted_iota(jnp.int32, imp.shape, 0)
    imp = jnp.where(j > own_row, -1.0, jnp.where((j == own_row) | (j == 0), FORCE, imp))
    imp = jnp.where(j >= n_sel, -2.0, imp)
    jf = j.astype(F32)
    rank = jnp.zeros(imp.shape, F32)
    for k in range(n_sel):
        row = imp[k:k + 1, :]
        rank = rank + jnp.where(row > imp, 1.0, jnp.where(row == imp, jnp.where(jf > k, 1.0, 0.0), 0.0))
    sel_t = jnp.where(rank < top, 1.0, 0.0)
    if nsp > nsr:
        sel_t = jnp.concatenate([sel_t, jnp.zeros((nsp - nsr, m), F32)], axis=0)
    r = lax.broadcasted_iota(jnp.int32, (m, m), 0)
    c = lax.broadcasted_iota(jnp.int32, (m, m), 1)
    eye = jnp.where(r == c, 1.0, 0.0).astype(BF16)
    return _mm_nt(eye, sel_t.astype(BF16)).astype(BF16)


def _block_mask(sel_bf16, first_block, n_blocks):
    nb = sel_bf16.shape[1]
    width = n_blocks * SEL_BLOCK
    j = lax.broadcasted_iota(jnp.int32, (nb, width), 0)
    c = lax.broadcasted_iota(jnp.int32, (nb, width), 1)
    e = jnp.where(j == first_block + (c >> SEL_SHIFT), 1.0, 0.0).astype(BF16)
    return jnp.where(_mm(sel_bf16, e) > 0.5, 0.0, NEG)


def _nsa_prompt_body(*refs, i, n_cmp, n_sel, top):
    q_ref, ckc_ref, cvc_ref, kv_ref, gn_ref, gs_ref, pb_ref, ov_ref = refs[0:8]
    o_ref, s_sc = refs[-2:]
    rows = R_B * TQ
    qf = q_ref[...].astype(F32)
    gn = gn_ref[...]
    ckc = ckc_ref[...]
    cvc = cvc_ref[...]
    ncp = ckc.shape[0]
    qp = i * TQ + lax.broadcasted_iota(jnp.int32, (TQ, 1), 0)
    qp4 = i * TQ + (lax.broadcasted_iota(jnp.int32, (rows, 1), 0) & (TQ - 1))
    nidx = lax.broadcasted_iota(jnp.int32, (rows, ncp), 1)
    valid = (nidx * CMP_STRIDE + (CMP_BLOCK - 1) <= qp4) & (nidx < n_cmp)
    own_row = (i * TQ + lax.broadcasted_iota(jnp.int32, (1, TQ), 1)) >> SEL_SHIFT
    rr = lax.broadcasted_iota(jnp.int32, (rows, TQ), 0) & (TQ - 1)
    cc = lax.broadcasted_iota(jnp.int32, (rows, TQ), 1)
    anti = jnp.where(cc > rr, 0.0, NEG)
    wt = WINDOW // TQ
    bpt = TQ // SEL_BLOCK
    pieces = []
    for g in range(N_KV):
        qrows = _nsa_q_rows(qf, g)

        def bias(d, g=g):
            return jnp.concatenate([pb_ref[g * R_B + r, d] for r in range(R_B)], axis=0)

        p_c, o_cmp = _cmp_attention(qrows, ckc, cvc, valid)
        imp_sum = p_c[0:TQ]
        for r in range(1, R_B):
            imp_sum = imp_sum + p_c[r * TQ:(r + 1) * TQ]
        sel = _select_blocks_t(imp_sum, ov_ref[...], own_row, n_sel, top)

        def near_bias(s, c0, bias=bias):
            if c0 == i * TQ:
                return s + bias(0)
            if c0 == (i - 1) * TQ:
                return s + bias(1)
            return s

        def sel_score(c0, c1, qrows=qrows, sel=sel, near_bias=near_bias):
            mask = _block_mask(sel, c0 // SEL_BLOCK, (c1 - c0) // SEL_BLOCK)
            s = _mm(qrows, kv_ref[0:LANE, c0:c1]) + jnp.concatenate([mask] * R_B, axis=0)
            return near_bias(s, c0)

        def sel_value(c0, c1, p):
            return _mm_nt(p, kv_ref[LANE:2 * LANE, c0:c1])

        o_sel = _two_pass_softmax(sel_score, sel_value, _key_spans(0, i), s_sc)

        def win_score(c0, c1, qrows=qrows, near_bias=near_bias):
            s = _mm(qrows, kv_ref[2 * LANE:3 * LANE, c0:c1])
            if c0 == (i - wt) * TQ:
                s = s + anti
            return near_bias(s, c0)

        def win_value(c0, c1, p):
            return _mm_nt(p, kv_ref[3 * LANE:4 * LANE, c0:c1])

        win_spans = [(j * TQ, (j + 1) * TQ) for j in range(max(i - wt, 0), i + 1)]
        o_win = _two_pass_softmax(win_score, win_value, win_spans, s_sc)

        for r in range(R_B):
            hh = g * R_B + r
            rs = slice(r * TQ, (r + 1) * TQ)
            o = (gn[:, 3 * hh:3 * hh + 1] * o_cmp[rs] + gn[:, 3 * hh + 1:3 * hh + 2] * o_sel[rs]
                 + gn[:, 3 * hh + 2:3 * hh + 3] * o_win[rs])
            pieces.append(o[:, g * HD_B:(g + 1) * HD_B])
    o_ref[...] = (jnp.concatenate(pieces, axis=1) * gs_ref[...]).astype(BF16)


def _nsa_prompt(nq_b, ckc, cvc, kvtb, gn, gs, bias_p, l, bsz, t):
    nt = t // TQ
    nch = t // CMP_STRIDE
    n_cmp = nch - 1
    n_sel = -(-t // SEL_BLOCK)
    top = min(SEL_TOPN, n_sel)
    nsp = -(-n_sel // LANE) * LANE
    ov = jnp.asarray(_overlap_np(nch, nsp, n_cmp, n_sel).T, BF16)
    rows = R_B * TQ
    pb = bias_p[N_HA:]
    out = None
    for i in range(nt):
        w = (i + 1) * TQ
        alias = [] if out is None else [out]
        out = pl.pallas_call(
            functools.partial(_nsa_prompt_body, i=i, n_cmp=n_cmp, n_sel=n_sel, top=top),
            grid=(bsz,),
            in_specs=[pl.BlockSpec((TQ, 512), lambda b, i=i: (b * nt + i, 0)),
                      pl.BlockSpec((nch, LANE), lambda b: (b, 0)),
                      pl.BlockSpec((nch, LANE), lambda b: (b, 0)),
                      pl.BlockSpec((None, 512, w), lambda b: (b, 0, 0)),
                      pl.BlockSpec((TQ, LANE), lambda b, i=i: (b * nt + i, 0)),
                      pl.BlockSpec((TQ, 512), lambda b, i=i: (b * nt + i, 1)),
                      pl.BlockSpec((N_HB, 2, TQ, TQ), lambda b: (0, 0, 0, 0)),
                      pl.BlockSpec((nsp, nch), lambda b: (0, 0))]
                     + [pl.BlockSpec(memory_space=pl.ANY)] * len(alias),
            out_specs=pl.BlockSpec((TQ, 512), lambda b, i=i: (b * nt + i, 0)),
            out_shape=jax.ShapeDtypeStruct((bsz * t, N_HB * HD_B), BF16),
            input_output_aliases={8: 0} if alias else {},
            scratch_shapes=[pltpu.VMEM((rows, w), F32)],
            compiler_params=_params("arbitrary"),
            name=f"nsa_prompt_q{i}",
        )(nq_b, ckc, cvc, kvtb, gn, gs, pb, ov, *alias)
    return out


def _diff_decode_body(pt_ref, *refs, n_pg, dec_seq, lam_init):
    kt_pages = refs[0:n_pg]
    v_pages = refs[n_pg:2 * n_pg]
    ktt_ref, vt_ref, q_ref, gs_ref, bias_ref, lam_ref, sub_ref, o_ref, m_sc, l_sc, acc_sc = refs[2 * n_pg:]
    c = pl.program_id(1)
    last = pl.num_programs(1) - 1
    page = kt_pages[0].shape[1]
    hrows = 2 * dec_seq
    q = q_ref[...]
    blk = lax.broadcasted_iota(jnp.int32, q.shape, 1) >> HD_SHIFT
    qrows = jnp.concatenate([jnp.where(blk == hm, q, 0.0) for hm in range(2 * N_HA)], axis=0).astype(BF16)

    def values(refs_):
        def pv(p):
            outs = []
            for h in range(N_HA):
                vh = jnp.concatenate([r[pl.ds(h, page, stride=N_HA), :].astype(BF16) for r in refs_], axis=0)
                outs.append(_mm(p[h * hrows:(h + 1) * hrows], vh))
            return jnp.concatenate(outs, axis=0)
        return pv

    @pl.when(c == 0)
    def _():
        _flash_init(m_sc, l_sc, acc_sc)

    parts = [_mm(qrows, r[...].astype(BF16)) for r in kt_pages]
    parts[-1] = parts[-1] + jnp.where(c == last, bias_ref[:, 0:page], 0.0)
    _flash_update(jnp.concatenate(parts, axis=1), values(v_pages), m_sc, l_sc, acc_sc)

    @pl.when(c == last)
    def _():
        s = _mm(qrows, ktt_ref[...].astype(BF16)) + bias_ref[:, page:2 * page]
        _flash_update(s, values([vt_ref]), m_sc, l_sc, acc_sc)
        o = acc_sc[...] / l_sc[...]
        lam = _lambda(lam_ref, lam_init)
        outs = []
        for h in range(N_HA):
            r0 = h * hrows
            outs.append(_diff_finish(o[r0:r0 + dec_seq], o[r0 + dec_seq:r0 + hrows], lam, sub_ref[...],
                                     lam_init))
        o_ref[...] = jnp.concatenate(outs, axis=1) * gs_ref[...]


def _page_specs(n_pg, l, shape):
    return [pl.BlockSpec((None, None) + shape, (lambda b, c, pt, k=k: (l, pt[b, c * n_pg + k], 0, 0)))
            for k in range(n_pg)]


def _diff_decode(pt, pool_kt, pool_v, kt_tail, v_tail, dq, gs, bias_d, l, prep, lam_init):
    db, n_pages = pt.shape
    page = pool_kt.shape[3]
    dec_seq = dq.shape[0] // db
    n_pg = min(DIFF_PAGES_PER_STEP, n_pages)
    rows = 2 * N_HA * dec_seq
    wd = bias_d.shape[1]
    grid_spec = pltpu.PrefetchScalarGridSpec(
        num_scalar_prefetch=1,
        grid=(db, n_pages // n_pg),
        in_specs=_page_specs(n_pg, l, (512, page)) + _page_specs(n_pg, l, (page * N_HA, VD_A)) + [
            pl.BlockSpec((None, 512, page), lambda b, c, pt: (b, 0, 0)),
            pl.BlockSpec((None, page * N_HA, VD_A), lambda b, c, pt: (b, 0, 0)),
            pl.BlockSpec((dec_seq, 512), lambda b, c, pt: (b, 0)),
            pl.BlockSpec((dec_seq, 512), lambda b, c, pt: (b, 0)),
            pl.BlockSpec((rows, wd), lambda b, c, pt: (0, 0)),
            pl.BlockSpec((None, 4, HD_A), lambda b, c, pt: (l, 0, 0)),
            pl.BlockSpec((None, 1, VD_A), lambda b, c, pt: (l, 0, 0))],
        out_specs=pl.BlockSpec((dec_seq, 512), lambda b, c, pt: (b, 0)),
        scratch_shapes=[pltpu.VMEM((rows, 1), F32), pltpu.VMEM((rows, 1), F32),
                        pltpu.VMEM((rows, VD_A), F32)])
    return pl.pallas_call(
        functools.partial(_diff_decode_body, n_pg=n_pg, dec_seq=dec_seq, lam_init=lam_init),
        grid_spec=grid_spec,
        out_shape=jax.ShapeDtypeStruct((db * dec_seq, N_HA * VD_A), F32),
        compiler_params=_params("arbitrary", "arbitrary"),
        name="diff_decode",
    )(pt, *([pool_kt] * n_pg), *([pool_v] * n_pg), kt_tail, v_tail, dq, gs, bias_d,
      prep["lam"], prep["subln"])


def _cmp_partial_decode_body(pt_ref, *refs, n_pg):
    pages = refs[0:n_pg]
    perm_ref, w1_ref, u_ref = refs[n_pg:]
    perm = perm_ref[...]
    cpp = perm.shape[0] // CMP_STRIDE
    rows = []
    for p in pages:
        t = _mm_nt(perm, p[...].astype(BF16))
        rows.append(jnp.concatenate([t[cpp * r:cpp * (r + 1)] for r in range(CMP_STRIDE)], axis=1))
    parts = _compress_partial(jnp.concatenate(rows, axis=0), w1_ref)
    for k, part in enumerate(parts):
        u_ref[:, 256 * k:256 * (k + 1)] = part


def _cmp_partial_decode(pt, pool_t, kv, l, prep):
    db, n_pages = pt.shape
    page = pool_t.shape[3]
    cpp = page // CMP_STRIDE
    n_pg = min(CMP_PAGES_PER_STEP, n_pages)
    steps = n_pages // n_pg
    flat = CMP_BLOCK * HD_B
    grid_spec = pltpu.PrefetchScalarGridSpec(
        num_scalar_prefetch=1,
        grid=(db, steps),
        in_specs=_page_specs(n_pg, l, (LANE, page)) + [
            pl.BlockSpec((page, page), lambda b, c, pt: (0, 0)),
            pl.BlockSpec((None, None, flat, 256), lambda b, c, pt: (l, kv, 0, 0))],
        out_specs=pl.BlockSpec((n_pg * cpp, 1024), lambda b, c, pt: (b * steps + c, 0)))
    return pl.pallas_call(
        functools.partial(_cmp_partial_decode_body, n_pg=n_pg),
        grid_spec=grid_spec,
        out_shape=jax.ShapeDtypeStruct((db * n_pages * cpp, 1024), F32),
        compiler_params=_params("arbitrary", "arbitrary"),
        name="cmp_partial_decode",
    )(pt, *([pool_t] * n_pg), prep["perm"], prep["w1"])


def _nsa_select_decode_body(uk_ref, uv_ref, w1_ref, pos_ref, b1_ref, w2_ref, b2_ref, kn_ref, bd_ref,
                            q_ref, ov_ref, ocmp_o, sel_o, *, n_cmp, n_sel, top, pos0, dec_seq):
    outs = []
    for kv, u_ref in enumerate((uk_ref, uv_ref)):
        parts = [u_ref[:, 256 * k:256 * (k + 1)] for k in range(4)]
        outs.append(_compress_finish(parts, w1_ref.at[kv], pos_ref.at[kv], b1_ref.at[kv],
                                     w2_ref.at[kv], b2_ref.at[kv]))
    ckc = _group_rms(outs[0], bd_ref[...], kn_ref[...])
    cvc = outs[1]
    ncp = ckc.shape[0]
    qf = q_ref[...]
    rows = R_B * dec_seq
    qp = pos0 + lax.broadcasted_iota(jnp.int32, (dec_seq, 1), 0)
    qp4 = pos0 + (lax.broadcasted_iota(jnp.int32, (rows, 1), 0) & (dec_seq - 1))
    nidx = lax.broadcasted_iota(jnp.int32, (rows, ncp), 1)
    valid = (nidx * CMP_STRIDE + (CMP_BLOCK - 1) <= qp4) & (nidx < n_cmp)
    own = qp >> SEL_SHIFT
    for g in range(N_KV):
        qrows = _nsa_q_rows(qf, g)
        p_c, o_cmp = _cmp_attention(qrows, ckc, cvc, valid)
        imp_sum = p_c[0:dec_seq]
        for r in range(1, R_B):
            imp_sum = imp_sum + p_c[r * dec_seq:(r + 1) * dec_seq]
        sel = _select_blocks(imp_sum, ov_ref[...], own, n_sel, top)
        ocmp_o[g * rows:(g + 1) * rows, :] = o_cmp
        sel_o[g * rows:(g + 1) * rows, :] = jnp.concatenate([sel] * R_B, axis=0)


def _nsa_select_decode(uk, uv, nq, l, prep, db, pos0, dec_seq):
    nch = uk.shape[0] // db
    n_cmp = nch - 1
    n_sel = -(-(pos0 + dec_seq) // SEL_BLOCK)
    top = min(SEL_TOPN, n_sel)
    nsp = -(-n_sel // LANE) * LANE
    ov = jnp.asarray(_overlap_np(nch, nsp, n_cmp, n_sel), BF16)
    rows = N_HB * dec_seq
    u_spec = pl.BlockSpec((nch, 1024), lambda b: (b, 0))
    return pl.pallas_call(
        functools.partial(_nsa_select_decode_body, n_cmp=n_cmp, n_sel=n_sel, top=top, pos0=pos0,
                          dec_seq=dec_seq),
        grid=(db,),
        in_specs=[u_spec, u_spec] + _cmp_weight_specs(l) + [
            pl.BlockSpec((dec_seq, 512), lambda b: (b, 0)),
            pl.BlockSpec((nch, nsp), lambda b: (0, 0))],
        out_specs=[pl.BlockSpec((rows, LANE), lambda b: (b, 0)),
                   pl.BlockSpec((rows, nsp), lambda b: (b, 0))],
        out_shape=[jax.ShapeDtypeStruct((db * rows, LANE), F32),
                   jax.ShapeDtypeStruct((db * rows, nsp), F32)],
        compiler_params=_params("arbitrary"),
        name="nsa_select_decode",
    )(uk, uv, *_cmp_weights(prep), nq, ov)


def _nsa_decode_body(pt_ref, *refs, n_pg, n_pages, dec_seq):
    kt_pages = refs[0:n_pg]
    vt_pages = refs[n_pg:2 * n_pg]
    (ktt_ref, vtt_ref, wks_ref, wvs_ref, wkt_ref, wvt_ref, q_ref, sel_ref, ocmp_ref, gn_ref, gs_ref,
     bias_ref, o_ref, m_sc, l_sc, acc_sc) = refs[2 * n_pg:]
    c = pl.program_id(1)
    last = pl.num_programs(1) - 1
    page = kt_pages[0].shape[1]
    bpp = page // SEL_BLOCK
    qf = q_ref[...]
    qrows = jnp.concatenate([_nsa_q_rows(qf, g) for g in range(N_KV)], axis=0)
    sel = sel_ref[...].astype(BF16)

    def values(refs_):
        def pv(p):
            out = _mm_nt(p[:, 0:page], refs_[0][...].astype(BF16))
            for k in range(1, len(refs_)):
                out = out + _mm_nt(p[:, k * page:(k + 1) * page], refs_[k][...].astype(BF16))
            return out
        return pv

    @pl.when(c == 0)
    def _():
        _flash_init(m_sc, l_sc, acc_sc)

    parts = [_mm(qrows, r[...].astype(BF16)) for r in kt_pages]
    parts[-1] = parts[-1] + jnp.where(c == last, bias_ref[:, 0:page], 0.0)
    s = jnp.concatenate(parts, axis=1) + _block_mask(sel, c * (n_pg * bpp), n_pg * bpp)
    _flash_update(s, values(vt_pages), m_sc, l_sc, acc_sc)

    @pl.when(c == last)
    def _():
        tail_bias = bias_ref[:, page:2 * page]
        s = _mm(qrows, ktt_ref[...].astype(BF16)) + _block_mask(sel, n_pages * bpp, bpp) + tail_bias
        _flash_update(s, values([vtt_ref]), m_sc, l_sc, acc_sc)
        o_sel = acc_sc[...] / l_sc[...]
        s_w = _mm(qrows, wks_ref[...].astype(BF16)) + bias_ref[:, 2 * page:]
        s_t = _mm(qrows, wkt_ref[...].astype(BF16)) + tail_bias
        m = jnp.maximum(jnp.max(s_w, axis=-1, keepdims=True), jnp.max(s_t, axis=-1, keepdims=True))
        p_w = jnp.exp(s_w - m)
        p_t = jnp.exp(s_t - m)
        den = jnp.sum(p_w, axis=-1, keepdims=True) + jnp.sum(p_t, axis=-1, keepdims=True)
        o_win = (_mm_nt(p_w.astype(BF16), wvs_ref[...].astype(BF16))
                 + _mm_nt(p_t.astype(BF16), wvt_ref[...].astype(BF16))) / den
        o_cmp = ocmp_ref[...]
        gn = gn_ref[...]
        pieces = []
        for hh in range(N_HB):
            g = hh // R_B
            rs = slice(hh * dec_seq, (hh + 1) * dec_seq)
            o = (gn[:, 3 * hh:3 * hh + 1] * o_cmp[rs] + gn[:, 3 * hh + 1:3 * hh + 2] * o_sel[rs]
                 + gn[:, 3 * hh + 2:3 * hh + 3] * o_win[rs])
            pieces.append(o[:, g * HD_B:(g + 1) * HD_B])
        o_ref[...] = jnp.concatenate(pieces, axis=1) * gs_ref[...]


def _nsa_decode(pt, pool_kt, pool_vt, kt_tail, vt_tail, win_kt, win_vt, wkt_tail, wvt_tail, nq, sel, ocmp,
                gn, gs, bias_d, l):
    db, n_pages = pt.shape
    page = pool_kt.shape[3]
    dec_seq = nq.shape[0] // db
    n_pg = min(SEL_PAGES_PER_STEP, n_pages)
    rows = N_HB * dec_seq
    wd = bias_d.shape[1]
    wb = win_kt.shape[3]
    nsp = sel.shape[1]
    per_b = pl.BlockSpec((None, LANE, page), lambda b, c, pt: (b, 0, 0))
    win = pl.BlockSpec((None, None, LANE, wb), lambda b, c, pt: (l, b, 0, 0))
    grid_spec = pltpu.PrefetchScalarGridSpec(
        num_scalar_prefetch=1,
        grid=(db, n_pages // n_pg),
        in_specs=_page_specs(n_pg, l, (LANE, page)) + _page_specs(n_pg, l, (LANE, page)) + [
            per_b, per_b, win, win, per_b, per_b,
            pl.BlockSpec((dec_seq, 512), lambda b, c, pt: (b, 0)),
            pl.BlockSpec((rows, nsp), lambda b, c, pt: (b, 0)),
            pl.BlockSpec((rows, LANE), lambda b, c, pt: (b, 0)),
            pl.BlockSpec((dec_seq, LANE), lambda b, c, pt: (b, 0)),
            pl.BlockSpec((dec_seq, 512), lambda b, c, pt: (b, 1)),
            pl.BlockSpec((rows, wd), lambda b, c, pt: (0, 0))],
        out_specs=pl.BlockSpec((dec_seq, 512), lambda b, c, pt: (b, 0)),
        scratch_shapes=[pltpu.VMEM((rows, 1), F32), pltpu.VMEM((rows, 1), F32),
                        pltpu.VMEM((rows, LANE), F32)])
    return pl.pallas_call(
        functools.partial(_nsa_decode_body, n_pg=n_pg, n_pages=n_pages, dec_seq=dec_seq),
        grid_spec=grid_spec,
        out_shape=jax.ShapeDtypeStruct((db * dec_seq, N_HB * HD_B), F32),
        compiler_params=_params("arbitrary", "arbitrary"),
        name="nsa_decode",
    )(pt, *([pool_kt] * n_pg), *([pool_vt] * n_pg), kt_tail, vt_tail, win_kt, win_vt, wkt_tail, wvt_tail,
      nq, sel, ocmp, gn, gs, bias_d)


def _prepare(norm_g, w_in, b_in, diff_q_norm, diff_k_norm, diff_lambda, diff_subln, nsa_q_norm,
             nsa_k_norm, cmp_pos, cmp_w1, cmp_b1, cmp_w2, cmp_b2, w_branch_a, w_branch_b, w_out, page):
    depth, d, _ = w_in.shape
    c_gm = C_GN + N_GN
    pad = LANE - N_GN

    def reorder(a):
        z = jnp.zeros(a.shape[:-1] + (pad,), a.dtype)
        return jnp.concatenate([a[..., :C_GN], a[..., c_gm:c_gm + 2 * d], a[..., C_GN:c_gm], z], axis=-1)

    def t_rows(a):
        return jnp.concatenate([a[..., C_DK:C_DV], a[..., C_KV:C_GA]], axis=-1)

    scale_a = HD_A ** -0.5
    scale_b = HD_B ** -0.5
    gains = jnp.zeros((depth, 8, 512), F32)
    gains = gains.at[:, 0].set(jnp.tile(diff_q_norm, (1, 8)) * scale_a)
    gains = gains.at[:, 1].set(jnp.tile(diff_k_norm, (1, 8)))
    gains = gains.at[:, 2].set(jnp.tile(nsa_q_norm, (1, 8)) * scale_b)
    gains = gains.at[:, 3, 0:128].set(jnp.tile(nsa_k_norm[:, 1], (1, 2)))
    gains = gains.at[:, 3, 128:256].set(jnp.tile(nsa_k_norm[:, 2], (1, 2)))
    gains_t = jnp.concatenate([jnp.tile(diff_k_norm, (1, 8)), jnp.tile(nsa_k_norm[:, 1], (1, 2)),
                               jnp.tile(nsa_k_norm[:, 2], (1, 2)), jnp.zeros((depth, N_T - 768), F32)], axis=-1)
    zero = jnp.zeros_like(cmp_w2)
    w2bd = jnp.concatenate([jnp.concatenate([cmp_w2, zero], axis=-1),
                            jnp.concatenate([zero, cmp_w2], axis=-1)], axis=-2)
    return {
        "norm_g": norm_g.reshape(depth, 1, d),
        "w_in": reorder(w_in).astype(BF16),
        "b_in": reorder(b_in).reshape(depth, 1, -1),
        "w_t": jnp.swapaxes(t_rows(w_in), 1, 2).astype(BF16),
        "b_t": t_rows(b_in).reshape(depth, N_T, 1),
        "gains": gains,
        "gains_t": gains_t.reshape(depth, N_T, 1),
        "bd512": jnp.asarray(_blockdiag_np(512, HD_A), BF16),
        "bd128": jnp.asarray(_blockdiag_np(LANE, HD_B), BF16),
        "perm": jnp.asarray(_chunk_perm_np(page), BF16),
        "lam": diff_lambda,
        "subln": diff_subln.reshape(depth, 1, VD_A),
        "w1": cmp_w1.astype(BF16),
        "pos": cmp_pos.reshape(depth, 2, 1, CMP_BLOCK * HD_B),
        "b1": cmp_b1.reshape(depth, 2, 1, -1),
        "w2bd": w2bd.astype(BF16),
        "b2": jnp.tile(cmp_b2, (1, 1, 2)).reshape(depth, 2, 1, LANE),
        "cmp_kn": jnp.tile(nsa_k_norm[:, 0], (1, 2)).reshape(depth, 1, LANE),
        "w_a": w_branch_a.astype(BF16),
        "w_b": w_branch_b.astype(BF16),
        "w_out": w_out.astype(BF16),
    }


def kernel(x_prompt, x_sample, cache_diff_k, cache_diff_v, cache_cmp_k, cache_cmp_v, cache_sel_k, cache_sel_v,
           state_win_k, state_win_v, page_table, rel_bias, norm_g, w_in, b_in, diff_q_norm, diff_k_norm,
           diff_lambda, diff_subln, nsa_q_norm, nsa_k_norm, cmp_pos, cmp_w1, cmp_b1, cmp_w2, cmp_b2,
           w_branch_a, w_branch_b, w_out):
    bsz, t, d = x_prompt.shape
    db, dec_seq, _ = x_sample.shape
    depth, n_pool, page = cache_diff_k.shape[:3]
    n_pages = page_table.shape[1]
    past = n_pages * page
    wb = state_win_k.shape[2]
    assert t % TQ == 0 and WINDOW % TQ == 0 and page == LANE and wb == WINDOW and past % SEL_BLOCK == 0
    assert dec_seq <= CMP_STRIDE and dec_seq % 8 == 0 and min(WINDOW, t) % TQ == 0

    prep = _prepare(norm_g, w_in, b_in, diff_q_norm, diff_k_norm, diff_lambda, diff_subln, nsa_q_norm,
                    nsa_k_norm, cmp_pos, cmp_w1, cmp_b1, cmp_w2, cmp_b2, w_branch_a, w_branch_b, w_out, page)
    bias_p, bias_d = _bias_tiles(rel_bias, jnp.asarray(_prompt_bucket_tiles()),
                                 jnp.asarray(_decode_bucket_tiles(dec_seq, page, wb)))
    wd = bias_d.shape[-1]
    bias_d_diff = jnp.repeat(bias_d[:N_HA], 2, axis=0).reshape(2 * N_HA * dec_seq, wd)
    bias_d_nsa = bias_d[N_HA:].reshape(N_HB * dec_seq, wd)

    pool_dkt = jnp.transpose(cache_diff_k, (0, 1, 3, 4, 5, 2)).reshape(depth, n_pool, 512, page)
    pool_dv = cache_diff_v.reshape(depth, n_pool, page * N_HA, VD_A)
    def pool_t(a):
        return jnp.transpose(a, (0, 1, 3, 4, 2)).reshape(depth, a.shape[1], LANE, a.shape[2])
    pool_ckt, pool_cvt, pool_skt, pool_svt = (pool_t(a) for a in (cache_cmp_k, cache_cmp_v, cache_sel_k,
                                                                 cache_sel_v))
    win_kt, win_vt = pool_t(state_win_k), pool_t(state_win_v)

    def tail_t(a):
        a = jnp.swapaxes(a.reshape(db, dec_seq, a.shape[-1]), 1, 2)
        return jnp.pad(a, ((0, 0), (0, 0), (0, page - dec_seq)))

    xp = x_prompt.reshape(bsz * t, d)
    xs = x_sample.reshape(db * dec_seq, d)
    stacked = None
    rows_s = []
    for l in range(depth):
        lam_init = 0.8 - 0.6 * math.exp(-0.3 * l)

        outs = _in_proj_prompt(xp, l, prep, bsz, t, depth, stacked)
        dq_b, nq_b, dv_b, ck_n, cv_n, gs, gm, gn, ktb, kvtb = outs[:10]
        stacked = tuple(outs[10:])
        oa = _diff_prompt(dq_b, ktb, dv_b, gs, bias_p, l, prep, bsz, t, lam_init)
        ckc, cvc = _compress_prompt(ck_n, cv_n, l, prep, bsz, t)
        ob = _nsa_prompt(nq_b, ckc, cvc, kvtb, gn, gs, bias_p, l, bsz, t)
        xp = _out_proj(xp, oa, ob, gm, l, prep)

        dq, dk, dv, nq, kv6, gs, gm, gn = _in_proj_sample(xs, l, prep)
        ck, cv, sk, sv, wk, wv = (kv6[:, LANE * k:LANE * (k + 1)] for k in range(6))
        v_tail = jnp.pad(dv.reshape(db, dec_seq * N_HA, VD_A), ((0, 0), (0, (page - dec_seq) * N_HA), (0, 0)))
        oa = _diff_decode(page_table, pool_dkt, pool_dv, tail_t(dk), v_tail, dq, gs, bias_d_diff, l, prep,
                          lam_init)
        uk = _cmp_partial_decode(page_table, pool_ckt, 0, l, prep)
        uv = _cmp_partial_decode(page_table, pool_cvt, 1, l, prep)
        ocmp, sel = _nsa_select_decode(uk, uv, nq, l, prep, db, past, dec_seq)
        ob = _nsa_decode(page_table, pool_skt, pool_svt, tail_t(sk), tail_t(sv), win_kt, win_vt,
                         tail_t(wk), tail_t(wv), nq, sel, ocmp, gn, gs, bias_d_nsa, l)
        xs = _out_proj(xs, oa, ob, gm, l, prep)
        wk3 = wk.reshape(db, dec_seq, N_KV, HD_B)
        wv3 = wv.reshape(db, dec_seq, N_KV, HD_B)
        buf_k = jnp.concatenate([state_win_k[l], wk3], axis=1)
        buf_v = jnp.concatenate([state_win_v[l], wv3], axis=1)
        nw = min(WINDOW, buf_k.shape[1])
        rows_s.append((dk.reshape(db, dec_seq, N_HA, 2, HD_A), dv.reshape(db, dec_seq, N_HA, VD_A),
                       ck.reshape(db, dec_seq, N_KV, HD_B), cv.reshape(db, dec_seq, N_KV, HD_B),
                       sk.reshape(db, dec_seq, N_KV, HD_B), sv.reshape(db, dec_seq, N_KV, HD_B),
                       buf_k[:, -nw:], buf_v[:, -nw:]))

    dkt, dvr, ckt, cvt, skt, svt, wkt, wvt = stacked

    def untranspose(a):
        return jnp.transpose(a.reshape(depth, bsz, N_KV, HD_B, a.shape[-1]), (0, 1, 4, 2, 3))

    outs_p = [jnp.transpose(dkt.reshape(depth, bsz, N_HA, 2, HD_A, t), (0, 1, 5, 2, 3, 4)),
              dvr.reshape(depth, bsz, t, N_HA, VD_A),
              untranspose(ckt), untranspose(cvt), untranspose(skt), untranspose(svt),
              untranspose(wkt), untranspose(wvt)]
    outs_s = [jnp.stack(r, axis=0) for r in zip(*rows_s)]
    return (xp.reshape(bsz, t, d), xs.reshape(db, dec_seq, d), *outs_p, *outs_s)
```

```python
import functools
import math

import numpy as np
import jax
import jax.numpy as jnp
from jax import lax
from jax.experimental import pallas as pl
from jax.experimental.pallas import tpu as pltpu

F32 = jnp.float32
BF16 = jnp.bfloat16

N_HA, HD_A, VD_A = 4, 64, 128
N_HB, N_KV, HD_B = 8, 2, 64
R_B = N_HB // N_KV
CMP_STRIDE, CMP_BLOCK = 16, 32
SEL_BLOCK, SEL_TOPN, WINDOW = 64, 16, 512
NUM_BUCKETS, MAX_DISTANCE = 32, 128
EPS, NEG, FORCE = 1e-6, -1e30, 1e4
MASKED_BUCKET = NUM_BUCKETS
SEL_SHIFT = SEL_BLOCK.bit_length() - 1
HD_SHIFT = HD_A.bit_length() - 1

LANE = 128
TQ = 256
FAR_CHUNK = 512
VMEM_LIMIT = 56 * 1024 * 1024
DIFF_PAGES_PER_STEP = 16
SEL_PAGES_PER_STEP = 64
CMP_PAGES_PER_STEP = 64

C_DK, C_DV, C_NQ, C_KV, C_GA, C_GN = 512, 1024, 1536, 2048, 2816, 3840
N_GN = N_HB * 3
N_T = 512 + 768


def _mm(a, b):
    return jnp.dot(a, b, preferred_element_type=F32)


def _mm_nt(a, b):
    return lax.dot_general(a, b, (((1,), (1,)), ((), ())), preferred_element_type=F32)


def _split(x):
    hi = x.astype(BF16)
    lo = (x - hi.astype(F32)).astype(BF16)
    return hi, lo


def _sigmoid(z):
    return 1.0 / (1.0 + jnp.exp(-z))


def _group_rms(z, bd, gain):
    hi, lo = _split(z * z)
    ss = _mm(hi, bd) + _mm(lo, bd)
    return z * lax.rsqrt(ss * (1.0 / HD_A) + EPS) * gain


def _group_rms_t(z, gain):
    r, n = z.shape
    z3 = z.reshape(r // HD_A, HD_A, n)
    ss = jnp.sum(z3 * z3, axis=1, keepdims=True)
    return (z3 * lax.rsqrt(ss * (1.0 / HD_A) + EPS)).reshape(r, n) * gain


def _params(*sem):
    return pltpu.CompilerParams(dimension_semantics=sem, vmem_limit_bytes=VMEM_LIMIT)


def _lane_fold(x, op):
    out = x[:, 0:LANE]
    for k in range(1, x.shape[1] // LANE):
        out = op(out, x[:, k * LANE:(k + 1) * LANE])
    return out


def _key_spans(lo_tile, i):
    spans = []
    far_end = max(i - 1, lo_tile) * TQ
    c0 = lo_tile * TQ
    while c0 < far_end:
        spans.append((c0, min(c0 + FAR_CHUNK, far_end)))
        c0 = spans[-1][1]
    spans += [(j * TQ, (j + 1) * TQ) for j in range(max(i - 1, lo_tile), i + 1)]
    return spans


def _two_pass_softmax(score, value, spans, s_sc):
    mp = None
    for c0, c1 in spans:
        s = score(c0, c1)
        s_sc[:, c0:c1] = s
        f = _lane_fold(s, jnp.maximum)
        mp = f if mp is None else jnp.maximum(mp, f)
    m = jnp.max(mp, axis=-1, keepdims=True)
    lp = None
    acc = None
    for c0, c1 in spans:
        p = jnp.exp(s_sc[:, c0:c1] - m)
        f = _lane_fold(p, jnp.add)
        lp = f if lp is None else lp + f
        pv = value(c0, c1, p.astype(BF16))
        acc = pv if acc is None else acc + pv
    return acc / jnp.sum(lp, axis=-1, keepdims=True)


def _flash_update(s, pv, m_ref, l_ref, acc_ref):
    m_old = m_ref[...]
    m_new = jnp.maximum(m_old, jnp.max(s, axis=-1, keepdims=True))
    alpha = jnp.exp(m_old - m_new)
    p = jnp.exp(s - m_new)
    l_ref[...] = alpha * l_ref[...] + jnp.sum(p, axis=-1, keepdims=True)
    acc_ref[...] = alpha * acc_ref[...] + pv(p.astype(BF16))
    m_ref[...] = m_new


def _flash_init(m_ref, l_ref, acc_ref):
    m_ref[...] = jnp.full(m_ref.shape, -3e38, F32)
    l_ref[...] = jnp.zeros(l_ref.shape, F32)
    acc_ref[...] = jnp.zeros(acc_ref.shape, F32)


def _bucket_np(rel):
    n = np.maximum(rel, 0)
    exact = NUM_BUCKETS // 2
    nf = np.maximum(n, 1).astype(np.float32)
    large = exact + (np.log(nf / np.float32(exact)) / np.float32(math.log(MAX_DISTANCE / exact))
                     * np.float32(NUM_BUCKETS - exact)).astype(np.int32)
    large = np.minimum(large, NUM_BUCKETS - 1)
    return np.where(n < exact, n, large).astype(np.int32)


def _prompt_bucket_tiles():
    r = np.arange(TQ)[:, None]
    c = np.arange(TQ)[None, :]
    d0 = np.where(r - c >= 0, _bucket_np(r - c), MASKED_BUCKET)
    d1 = _bucket_np(TQ + r - c)
    return np.stack([d0, d1]).astype(np.int32)


def _decode_bucket_tiles(dec_seq, page, wb):
    q = np.arange(dec_seq)[:, None]
    c = np.arange(page)[None, :]
    last = _bucket_np(page + q - c)
    tail = np.where((c < dec_seq) & (c <= q), _bucket_np(q - c), MASKED_BUCKET)
    i = np.arange(wb)[None, :]
    relw = wb + q - i
    win = np.where(relw < WINDOW, _bucket_np(relw), MASKED_BUCKET)
    return np.concatenate([last, tail, win], axis=1).astype(np.int32)


def _overlap_np(n_rows, n_cols, n_cmp, n_sel):
    n = np.arange(n_rows)[:, None]
    j = np.arange(n_cols)[None, :]
    ov = (n * CMP_STRIDE < j * SEL_BLOCK + SEL_BLOCK) & (n * CMP_STRIDE + CMP_BLOCK > j * SEL_BLOCK)
    ov &= (n < n_cmp) & (j < n_sel)
    return ov.astype(np.float32)


def _blockdiag_np(n, group):
    i = np.arange(n)
    return (i[:, None] // group == i[None, :] // group).astype(np.float32)


def _chunk_perm_np(page):
    rows = np.arange(page)
    tok = (rows % (page // CMP_STRIDE)) * CMP_STRIDE + rows // (page // CMP_STRIDE)
    return (tok[:, None] == np.arange(page)[None, :]).astype(np.float32)


def _bias_body(tbl_ref, idxp_ref, idxd_ref, p_ref, d_ref):
    h = pl.program_id(0)
    far = tbl_ref[h, NUM_BUCKETS - 1]

    def expand(idx):
        acc = jnp.zeros(idx.shape, F32)
        for b in range(NUM_BUCKETS - 1):
            acc = jnp.where(idx == b, tbl_ref[h, b] - far, acc)
        return jnp.where(idx == MASKED_BUCKET, NEG, acc)

    p_ref[0] = expand(idxp_ref[0])
    p_ref[1] = expand(idxp_ref[1])
    d_ref[...] = expand(idxd_ref[...])


def _bias_tiles(rel_bias, idx_p, idx_d):
    nh = rel_bias.shape[0]
    dq, wd = idx_d.shape
    return pl.pallas_call(
        _bias_body,
        grid=(nh,),
        in_specs=[pl.BlockSpec(memory_space=pltpu.SMEM),
                  pl.BlockSpec((2, TQ, TQ), lambda h: (0, 0, 0)),
                  pl.BlockSpec((dq, wd), lambda h: (0, 0))],
        out_specs=[pl.BlockSpec((None, 2, TQ, TQ), lambda h: (h, 0, 0, 0)),
                   pl.BlockSpec((None, dq, wd), lambda h: (h, 0, 0))],
        out_shape=[jax.ShapeDtypeStruct((nh, 2, TQ, TQ), F32),
                   jax.ShapeDtypeStruct((nh, dq, wd), F32)],
        compiler_params=_params("arbitrary"),
        name="bias_tiles",
    )(rel_bias, idx_p, idx_d)


def _normed_input(x_ref, g_ref):
    x = x_ref[...]
    ms = jnp.mean(x * x, axis=-1, keepdims=True)
    return (x * lax.rsqrt(ms + EPS) * g_ref[...]).astype(BF16)


def _in_proj_sample_body(x_ref, g_ref, w_ref, b_ref, bd_ref, gain_ref,
                         dq_o, dk_o, dv_o, nq_o, kv_o, gs_o, gm_o, gn_o, *, d_model):
    h = _normed_input(x_ref, g_ref)

    def seg(a, b):
        return _mm(h, w_ref[:, a:b]) + b_ref[:, a:b]

    bd = bd_ref[...]
    bd2 = bd[0:LANE, 0:LANE]
    dq_o[...] = _group_rms(seg(0, C_DK), bd, gain_ref[0:1, :])
    dk_o[...] = _group_rms(seg(C_DK, C_DV), bd, gain_ref[1:2, :])
    dv_o[...] = seg(C_DV, C_NQ)
    nq_o[...] = _group_rms(seg(C_NQ, C_KV), bd, gain_ref[2:3, :])
    z = seg(C_KV, C_GA)
    kv_o[:, 0:256] = z[:, 0:256]
    kv_o[:, 256:384] = _group_rms(z[:, 256:384], bd2, gain_ref[3:4, 0:128])
    kv_o[:, 384:512] = z[:, 384:512]
    kv_o[:, 512:640] = _group_rms(z[:, 512:640], bd2, gain_ref[3:4, 128:256])
    kv_o[:, 640:768] = z[:, 640:768]
    z = seg(C_GA, C_GN)
    gs_o[...] = z * _sigmoid(z)
    gm_o[...] = _sigmoid(seg(C_GN, C_GN + 2 * d_model))
    gn_o[...] = _sigmoid(seg(C_GN + 2 * d_model, C_GN + 2 * d_model + LANE))


def _in_proj_sample(x2d, l, prep):
    n, d = x2d.shape
    tm = min(256, n)
    c = prep["w_in"].shape[-1]

    def row(w):
        return pl.BlockSpec((tm, w), lambda i: (i, 0))

    def lay(*shape):
        return pl.BlockSpec((None,) + shape, lambda i: (l,) + (0,) * len(shape))

    widths = [512, 512, 512, 512, 768, 1024, 2 * d, LANE]
    return pl.pallas_call(
        functools.partial(_in_proj_sample_body, d_model=d),
        grid=(n // tm,),
        in_specs=[row(d), lay(1, d), lay(d, c), lay(1, c),
                  pl.BlockSpec((512, 512), lambda i: (0, 0)), lay(8, 512)],
        out_specs=[row(w) for w in widths],
        out_shape=[jax.ShapeDtypeStruct((n, w), F32) for w in widths],
        compiler_params=_params("arbitrary"),
        name="in_proj_sample",
    )(x2d, prep["norm_g"], prep["w_in"], prep["b_in"], prep["bd512"], prep["gains"])


N_STACKED = 8


def _in_proj_prompt_body(*refs, d_model, n_alias):
    x_ref, g_ref, w_ref, b_ref, wt_ref, bt_ref, bd_ref, gain_ref, gaint_ref = refs[0:9]
    (dq_o, nq_o, dvb_o, ckn_o, cvn_o, gs_o, gm_o, gn_o, ktb_o, kvtb_o,
     dkt_o, dv_o, ckt_o, cvt_o, skt_o, svt_o, wkt_o, wvt_o) = refs[9 + n_alias:]
    h = _normed_input(x_ref, g_ref)
    tm = h.shape[0]

    def seg(a, b):
        return _mm(h, w_ref[:, a:b]) + b_ref[:, a:b]

    def seg_t(a, b):
        return _mm_nt(wt_ref[a:b, :], h) + bt_ref[a:b, :]

    bd = bd_ref[...]
    dq_o[...] = _group_rms(seg(0, C_DK), bd, gain_ref[0:1, :]).astype(BF16)
    nq_o[...] = _group_rms(seg(C_NQ, C_KV), bd, gain_ref[2:3, :]).astype(BF16)
    y = seg(C_DV, C_NQ)
    dvb_o[...] = y.astype(BF16)
    for hh in range(N_HA):
        dv_o[pl.ds(hh, tm, stride=N_HA), :] = y[:, hh * VD_A:(hh + 1) * VD_A]
    z = seg(C_KV, C_KV + 256)
    ckn_o[...] = z[:, 0:128]
    cvn_o[...] = z[:, 128:256]
    z = seg(C_GA, C_GN)
    gs_o[...] = z * _sigmoid(z)
    gm_o[...] = _sigmoid(seg(C_GN, C_GN + 2 * d_model))
    gn_o[...] = _sigmoid(seg(C_GN + 2 * d_model, C_GN + 2 * d_model + LANE))

    yt = _group_rms_t(seg_t(0, 512), gaint_ref[0:512, :])
    dkt_o[...] = yt
    ktb_o[...] = yt.astype(BF16)
    zt = seg_t(512, N_T)
    ckt_o[...] = zt[0:128]
    cvt_o[...] = zt[128:256]
    sk = _group_rms_t(zt[256:384], gaint_ref[512:640, :])
    sv = zt[384:512]
    wk = _group_rms_t(zt[512:640], gaint_ref[640:768, :])
    wv = zt[640:768]
    skt_o[...] = sk
    svt_o[...] = sv
    wkt_o[...] = wk
    wvt_o[...] = wv
    kvtb_o[0:128, :] = sk.astype(BF16)
    kvtb_o[128:256, :] = sv.astype(BF16)
    kvtb_o[256:384, :] = wk.astype(BF16)
    kvtb_o[384:512, :] = wv.astype(BF16)


def _in_proj_prompt(x2d, l, prep, bsz, t, depth, stacked):
    n, d = x2d.shape
    tm = TQ
    nt = t // tm
    c = prep["w_in"].shape[-1]
    nw = min(WINDOW, t)
    nwt = nw // tm

    def row(w):
        return pl.BlockSpec((tm, w), lambda i: (i, 0))

    def lay(*shape):
        return pl.BlockSpec((None,) + shape, lambda i: (l,) + (0,) * len(shape))

    def tile_t(rows):
        return pl.BlockSpec((None, rows, tm), lambda i: (i // nt, 0, i % nt))

    def stack_t(rows):
        return pl.BlockSpec((None, None, rows, tm), lambda i: (l, i // nt, 0, i % nt))

    win_t = pl.BlockSpec((None, None, LANE, tm),
                         lambda i: (l, i // nt, 0, jnp.maximum(i % nt - (nt - nwt), 0)))
    out_specs = [row(512), row(512), row(512), row(LANE), row(LANE), row(1024), row(2 * d), row(LANE),
                 tile_t(512), tile_t(512),
                 stack_t(512), pl.BlockSpec((None, tm * N_HA, VD_A), lambda i: (l, i, 0)),
                 stack_t(LANE), stack_t(LANE), stack_t(LANE), stack_t(LANE), win_t, win_t]
    kv_t = jax.ShapeDtypeStruct((depth, bsz, LANE, t), F32)
    win_shape = jax.ShapeDtypeStruct((depth, bsz, LANE, nw), F32)
    out_shape = [jax.ShapeDtypeStruct((n, 512), BF16), jax.ShapeDtypeStruct((n, 512), BF16),
                 jax.ShapeDtypeStruct((n, 512), BF16), jax.ShapeDtypeStruct((n, LANE), F32),
                 jax.ShapeDtypeStruct((n, LANE), F32), jax.ShapeDtypeStruct((n, 1024), F32),
                 jax.ShapeDtypeStruct((n, 2 * d), F32), jax.ShapeDtypeStruct((n, LANE), F32),
                 jax.ShapeDtypeStruct((bsz, 512, t), BF16), jax.ShapeDtypeStruct((bsz, 512, t), BF16),
                 jax.ShapeDtypeStruct((depth, bsz, 512, t), F32),
                 jax.ShapeDtypeStruct((depth, n * N_HA, VD_A), F32),
                 kv_t, kv_t, kv_t, kv_t, win_shape, win_shape]
    n_alias = 0 if stacked is None else N_STACKED
    n_in = 9
    n_plain = len(out_shape) - N_STACKED
    aliases = {n_in + k: n_plain + k for k in range(n_alias)}
    return pl.pallas_call(
        functools.partial(_in_proj_prompt_body, d_model=d, n_alias=n_alias),
        grid=(n // tm,),
        in_specs=[row(d), lay(1, d), lay(d, c), lay(1, c), lay(N_T, d), lay(N_T, 1),
                  pl.BlockSpec((512, 512), lambda i: (0, 0)), lay(8, 512), lay(N_T, 1)]
                 + [pl.BlockSpec(memory_space=pl.ANY)] * n_alias,
        out_specs=out_specs,
        out_shape=out_shape,
        input_output_aliases=aliases,
        compiler_params=_params("arbitrary"),
        name="in_proj_prompt",
    )(x2d, prep["norm_g"], prep["w_in"], prep["b_in"], prep["w_t"], prep["b_t"], prep["bd512"],
      prep["gains"], prep["gains_t"], *(stacked or ()))


def _out_proj_body(x_ref, oa_ref, ob_ref, gm_ref, wa_ref, wb_ref, wo_ref, y_ref, *, d_model):
    ya = _mm(oa_ref[...].astype(BF16), wa_ref[...])
    yb = _mm(ob_ref[...].astype(BF16), wb_ref[...])
    gm = gm_ref[...]
    m = gm[:, 0:d_model] * ya + gm[:, d_model:2 * d_model] * yb
    y_ref[...] = x_ref[...] + _mm(m.astype(BF16), wo_ref[...])


def _out_proj(x2d, oa, ob, gm, l, prep):
    n, d = x2d.shape
    tm = min(512, n)

    def row(w):
        return pl.BlockSpec((tm, w), lambda i: (i, 0))

    def lay(*shape):
        return pl.BlockSpec((None,) + shape, lambda i: (l,) + (0,) * len(shape))

    return pl.pallas_call(
        functools.partial(_out_proj_body, d_model=d),
        grid=(n // tm,),
        in_specs=[row(d), row(512), row(512), row(2 * d), lay(512, d), lay(512, d), lay(d, d)],
        out_specs=row(d),
        out_shape=jax.ShapeDtypeStruct((n, d), F32),
        compiler_params=_params("arbitrary"),
        name="out_proj",
    )(x2d, oa, ob, gm, prep["w_a"], prep["w_b"], prep["w_out"])


def _lambda(lam_ref, lam_init):
    lv = lam_ref[...]
    a = jnp.sum(lv[0:1] * lv[1:2], axis=-1, keepdims=True)
    b = jnp.sum(lv[2:3] * lv[3:4], axis=-1, keepdims=True)
    return jnp.exp(a) - jnp.exp(b) + lam_init


def _diff_finish(o1, o2, lam, sub, lam_init):
    o = o1 - lam * o2
    ms = jnp.mean(o * o, axis=-1, keepdims=True)
    return o * lax.rsqrt(ms + EPS) * sub * (1.0 - lam_init)


def _diff_prompt_body(*refs, i, lam_init):
    q_ref, kt_ref, v_ref, gs_ref, bias_ref, lam_ref, sub_ref = refs[0:7]
    o_ref, s_sc = refs[-2:]
    q = q_ref[...]
    lane = lax.broadcasted_iota(jnp.int32, q.shape, 1)
    zero = jnp.zeros_like(q)
    qq = jnp.concatenate([jnp.where(lane < HD_A, q, zero), jnp.where(lane >= HD_A, q, zero)], axis=0)

    def score(c0, c1):
        s = _mm(qq, kt_ref[:, c0:c1])
        if c0 >= (i - 1) * TQ:
            b = bias_ref[0 if c0 == i * TQ else 1]
            s = s + jnp.concatenate([b, b], axis=0)
        return s

    def value(c0, c1, p):
        return _mm(p, v_ref[c0:c1, :])

    o = _two_pass_softmax(score, value, _key_spans(0, i), s_sc)
    y = _diff_finish(o[0:TQ], o[TQ:2 * TQ], _lambda(lam_ref, lam_init), sub_ref[...], lam_init)
    o_ref[...] = (y * gs_ref[...]).astype(BF16)


def _diff_prompt(dq_b, ktb, dv_b, gs, bias_p, l, prep, bsz, t, lam_init):
    nt = t // TQ
    out = None
    for i in range(nt):
        w = (i + 1) * TQ
        alias = [] if out is None else [out]
        out = pl.pallas_call(
            functools.partial(_diff_prompt_body, i=i, lam_init=lam_init),
            grid=(bsz, N_HA),
            in_specs=[pl.BlockSpec((TQ, LANE), lambda b, h, i=i: (b * nt + i, h)),
                      pl.BlockSpec((None, 2 * HD_A, w), lambda b, h: (b, h, 0)),
                      pl.BlockSpec((t, LANE), lambda b, h: (b, h)),
                      pl.BlockSpec((TQ, LANE), lambda b, h, i=i: (b * nt + i, h)),
                      pl.BlockSpec((None, 2, TQ, TQ), lambda b, h: (h, 0, 0, 0)),
                      pl.BlockSpec((None, 4, HD_A), lambda b, h: (l, 0, 0)),
                      pl.BlockSpec((None, 1, VD_A), lambda b, h: (l, 0, 0))]
                     + [pl.BlockSpec(memory_space=pl.ANY)] * len(alias),
            out_specs=pl.BlockSpec((TQ, LANE), lambda b, h, i=i: (b * nt + i, h)),
            out_shape=jax.ShapeDtypeStruct((bsz * t, N_HA * VD_A), BF16),
            input_output_aliases={7: 0} if alias else {},
            scratch_shapes=[pltpu.VMEM((2 * TQ, w), F32)],
            compiler_params=_params("arbitrary", "arbitrary"),
            name=f"diff_prompt_q{i}",
        )(dq_b, ktb, dv_b, gs, bias_p, prep["lam"], prep["subln"], *alias)
    return out


def _deinterleave(a):
    lane = lax.broadcasted_iota(jnp.int32, (a.shape[0], LANE), 1)
    low = lane < HD_B
    g0, g1 = [], []
    for k in range(CMP_STRIDE // 2):
        xe = a[:, 2 * LANE * k:2 * LANE * k + LANE]
        xo = a[:, 2 * LANE * k + LANE:2 * LANE * (k + 1)]
        g0.append(jnp.where(low, xe, pltpu.roll(xo, HD_B, 1)))
        g1.append(jnp.where(low, pltpu.roll(xe, HD_B, 1), xo))
    return jnp.concatenate(g0, axis=1).astype(BF16), jnp.concatenate(g1, axis=1).astype(BF16)


def _compress_partial(a, w1_ref):
    a0, a1 = _deinterleave(a)
    half = CMP_STRIDE * HD_B
    w1a = w1_ref[0:half, :]
    w1b = w1_ref[half:2 * half, :]
    return _mm(a0, w1a), _mm(a1, w1a), _mm(a0, w1b), _mm(a1, w1b)


def _compress_finish(parts, w1_ref, pos_ref, b1_ref, w2_ref, b2_ref):
    ha0, ha1, hb0, hb1 = parts
    m = ha0.shape[0]
    pos = jnp.broadcast_to(pos_ref[...], (8, pos_ref.shape[-1])).astype(BF16)
    c = _mm(pos, w1_ref[...])[0:1] + b1_ref[...]
    h0 = ha0 + pltpu.roll(hb0, m - 1, 0) + c
    h1 = ha1 + pltpu.roll(hb1, m - 1, 0) + c
    hid = jnp.concatenate([h0 * _sigmoid(h0), h1 * _sigmoid(h1)], axis=1).astype(BF16)
    return _mm(hid, w2_ref[...]) + b2_ref[...]


def _compress_prompt_body(ak_ref, av_ref, w1_ref, pos_ref, b1_ref, w2_ref, b2_ref, kn_ref, bd_ref,
                          ck_o, cv_o):
    outs = []
    for kv, a_ref in enumerate((ak_ref, av_ref)):
        parts = _compress_partial(a_ref[...], w1_ref.at[kv])
        outs.append(_compress_finish(parts, w1_ref.at[kv], pos_ref.at[kv], b1_ref.at[kv],
                                     w2_ref.at[kv], b2_ref.at[kv]))
    ck_o[...] = _group_rms(outs[0], bd_ref[...], kn_ref[...])
    cv_o[...] = outs[1]


def _cmp_weight_specs(l):
    def lay(*shape):
        return pl.BlockSpec((None,) + shape, lambda *a: (l,) + (0,) * len(shape))
    flat = CMP_BLOCK * HD_B
    return [lay(2, flat, 256), lay(2, 1, flat), lay(2, 1, 256), lay(2, 512, LANE), lay(2, 1, LANE),
            lay(1, LANE), pl.BlockSpec((LANE, LANE), lambda *a: (0, 0))]


def _cmp_weights(prep):
    return (prep["w1"], prep["pos"], prep["b1"], prep["w2bd"], prep["b2"], prep["cmp_kn"], prep["bd128"])


def _compress_prompt(ck, cv, l, prep, bsz, t):
    nch = t // CMP_STRIDE
    wdt = CMP_STRIDE * LANE
    a_spec = pl.BlockSpec((nch, wdt), lambda b: (b, 0))
    o_spec = pl.BlockSpec((nch, LANE), lambda b: (b, 0))
    return pl.pallas_call(
        _compress_prompt_body,
        grid=(bsz,),
        in_specs=[a_spec, a_spec] + _cmp_weight_specs(l),
        out_specs=[o_spec, o_spec],
        out_shape=[jax.ShapeDtypeStruct((bsz * nch, LANE), F32)] * 2,
        compiler_params=_params("arbitrary"),
        name="compress_prompt",
    )(ck.reshape(bsz * nch, wdt), cv.reshape(bsz * nch, wdt), *_cmp_weights(prep))


def _nsa_q_rows(qf, g):
    m = qf.shape[0]
    zero = jnp.zeros((m, HD_B), F32)
    rows = []
    for r in range(R_B):
        c0 = (g * R_B + r) * HD_B
        piece = qf[:, c0:c0 + HD_B]
        rows.append(jnp.concatenate([piece, zero] if g == 0 else [zero, piece], axis=1))
    return jnp.concatenate(rows, axis=0).astype(BF16)


def _cmp_attention(qrows, ckc, cvc, valid):
    hi, lo = _split(ckc)
    s = _mm_nt(qrows, hi) + _mm_nt(qrows, lo)
    s = jnp.where(valid, s, NEG)
    m = jnp.max(s, axis=-1, keepdims=True)
    e = jnp.exp(s - m)
    anyv = jnp.max(jnp.where(valid, 1.0, 0.0), axis=-1, keepdims=True)
    p = e / jnp.sum(e, axis=-1, keepdims=True) * anyv
    return p, _mm(p.astype(BF16), cvc.astype(BF16))


def _select_blocks(imp_sum, ov, own, n_sel, top):
    hi, lo = _split(imp_sum)
    imp = _mm(hi, ov) + _mm(lo, ov)
    j = lax.broadcasted_iota(jnp.int32, imp.shape, 1)
    imp = jnp.where(j > own, -1.0, jnp.where((j == own) | (j == 0), FORCE, imp))
    imp = jnp.where(j >= n_sel, -2.0, imp)
    jf = j.astype(F32)
    rank = jnp.zeros(imp.shape, F32)
    for k in range(n_sel):
        col = imp[:, k:k + 1]
        rank = rank + jnp.where(col > imp, 1.0, jnp.where(col == imp, jnp.where(jf > k, 1.0, 0.0), 0.0))
    return jnp.where(rank < top, 1.0, 0.0)


def _select_blocks_t(imp_sum, ov_t, own_row, n_sel, top):
    m = imp_sum.shape[0]
    nsp = ov_t.shape[0]
    nsr = -(-n_sel // 8) * 8
    hi, lo = _split(imp_sum)
    imp = (_mm_nt(ov_t, hi) + _mm_nt(ov_t, lo))[0:nsr]
    j = lax.broadcasted_iota(jnp.int32, imp.shape, 0)
    imp = jnp.where(j > own_row, -1.0, jnp.where((j == own_row) | (j == 0), FORCE, imp))
    imp = jnp.where(j >= n_sel, -2.0, imp)
    jf = j.astype(F32)
    rank = jnp.zeros(imp.shape, F32)
    for k in range(n_sel):
        row = imp[k:k + 1, :]
        rank = rank + jnp.where(row > imp, 1.0, jnp.where(row == imp, jnp.where(jf > k, 1.0, 0.0), 0.0))
    sel_t = jnp.where(rank < top, 1.0, 0.0)
    if nsp > nsr:
        sel_t = jnp.concatenate([sel_t, jnp.zeros((nsp - nsr, m), F32)], axis=0)
    r = lax.broadcasted_iota(jnp.int32, (m, m), 0)
    c = lax.broadcasted_iota(jnp.int32, (m, m), 1)
    eye = jnp.where(r == c, 1.0, 0.0).astype(BF16)
    return _mm_nt(eye, sel_t.astype(BF16)).astype(BF16)


def _block_mask(sel_bf16, first_block, n_blocks):
    nb = sel_bf16.shape[1]
    width = n_blocks * SEL_BLOCK
    j = lax.broadcasted_iota(jnp.int32, (nb, width), 0)
    c = lax.broadcasted_iota(jnp.int32, (nb, width), 1)
    e = jnp.where(j == first_block + (c >> SEL_SHIFT), 1.0, 0.0).astype(BF16)
    return jnp.where(_mm(sel_bf16, e) > 0.5, 0.0, NEG)


def _nsa_prompt_body(*refs, i, n_cmp, n_sel, top):
    q_ref, ckc_ref, cvc_ref, kv_ref, gn_ref, gs_ref, pb_ref, ov_ref = refs[0:8]
    o_ref, s_sc = refs[-2:]
    rows = R_B * TQ
    qf = q_ref[...].astype(F32)
    gn = gn_ref[...]
    ckc = ckc_ref[...]
    cvc = cvc_ref[...]
    ncp = ckc.shape[0]
    qp = i * TQ + lax.broadcasted_iota(jnp.int32, (TQ, 1), 0)
    qp4 = i * TQ + (lax.broadcasted_iota(jnp.int32, (rows, 1), 0) & (TQ - 1))
    nidx = lax.broadcasted_iota(jnp.int32, (rows, ncp), 1)
    valid = (nidx * CMP_STRIDE + (CMP_BLOCK - 1) <= qp4) & (nidx < n_cmp)
    own_row = (i * TQ + lax.broadcasted_iota(jnp.int32, (1, TQ), 1)) >> SEL_SHIFT
    rr = lax.broadcasted_iota(jnp.int32, (rows, TQ), 0) & (TQ - 1)
    cc = lax.broadcasted_iota(jnp.int32, (rows, TQ), 1)
    anti = jnp.where(cc > rr, 0.0, NEG)
    wt = WINDOW // TQ
    bpt = TQ // SEL_BLOCK
    pieces = []
    for g in range(N_KV):
        qrows = _nsa_q_rows(qf, g)

        def bias(d, g=g):
            return jnp.concatenate([pb_ref[g * R_B + r, d] for r in range(R_B)], axis=0)

        p_c, o_cmp = _cmp_attention(qrows, ckc, cvc, valid)
        imp_sum = p_c[0:TQ]
        for r in range(1, R_B):
            imp_sum = imp_sum + p_c[r * TQ:(r + 1) * TQ]
        sel = _select_blocks_t(imp_sum, ov_ref[...], own_row, n_sel, top)

        def near_bias(s, c0, bias=bias):
            if c0 == i * TQ:
                return s + bias(0)
            if c0 == (i - 1) * TQ:
                return s + bias(1)
            return s

        def sel_score(c0, c1, qrows=qrows, sel=sel, near_bias=near_bias):
            mask = _block_mask(sel, c0 // SEL_BLOCK, (c1 - c0) // SEL_BLOCK)
            s = _mm(qrows, kv_ref[0:LANE, c0:c1]) + jnp.concatenate([mask] * R_B, axis=0)
            return near_bias(s, c0)

        def sel_value(c0, c1, p):
            return _mm_nt(p, kv_ref[LANE:2 * LANE, c0:c1])

        o_sel = _two_pass_softmax(sel_score, sel_value, _key_spans(0, i), s_sc)

        def win_score(c0, c1, qrows=qrows, near_bias=near_bias):
            s = _mm(qrows, kv_ref[2 * LANE:3 * LANE, c0:c1])
            if c0 == (i - wt) * TQ:
                s = s + anti
            return near_bias(s, c0)

        def win_value(c0, c1, p):
            return _mm_nt(p, kv_ref[3 * LANE:4 * LANE, c0:c1])

        win_spans = [(j * TQ, (j + 1) * TQ) for j in range(max(i - wt, 0), i + 1)]
        o_win = _two_pass_softmax(win_score, win_value, win_spans, s_sc)

        for r in range(R_B):
            hh = g * R_B + r
            rs = slice(r * TQ, (r + 1) * TQ)
            o = (gn[:, 3 * hh:3 * hh + 1] * o_cmp[rs] + gn[:, 3 * hh + 1:3 * hh + 2] * o_sel[rs]
                 + gn[:, 3 * hh + 2:3 * hh + 3] * o_win[rs])
            pieces.append(o[:, g * HD_B:(g + 1) * HD_B])
    o_ref[...] = (jnp.concatenate(pieces, axis=1) * gs_ref[...]).astype(BF16)


def _nsa_prompt(nq_b, ckc, cvc, kvtb, gn, gs, bias_p, l, bsz, t):
    nt = t // TQ
    nch = t // CMP_STRIDE
    n_cmp = nch - 1
    n_sel = -(-t // SEL_BLOCK)
    top = min(SEL_TOPN, n_sel)
    nsp = -(-n_sel // LANE) * LANE
    ov = jnp.asarray(_overlap_np(nch, nsp, n_cmp, n_sel).T, BF16)
    rows = R_B * TQ
    out = None
    for i in range(nt):
        w = (i + 1) * TQ
        alias = [] if out is None else [out]
        out = pl.pallas_call(
            functools.partial(_nsa_prompt_body, i=i, n_cmp=n_cmp, n_sel=n_sel, top=top),
            grid=(bsz,),
            in_specs=[pl.BlockSpec((TQ, 512), lambda b, i=i: (b * nt + i, 0)),
                      pl.BlockSpec((nch, LANE), lambda b: (b, 0)),
                      pl.BlockSpec((nch, LANE), lambda b: (b, 0)),
                      pl.BlockSpec((None, 512, w), lambda b: (b, 0, 0)),
                      pl.BlockSpec((TQ, LANE), lambda b, i=i: (b * nt + i, 0)),
                      pl.BlockSpec((TQ, 512), lambda b, i=i: (b * nt + i, 1)),
                      pl.BlockSpec((N_HB, 2, TQ, TQ), lambda b: (0, 0, 0, 0)),
                      pl.BlockSpec((nsp, nch), lambda b: (0, 0))]
                     + [pl.BlockSpec(memory_space=pl.ANY)] * len(alias),
            out_specs=pl.BlockSpec((TQ, 512), lambda b, i=i: (b * nt + i, 0)),
            out_shape=jax.ShapeDtypeStruct((bsz * t, N_HB * HD_B), BF16),
            input_output_aliases={8: 0} if alias else {},
            scratch_shapes=[pltpu.VMEM((rows, w), F32)],
            compiler_params=_params("arbitrary"),
            name=f"nsa_prompt_q{i}",
        )(nq_b, ckc, cvc, kvtb, gn, gs, bias_p, ov, *alias)
    return out


def _diff_decode_body(pt_ref, *refs, n_pg, dec_seq, lam_init):
    kt_pages = refs[0:n_pg]
    v_pages = refs[n_pg:2 * n_pg]
    ktt_ref, vt_ref, q_ref, gs_ref, bias_ref, lam_ref, sub_ref, o_ref, m_sc, l_sc, acc_sc = refs[2 * n_pg:]
    c = pl.program_id(1)
    last = pl.num_programs(1) - 1
    page = kt_pages[0].shape[1]
    hrows = 2 * dec_seq
    q = q_ref[...]
    blk = lax.broadcasted_iota(jnp.int32, q.shape, 1) >> HD_SHIFT
    qrows = jnp.concatenate([jnp.where(blk == hm, q, 0.0) for hm in range(2 * N_HA)], axis=0).astype(BF16)

    def values(refs_):
        def pv(p):
            outs = []
            for h in range(N_HA):
                vh = jnp.concatenate([r[pl.ds(h, page, stride=N_HA), :].astype(BF16) for r in refs_], axis=0)
                outs.append(_mm(p[h * hrows:(h + 1) * hrows], vh))
            return jnp.concatenate(outs, axis=0)
        return pv

    @pl.when(c == 0)
    def _():
        _flash_init(m_sc, l_sc, acc_sc)

    parts = [_mm(qrows, r[...].astype(BF16)) for r in kt_pages]
    parts[-1] = parts[-1] + jnp.where(c == last, bias_ref[:, 0:page], 0.0)
    _flash_update(jnp.concatenate(parts, axis=1), values(v_pages), m_sc, l_sc, acc_sc)

    @pl.when(c == last)
    def _():
        s = _mm(qrows, ktt_ref[...].astype(BF16)) + bias_ref[:, page:2 * page]
        _flash_update(s, values([vt_ref]), m_sc, l_sc, acc_sc)
        o = acc_sc[...] / l_sc[...]
        lam = _lambda(lam_ref, lam_init)
        outs = []
        for h in range(N_HA):
            r0 = h * hrows
            outs.append(_diff_finish(o[r0:r0 + dec_seq], o[r0 + dec_seq:r0 + hrows], lam, sub_ref[...],
                                     lam_init))
        o_ref[...] = jnp.concatenate(outs, axis=1) * gs_ref[...]


def _page_specs(n_pg, l, shape):
    return [pl.BlockSpec((None, None) + shape, (lambda b, c, pt, k=k: (l, pt[b, c * n_pg + k], 0, 0)))
            for k in range(n_pg)]


def _diff_decode(pt, pool_kt, pool_v, kt_tail, v_tail, dq, gs, bias_d, l, prep, lam_init):
    db, n_pages = pt.shape
    page = pool_kt.shape[3]
    dec_seq = dq.shape[0] // db
    n_pg = min(DIFF_PAGES_PER_STEP, n_pages)
    rows = 2 * N_HA * dec_seq
    wd = bias_d.shape[1]
    grid_spec = pltpu.PrefetchScalarGridSpec(
        num_scalar_prefetch=1,
        grid=(db, n_pages // n_pg),
        in_specs=_page_specs(n_pg, l, (512, page)) + _page_specs(n_pg, l, (page * N_HA, VD_A)) + [
            pl.BlockSpec((None, 512, page), lambda b, c, pt: (b, 0, 0)),
            pl.BlockSpec((None, page * N_HA, VD_A), lambda b, c, pt: (b, 0, 0)),
            pl.BlockSpec((dec_seq, 512), lambda b, c, pt: (b, 0)),
            pl.BlockSpec((dec_seq, 512), lambda b, c, pt: (b, 0)),
            pl.BlockSpec((rows, wd), lambda b, c, pt: (0, 0)),
            pl.BlockSpec((None, 4, HD_A), lambda b, c, pt: (l, 0, 0)),
            pl.BlockSpec((None, 1, VD_A), lambda b, c, pt: (l, 0, 0))],
        out_specs=pl.BlockSpec((dec_seq, 512), lambda b, c, pt: (b, 0)),
        scratch_shapes=[pltpu.VMEM((rows, 1), F32), pltpu.VMEM((rows, 1), F32),
                        pltpu.VMEM((rows, VD_A), F32)])
    return pl.pallas_call(
        functools.partial(_diff_decode_body, n_pg=n_pg, dec_seq=dec_seq, lam_init=lam_init),
        grid_spec=grid_spec,
        out_shape=jax.ShapeDtypeStruct((db * dec_seq, N_HA * VD_A), F32),
        compiler_params=_params("arbitrary", "arbitrary"),
        name="diff_decode",
    )(pt, *([pool_kt] * n_pg), *([pool_v] * n_pg), kt_tail, v_tail, dq, gs, bias_d,
      prep["lam"], prep["subln"])


def _cmp_partial_decode_body(pt_ref, *refs, n_pg):
    pages = refs[0:n_pg]
    perm_ref, w1_ref, u_ref = refs[n_pg:]
    perm = perm_ref[...]
    cpp = perm.shape[0] // CMP_STRIDE
    rows = []
    for k in range(0, n_pg, 2):
        x2 = jnp.concatenate([pages[k][...], pages[k + 1][...]], axis=0).astype(BF16)
        t2 = _mm_nt(perm, x2)
        for half in range(2):
            t = t2[:, half * LANE:(half + 1) * LANE]
            rows.append(jnp.concatenate([t[cpp * r:cpp * (r + 1)] for r in range(CMP_STRIDE)], axis=1))
    parts = _compress_partial(jnp.concatenate(rows, axis=0), w1_ref)
    for k, part in enumerate(parts):
        u_ref[:, 256 * k:256 * (k + 1)] = part


def _cmp_partial_decode(pt, pool_t, kv, l, prep):
    db, n_pages = pt.shape
    page = pool_t.shape[3]
    cpp = page // CMP_STRIDE
    n_pg = min(CMP_PAGES_PER_STEP, n_pages)
    steps = n_pages // n_pg
    flat = CMP_BLOCK * HD_B
    grid_spec = pltpu.PrefetchScalarGridSpec(
        num_scalar_prefetch=1,
        grid=(db, steps),
        in_specs=_page_specs(n_pg, l, (LANE, page)) + [
            pl.BlockSpec((page, page), lambda b, c, pt: (0, 0)),
            pl.BlockSpec((None, None, flat, 256), lambda b, c, pt: (l, kv, 0, 0))],
        out_specs=pl.BlockSpec((n_pg * cpp, 1024), lambda b, c, pt: (b * steps + c, 0)))
    return pl.pallas_call(
        functools.partial(_cmp_partial_decode_body, n_pg=n_pg),
        grid_spec=grid_spec,
        out_shape=jax.ShapeDtypeStruct((db * n_pages * cpp, 1024), F32),
        compiler_params=_params("arbitrary", "arbitrary"),
        name="cmp_partial_decode",
    )(pt, *([pool_t] * n_pg), prep["perm"], prep["w1"])


def _nsa_select_decode_body(uk_ref, uv_ref, w1_ref, pos_ref, b1_ref, w2_ref, b2_ref, kn_ref, bd_ref,
                            q_ref, ov_ref, ocmp_o, sel_o, *, n_cmp, n_sel, top, pos0, dec_seq):
    outs = []
    for kv, u_ref in enumerate((uk_ref, uv_ref)):
        parts = [u_ref[:, 256 * k:256 * (k + 1)] for k in range(4)]
        outs.append(_compress_finish(parts, w1_ref.at[kv], pos_ref.at[kv], b1_ref.at[kv],
                                     w2_ref.at[kv], b2_ref.at[kv]))
    ckc = _group_rms(outs[0], bd_ref[...], kn_ref[...])
    cvc = outs[1]
    ncp = ckc.shape[0]
    qf = q_ref[...]
    rows = R_B * dec_seq
    qp = pos0 + lax.broadcasted_iota(jnp.int32, (dec_seq, 1), 0)
    qp4 = pos0 + (lax.broadcasted_iota(jnp.int32, (rows, 1), 0) & (dec_seq - 1))
    nidx = lax.broadcasted_iota(jnp.int32, (rows, ncp), 1)
    valid = (nidx * CMP_STRIDE + (CMP_BLOCK - 1) <= qp4) & (nidx < n_cmp)
    own = qp >> SEL_SHIFT
    for g in range(N_KV):
        qrows = _nsa_q_rows(qf, g)
        p_c, o_cmp = _cmp_attention(qrows, ckc, cvc, valid)
        imp_sum = p_c[0:dec_seq]
        for r in range(1, R_B):
            imp_sum = imp_sum + p_c[r * dec_seq:(r + 1) * dec_seq]
        sel = _select_blocks(imp_sum, ov_ref[...], own, n_sel, top)
        ocmp_o[g * rows:(g + 1) * rows, :] = o_cmp
        sel_o[g * rows:(g + 1) * rows, :] = jnp.concatenate([sel] * R_B, axis=0)


def _nsa_select_decode(uk, uv, nq, l, prep, db, pos0, dec_seq):
    nch = uk.shape[0] // db
    n_cmp = nch - 1
    n_sel = -(-(pos0 + dec_seq) // SEL_BLOCK)
    top = min(SEL_TOPN, n_sel)
    nsp = -(-n_sel // LANE) * LANE
    ov = jnp.asarray(_overlap_np(nch, nsp, n_cmp, n_sel), BF16)
    rows = N_HB * dec_seq
    u_spec = pl.BlockSpec((nch, 1024), lambda b: (b, 0))
    return pl.pallas_call(
        functools.partial(_nsa_select_decode_body, n_cmp=n_cmp, n_sel=n_sel, top=top, pos0=pos0,
                          dec_seq=dec_seq),
        grid=(db,),
        in_specs=[u_spec, u_spec] + _cmp_weight_specs(l) + [
            pl.BlockSpec((dec_seq, 512), lambda b: (b, 0)),
            pl.BlockSpec((nch, nsp), lambda b: (0, 0))],
        out_specs=[pl.BlockSpec((rows, LANE), lambda b: (b, 0)),
                   pl.BlockSpec((rows, nsp), lambda b: (b, 0))],
        out_shape=[jax.ShapeDtypeStruct((db * rows, LANE), F32),
                   jax.ShapeDtypeStruct((db * rows, nsp), F32)],
        compiler_params=_params("arbitrary"),
        name="nsa_select_decode",
    )(uk, uv, *_cmp_weights(prep), nq, ov)


def _nsa_decode_body(pt_ref, *refs, n_pg, n_pages, dec_seq):
    kt_pages = refs[0:n_pg]
    vt_pages = refs[n_pg:2 * n_pg]
    (ktt_ref, vtt_ref, wks_ref, wvs_ref, wkt_ref, wvt_ref, q_ref, sel_ref, ocmp_ref, gn_ref, gs_ref,
     bias_ref) = refs[2 * n_pg:2 * n_pg + 12]
    o_ref, swk_o, swv_o, m_sc, l_sc, acc_sc = refs[-6:]
    c = pl.program_id(1)
    last = pl.num_programs(1) - 1
    page = kt_pages[0].shape[1]
    bpp = page // SEL_BLOCK
    qf = q_ref[...]
    qrows = jnp.concatenate([_nsa_q_rows(qf, g) for g in range(N_KV)], axis=0)
    sel = sel_ref[...].astype(BF16)

    def values(refs_):
        def pv(p):
            out = _mm_nt(p[:, 0:page], refs_[0][...].astype(BF16))
            for k in range(1, len(refs_)):
                out = out + _mm_nt(p[:, k * page:(k + 1) * page], refs_[k][...].astype(BF16))
            return out
        return pv

    @pl.when(c == 0)
    def _():
        _flash_init(m_sc, l_sc, acc_sc)

    parts = [_mm(qrows, r[...].astype(BF16)) for r in kt_pages]
    parts[-1] = parts[-1] + jnp.where(c == last, bias_ref[:, 0:page], 0.0)
    s = jnp.concatenate(parts, axis=1) + _block_mask(sel, c * (n_pg * bpp), n_pg * bpp)
    _flash_update(s, values(vt_pages), m_sc, l_sc, acc_sc)

    @pl.when(c == last)
    def _():
        tail_bias = bias_ref[:, page:2 * page]
        s = _mm(qrows, ktt_ref[...].astype(BF16)) + _block_mask(sel, n_pages * bpp, bpp) + tail_bias
        _flash_update(s, values([vtt_ref]), m_sc, l_sc, acc_sc)
        o_sel = acc_sc[...] / l_sc[...]
        s_w = _mm(qrows, wks_ref[...].astype(BF16)) + bias_ref[:, 2 * page:]
        s_t = _mm(qrows, wkt_ref[...].astype(BF16)) + tail_bias
        m = jnp.maximum(jnp.max(s_w, axis=-1, keepdims=True), jnp.max(s_t, axis=-1, keepdims=True))
        p_w = jnp.exp(s_w - m)
        p_t = jnp.exp(s_t - m)
        den = jnp.sum(p_w, axis=-1, keepdims=True) + jnp.sum(p_t, axis=-1, keepdims=True)
        o_win = (_mm_nt(p_w.astype(BF16), wvs_ref[...].astype(BF16))
                 + _mm_nt(p_t.astype(BF16), wvt_ref[...].astype(BF16))) / den
        o_cmp = ocmp_ref[...]
        gn = gn_ref[...]
        pieces = []
        for hh in range(N_HB):
            g = hh // R_B
            rs = slice(hh * dec_seq, (hh + 1) * dec_seq)
            o = (gn[:, 3 * hh:3 * hh + 1] * o_cmp[rs] + gn[:, 3 * hh + 1:3 * hh + 2] * o_sel[rs]
                 + gn[:, 3 * hh + 2:3 * hh + 3] * o_win[rs])
            pieces.append(o[:, g * HD_B:(g + 1) * HD_B])
        o_ref[...] = jnp.concatenate(pieces, axis=1) * gs_ref[...]
        wb = wks_ref.shape[1]
        for state_ref, tail_ref, out in ((wks_ref, wkt_ref, swk_o), (wvs_ref, wvt_ref, swv_o)):
            out[...] = jnp.concatenate([state_ref[...], tail_ref[...]], axis=1)[:, dec_seq:dec_seq + wb]


def _nsa_decode(pt, pool_kt, pool_vt, kt_tail, vt_tail, win_kt, win_vt, wkt_tail, wvt_tail, nq, sel, ocmp,
                gn, gs, bias_d, l, next_win):
    db, n_pages = pt.shape
    page = pool_kt.shape[3]
    dec_seq = nq.shape[0] // db
    n_pg = min(SEL_PAGES_PER_STEP, n_pages)
    rows = N_HB * dec_seq
    wd = bias_d.shape[1]
    wb = win_kt.shape[3]
    nsp = sel.shape[1]
    per_b = pl.BlockSpec((None, LANE, page), lambda b, c, pt: (b, 0, 0))
    win = pl.BlockSpec((None, None, LANE, wb), lambda b, c, pt: (l, b, 0, 0))
    grid_spec = pltpu.PrefetchScalarGridSpec(
        num_scalar_prefetch=1,
        grid=(db, n_pages // n_pg),
        in_specs=_page_specs(n_pg, l, (LANE, page)) + _page_specs(n_pg, l, (LANE, page)) + [
            per_b, per_b, win, win, per_b, per_b,
            pl.BlockSpec((dec_seq, 512), lambda b, c, pt: (b, 0)),
            pl.BlockSpec((rows, nsp), lambda b, c, pt: (b, 0)),
            pl.BlockSpec((rows, LANE), lambda b, c, pt: (b, 0)),
            pl.BlockSpec((dec_seq, LANE), lambda b, c, pt: (b, 0)),
            pl.BlockSpec((dec_seq, 512), lambda b, c, pt: (b, 1)),
            pl.BlockSpec((rows, wd), lambda b, c, pt: (0, 0))]
                 + [pl.BlockSpec(memory_space=pl.ANY)] * len(next_win),
        out_specs=[pl.BlockSpec((dec_seq, 512), lambda b, c, pt: (b, 0)), win, win],
        scratch_shapes=[pltpu.VMEM((rows, 1), F32), pltpu.VMEM((rows, 1), F32),
                        pltpu.VMEM((rows, LANE), F32)])
    n_in = 1 + 2 * n_pg + 12
    return pl.pallas_call(
        functools.partial(_nsa_decode_body, n_pg=n_pg, n_pages=n_pages, dec_seq=dec_seq),
        grid_spec=grid_spec,
        out_shape=[jax.ShapeDtypeStruct((db * dec_seq, N_HB * HD_B), F32),
                   jax.ShapeDtypeStruct(win_kt.shape, F32), jax.ShapeDtypeStruct(win_vt.shape, F32)],
        input_output_aliases={n_in + k: 1 + k for k in range(len(next_win))},
        compiler_params=_params("arbitrary", "arbitrary"),
        name="nsa_decode",
    )(pt, *([pool_kt] * n_pg), *([pool_vt] * n_pg), kt_tail, vt_tail, win_kt, win_vt, wkt_tail, wvt_tail,
      nq, sel, ocmp, gn, gs, bias_d, *next_win)


def _prepare(norm_g, w_in, b_in, diff_q_norm, diff_k_norm, diff_lambda, diff_subln, nsa_q_norm,
             nsa_k_norm, cmp_pos, cmp_w1, cmp_b1, cmp_w2, cmp_b2, w_branch_a, w_branch_b, w_out, page):
    depth, d, _ = w_in.shape
    c_gm = C_GN + N_GN
    pad = LANE - N_GN

    def reorder(a):
        z = jnp.zeros(a.shape[:-1] + (pad,), a.dtype)
        return jnp.concatenate([a[..., :C_GN], a[..., c_gm:c_gm + 2 * d], a[..., C_GN:c_gm], z], axis=-1)

    def t_rows(a):
        return jnp.concatenate([a[..., C_DK:C_DV], a[..., C_KV:C_GA]], axis=-1)

    scale_a = HD_A ** -0.5
    scale_b = HD_B ** -0.5
    gains = jnp.zeros((depth, 8, 512), F32)
    gains = gains.at[:, 0].set(jnp.tile(diff_q_norm, (1, 8)) * scale_a)
    gains = gains.at[:, 1].set(jnp.tile(diff_k_norm, (1, 8)))
    gains = gains.at[:, 2].set(jnp.tile(nsa_q_norm, (1, 8)) * scale_b)
    gains = gains.at[:, 3, 0:128].set(jnp.tile(nsa_k_norm[:, 1], (1, 2)))
    gains = gains.at[:, 3, 128:256].set(jnp.tile(nsa_k_norm[:, 2], (1, 2)))
    gains_t = jnp.concatenate([jnp.tile(diff_k_norm, (1, 8)), jnp.tile(nsa_k_norm[:, 1], (1, 2)),
                               jnp.tile(nsa_k_norm[:, 2], (1, 2)), jnp.zeros((depth, N_T - 768), F32)], axis=-1)
    zero = jnp.zeros_like(cmp_w2)
    w2bd = jnp.concatenate([jnp.concatenate([cmp_w2, zero], axis=-1),
                            jnp.concatenate([zero, cmp_w2], axis=-1)], axis=-2)
    return {
        "norm_g": norm_g.reshape(depth, 1, d),
        "w_in": reorder(w_in).astype(BF16),
        "b_in": reorder(b_in).reshape(depth, 1, -1),
        "w_t": jnp.swapaxes(t_rows(w_in), 1, 2).astype(BF16),
        "b_t": t_rows(b_in).reshape(depth, N_T, 1),
        "gains": gains,
        "gains_t": gains_t.reshape(depth, N_T, 1),
        "bd512": jnp.asarray(_blockdiag_np(512, HD_A), BF16),
        "bd128": jnp.asarray(_blockdiag_np(LANE, HD_B), BF16),
        "perm": jnp.asarray(_chunk_perm_np(page), BF16),
        "lam": diff_lambda,
        "subln": diff_subln.reshape(depth, 1, VD_A),
        "w1": cmp_w1.astype(BF16),
        "pos": cmp_pos.reshape(depth, 2, 1, CMP_BLOCK * HD_B),
        "b1": cmp_b1.reshape(depth, 2, 1, -1),
        "w2bd": w2bd.astype(BF16),
        "b2": jnp.tile(cmp_b2, (1, 1, 2)).reshape(depth, 2, 1, LANE),
        "cmp_kn": jnp.tile(nsa_k_norm[:, 0], (1, 2)).reshape(depth, 1, LANE),
        "w_a": w_branch_a.astype(BF16),
        "w_b": w_branch_b.astype(BF16),
        "w_out": w_out.astype(BF16),
    }


def kernel(x_prompt, x_sample, cache_diff_k, cache_diff_v, cache_cmp_k, cache_cmp_v, cache_sel_k, cache_sel_v,
           state_win_k, state_win_v, page_table, rel_bias, norm_g, w_in, b_in, diff_q_norm, diff_k_norm,
           diff_lambda, diff_subln, nsa_q_norm, nsa_k_norm, cmp_pos, cmp_w1, cmp_b1, cmp_w2, cmp_b2,
           w_branch_a, w_branch_b, w_out):
    bsz, t, d = x_prompt.shape
    db, dec_seq, _ = x_sample.shape
    depth, n_pool, page = cache_diff_k.shape[:3]
    n_pages = page_table.shape[1]
    past = n_pages * page
    wb = state_win_k.shape[2]
    assert t % TQ == 0 and WINDOW % TQ == 0 and page == LANE and wb == WINDOW and past % SEL_BLOCK == 0
    assert dec_seq <= CMP_STRIDE and dec_seq % 8 == 0 and min(WINDOW, t) % TQ == 0

    prep = _prepare(norm_g, w_in, b_in, diff_q_norm, diff_k_norm, diff_lambda, diff_subln, nsa_q_norm,
                    nsa_k_norm, cmp_pos, cmp_w1, cmp_b1, cmp_w2, cmp_b2, w_branch_a, w_branch_b, w_out, page)
    bias_p, bias_d = _bias_tiles(rel_bias, jnp.asarray(_prompt_bucket_tiles()),
                                 jnp.asarray(_decode_bucket_tiles(dec_seq, page, wb)))
    wd = bias_d.shape[-1]
    bias_d_diff = jnp.repeat(bias_d[:N_HA], 2, axis=0).reshape(2 * N_HA * dec_seq, wd)
    bias_d_nsa = bias_d[N_HA:].reshape(N_HB * dec_seq, wd)
    bias_p_nsa = bias_p[N_HA:]

    pool_dkt = jnp.transpose(cache_diff_k, (0, 1, 3, 4, 5, 2)).reshape(depth, n_pool, 512, page)
    pool_dv = cache_diff_v.reshape(depth, n_pool, page * N_HA, VD_A)
    def pool_t(a):
        return jnp.transpose(a, (0, 1, 3, 4, 2)).reshape(depth, a.shape[1], LANE, a.shape[2])
    pool_ckt, pool_cvt, pool_skt, pool_svt = (pool_t(a) for a in (cache_cmp_k, cache_cmp_v, cache_sel_k,
                                                                 cache_sel_v))
    win_kt, win_vt = pool_t(state_win_k), pool_t(state_win_v)

    def tail_t(a):
        a = jnp.swapaxes(a.reshape(db, dec_seq, a.shape[-1]), 1, 2)
        return jnp.pad(a, ((0, 0), (0, 0), (0, page - dec_seq)))

    xp = x_prompt.reshape(bsz * t, d)
    xs = x_sample.reshape(db * dec_seq, d)
    stacked = None
    next_win = []
    rows_s = []
    for l in range(depth):
        lam_init = 0.8 - 0.6 * math.exp(-0.3 * l)

        outs = _in_proj_prompt(xp, l, prep, bsz, t, depth, stacked)
        dq_b, nq_b, dv_b, ck_n, cv_n, gs, gm, gn, ktb, kvtb = outs[:10]
        stacked = tuple(outs[10:])
        oa = _diff_prompt(dq_b, ktb, dv_b, gs, bias_p, l, prep, bsz, t, lam_init)
        ckc, cvc = _compress_prompt(ck_n, cv_n, l, prep, bsz, t)
        ob = _nsa_prompt(nq_b, ckc, cvc, kvtb, gn, gs, bias_p_nsa, l, bsz, t)
        xp = _out_proj(xp, oa, ob, gm, l, prep)

        dq, dk, dv, nq, kv6, gs, gm, gn = _in_proj_sample(xs, l, prep)
        ck, cv, sk, sv, wk, wv = (kv6[:, LANE * k:LANE * (k + 1)] for k in range(6))
        v_tail = jnp.pad(dv.reshape(db, dec_seq * N_HA, VD_A), ((0, 0), (0, (page - dec_seq) * N_HA), (0, 0)))
        oa = _diff_decode(page_table, pool_dkt, pool_dv, tail_t(dk), v_tail, dq, gs, bias_d_diff, l, prep,
                          lam_init)
        uk = _cmp_partial_decode(page_table, pool_ckt, 0, l, prep)
        uv = _cmp_partial_decode(page_table, pool_cvt, 1, l, prep)
        ocmp, sel = _nsa_select_decode(uk, uv, nq, l, prep, db, past, dec_seq)
        ob, *next_win = _nsa_decode(page_table, pool_skt, pool_svt, tail_t(sk), tail_t(sv), win_kt, win_vt,
                                    tail_t(wk), tail_t(wv), nq, sel, ocmp, gn, gs, bias_d_nsa, l, next_win)
        xs = _out_proj(xs, oa, ob, gm, l, prep)
        rows_s.append((dk.reshape(db, dec_seq, N_HA, 2, HD_A), dv.reshape(db, dec_seq, N_HA, VD_A),
                       ck.reshape(db, dec_seq, N_KV, HD_B), cv.reshape(db, dec_seq, N_KV, HD_B),
                       sk.reshape(db, dec_seq, N_KV, HD_B), sv.reshape(db, dec_seq, N_KV, HD_B)))

    dkt, dvr, ckt, cvt, skt, svt, wkt, wvt = stacked

    def untranspose(a):
        return jnp.transpose(a.reshape(depth, a.shape[1], N_KV, HD_B, a.shape[-1]), (0, 1, 4, 2, 3))

    outs_p = [jnp.transpose(dkt.reshape(depth, bsz, N_HA, 2, HD_A, t), (0, 1, 5, 2, 3, 4)),
              dvr.reshape(depth, bsz, t, N_HA, VD_A),
              untranspose(ckt), untranspose(cvt), untranspose(skt), untranspose(svt),
              untranspose(wkt), untranspose(wvt)]
    outs_s = [jnp.stack(r, axis=0) for r in zip(*rows_s)] + [untranspose(a) for a in next_win]
    return (xp.reshape(bsz, t, d), xs.reshape(db, dec_seq, d), *outs_p, *outs_s)
```

```python
import functools
import math

import numpy as np
import jax
import jax.numpy as jnp
from jax import lax
from jax.experimental import pallas as pl
from jax.experimental.pallas import tpu as pltpu

F32 = jnp.float32
BF16 = jnp.bfloat16

N_HA, HD_A, VD_A = 4, 64, 128
N_HB, N_KV, HD_B = 8, 2, 64
R_B = N_HB // N_KV
CMP_STRIDE, CMP_BLOCK = 16, 32
SEL_BLOCK, SEL_TOPN, WINDOW = 64, 16, 512
NUM_BUCKETS, MAX_DISTANCE = 32, 128
EPS, NEG, FORCE = 1e-6, -1e30, 1e4
MASKED_BUCKET = NUM_BUCKETS
SEL_SHIFT = SEL_BLOCK.bit_length() - 1
HD_SHIFT = HD_A.bit_length() - 1

LANE = 128
TQ = 256
FAR_CHUNK = 512
VMEM_LIMIT = 56 * 1024 * 1024
DIFF_PAGES_PER_STEP = 16
SEL_PAGES_PER_STEP = 64

C_DK, C_DV, C_NQ, C_KV, C_GA, C_GN = 512, 1024, 1536, 2048, 2816, 3840
N_GN = N_HB * 3
N_T = 512 + 768
KV_ROWS = 640


def _mm(a, b):
    return jnp.dot(a, b, preferred_element_type=F32)


def _mm_nt(a, b):
    return lax.dot_general(a, b, (((1,), (1,)), ((), ())), preferred_element_type=F32)


def _split(x):
    hi = x.astype(BF16)
    lo = (x - hi.astype(F32)).astype(BF16)
    return hi, lo


def _sigmoid(z):
    return 1.0 / (1.0 + jnp.exp(-z))


def _group_rms(z, bd, gain):
    hi, lo = _split(z * z)
    ss = _mm(hi, bd) + _mm(lo, bd)
    return z * lax.rsqrt(ss * (1.0 / HD_A) + EPS) * gain


def _group_rms_t(z, gain):
    r, n = z.shape
    z3 = z.reshape(r // HD_A, HD_A, n)
    ss = jnp.sum(z3 * z3, axis=1, keepdims=True)
    return (z3 * lax.rsqrt(ss * (1.0 / HD_A) + EPS)).reshape(r, n) * gain


def _params(*sem):
    return pltpu.CompilerParams(dimension_semantics=sem, vmem_limit_bytes=VMEM_LIMIT)


def _lane_fold(x, op):
    out = x[:, 0:LANE]
    for k in range(1, x.shape[1] // LANE):
        out = op(out, x[:, k * LANE:(k + 1) * LANE])
    return out


def _key_spans(lo_tile, i):
    spans = []
    far_end = max(i - 1, lo_tile) * TQ
    c0 = lo_tile * TQ
    while c0 < far_end:
        spans.append((c0, min(c0 + FAR_CHUNK, far_end)))
        c0 = spans[-1][1]
    spans += [(j * TQ, (j + 1) * TQ) for j in range(max(i - 1, lo_tile), i + 1)]
    return spans


def _two_pass_softmax(score, value, spans, s_sc):
    mp = None
    for c0, c1 in spans:
        s = score(c0, c1)
        s_sc[:, c0:c1] = s
        f = _lane_fold(s, jnp.maximum)
        mp = f if mp is None else jnp.maximum(mp, f)
    m = jnp.max(mp, axis=-1, keepdims=True)
    lp = None
    acc = None
    for c0, c1 in spans:
        p = jnp.exp(s_sc[:, c0:c1] - m)
        f = _lane_fold(p, jnp.add)
        lp = f if lp is None else lp + f
        pv = value(c0, c1, p.astype(BF16))
        acc = pv if acc is None else acc + pv
    return acc * (1.0 / jnp.sum(lp, axis=-1, keepdims=True))


def _flash_update(s, pv, m_ref, l_ref, acc_ref):
    m_old = m_ref[...]
    m_new = jnp.maximum(m_old, jnp.max(s, axis=-1, keepdims=True))
    alpha = jnp.exp(m_old - m_new)
    p = jnp.exp(s - m_new)
    l_ref[...] = alpha * l_ref[...] + jnp.sum(p, axis=-1, keepdims=True)
    acc_ref[...] = alpha * acc_ref[...] + pv(p.astype(BF16))
    m_ref[...] = m_new


def _flash_init(m_ref, l_ref, acc_ref):
    m_ref[...] = jnp.full(m_ref.shape, -3e38, F32)
    l_ref[...] = jnp.zeros(l_ref.shape, F32)
    acc_ref[...] = jnp.zeros(acc_ref.shape, F32)


def _bucket_np(rel):
    n = np.maximum(rel, 0)
    exact = NUM_BUCKETS // 2
    nf = np.maximum(n, 1).astype(np.float32)
    large = exact + (np.log(nf / np.float32(exact)) / np.float32(math.log(MAX_DISTANCE / exact))
                     * np.float32(NUM_BUCKETS - exact)).astype(np.int32)
    large = np.minimum(large, NUM_BUCKETS - 1)
    return np.where(n < exact, n, large).astype(np.int32)


def _prompt_bucket_tiles():
    r = np.arange(TQ)[:, None]
    c = np.arange(TQ)[None, :]
    d0 = np.where(r - c >= 0, _bucket_np(r - c), MASKED_BUCKET)
    d1 = _bucket_np(TQ + r - c)
    return np.stack([d0, d1]).astype(np.int32)


def _decode_bucket_tiles(dec_seq, page, wb):
    q = np.arange(dec_seq)[:, None]
    c = np.arange(page)[None, :]
    last = _bucket_np(page + q - c)
    tail = np.where((c < dec_seq) & (c <= q), _bucket_np(q - c), MASKED_BUCKET)
    i = np.arange(wb)[None, :]
    relw = wb + q - i
    win = np.where(relw < WINDOW, _bucket_np(relw), MASKED_BUCKET)
    return np.concatenate([last, tail, win], axis=1).astype(np.int32)


def _overlap_np(n_rows, n_cols, n_cmp, n_sel):
    n = np.arange(n_rows)[:, None]
    j = np.arange(n_cols)[None, :]
    ov = (n * CMP_STRIDE < j * SEL_BLOCK + SEL_BLOCK) & (n * CMP_STRIDE + CMP_BLOCK > j * SEL_BLOCK)
    ov &= (n < n_cmp) & (j < n_sel)
    return ov.astype(np.float32)


def _blockdiag_np(n, group):
    i = np.arange(n)
    return (i[:, None] // group == i[None, :] // group).astype(np.float32)


def _chunk_perm_np(page):
    rows = np.arange(page)
    tok = (rows % (page // CMP_STRIDE)) * CMP_STRIDE + rows // (page // CMP_STRIDE)
    return (tok[:, None] == np.arange(page)[None, :]).astype(np.float32)


def _bias_body(tbl_ref, idxp_ref, idxd_ref, p_ref, d_ref):
    h = pl.program_id(0)
    far = tbl_ref[h, NUM_BUCKETS - 1]

    def expand(idx):
        acc = jnp.zeros(idx.shape, F32)
        for b in range(NUM_BUCKETS - 1):
            acc = jnp.where(idx == b, tbl_ref[h, b] - far, acc)
        return jnp.where(idx == MASKED_BUCKET, NEG, acc)

    p_ref[0] = expand(idxp_ref[0])
    p_ref[1] = expand(idxp_ref[1])
    d_ref[...] = expand(idxd_ref[...])


def _bias_tiles(rel_bias, idx_p, idx_d):
    nh = rel_bias.shape[0]
    dq, wd = idx_d.shape
    return pl.pallas_call(
        _bias_body,
        grid=(nh,),
        in_specs=[pl.BlockSpec(memory_space=pltpu.SMEM),
                  pl.BlockSpec((2, TQ, TQ), lambda h: (0, 0, 0)),
                  pl.BlockSpec((dq, wd), lambda h: (0, 0))],
        out_specs=[pl.BlockSpec((None, 2, TQ, TQ), lambda h: (h, 0, 0, 0)),
                   pl.BlockSpec((None, dq, wd), lambda h: (h, 0, 0))],
        out_shape=[jax.ShapeDtypeStruct((nh, 2, TQ, TQ), F32),
                   jax.ShapeDtypeStruct((nh, dq, wd), F32)],
        compiler_params=_params("arbitrary"),
        name="bias_tiles",
    )(rel_bias, idx_p, idx_d)


def _normed_input(x_ref, g_ref):
    x = x_ref[...]
    ms = jnp.mean(x * x, axis=-1, keepdims=True)
    return (x * lax.rsqrt(ms + EPS) * g_ref[...]).astype(BF16)


def _in_proj_sample_body(x_ref, g_ref, w_ref, b_ref, bd_ref, gain_ref,
                         dq_o, dk_o, dv_o, nq_o, kv_o, gs_o, gm_o, gn_o, *, d_model):
    h = _normed_input(x_ref, g_ref)

    def seg(a, b):
        return _mm(h, w_ref[:, a:b]) + b_ref[:, a:b]

    bd = bd_ref[...]
    bd2 = bd[0:LANE, 0:LANE]
    dq_o[...] = _group_rms(seg(0, C_DK), bd, gain_ref[0:1, :])
    dk_o[...] = _group_rms(seg(C_DK, C_DV), bd, gain_ref[1:2, :])
    dv_o[...] = seg(C_DV, C_NQ)
    nq_o[...] = _group_rms(seg(C_NQ, C_KV), bd, gain_ref[2:3, :])
    z = seg(C_KV, C_GA)
    kv_o[:, 0:256] = z[:, 0:256]
    kv_o[:, 256:384] = _group_rms(z[:, 256:384], bd2, gain_ref[3:4, 0:128])
    kv_o[:, 384:512] = z[:, 384:512]
    kv_o[:, 512:640] = _group_rms(z[:, 512:640], bd2, gain_ref[3:4, 128:256])
    kv_o[:, 640:768] = z[:, 640:768]
    z = seg(C_GA, C_GN)
    gs_o[...] = z * _sigmoid(z)
    gm_o[...] = _sigmoid(seg(C_GN, C_GN + 2 * d_model))
    gn_o[...] = _sigmoid(seg(C_GN + 2 * d_model, C_GN + 2 * d_model + LANE))


def _in_proj_sample(x2d, l, prep):
    n, d = x2d.shape
    tm = min(256, n)
    c = prep["w_in"].shape[-1]

    def row(w):
        return pl.BlockSpec((tm, w), lambda i: (i, 0))

    def lay(*shape):
        return pl.BlockSpec((None,) + shape, lambda i: (l,) + (0,) * len(shape))

    widths = [512, 512, 512, 512, 768, 1024, 2 * d, LANE]
    return pl.pallas_call(
        functools.partial(_in_proj_sample_body, d_model=d),
        grid=(n // tm,),
        in_specs=[row(d), lay(1, d), lay(d, c), lay(1, c),
                  pl.BlockSpec((512, 512), lambda i: (0, 0)), lay(8, 512)],
        out_specs=[row(w) for w in widths],
        out_shape=[jax.ShapeDtypeStruct((n, w), F32) for w in widths],
        compiler_params=_params("arbitrary"),
        name="in_proj_sample",
    )(x2d, prep["norm_g"], prep["w_in"], prep["b_in"], prep["bd512"], prep["gains"])


N_STACKED = 8


def _in_proj_prompt_body(*refs, d_model, n_alias, n_tiles):
    x_ref, g_ref, w_ref, b_ref, wt_ref, bt_ref, bd_ref, gain_ref, gaint_ref = refs[0:9]
    (dq_o, nq_o, dvb_o, ckn_o, cvn_o, gs_o, gm_o, gn_o, ktb_o, kvtb_o,
     dkt_o, dv_o, ckt_o, cvt_o, skt_o, svt_o, wkt_o, wvt_o) = refs[9 + n_alias:]
    h = _normed_input(x_ref, g_ref)
    tm = h.shape[0]

    def seg(a, b):
        return _mm(h, w_ref[:, a:b]) + b_ref[:, a:b]

    def seg_t(a, b):
        return _mm_nt(wt_ref[a:b, :], h) + bt_ref[a:b, :]

    bd = bd_ref[...]
    dq_o[...] = _group_rms(seg(0, C_DK), bd, gain_ref[0:1, :]).astype(BF16)
    nq_o[...] = _group_rms(seg(C_NQ, C_KV), bd, gain_ref[2:3, :]).astype(BF16)
    y = seg(C_DV, C_NQ)
    dvb_o[...] = y.astype(BF16)
    for hh in range(N_HA):
        dv_o[pl.ds(hh, tm, stride=N_HA), :] = y[:, hh * VD_A:(hh + 1) * VD_A]
    z = seg(C_KV, C_KV + 256)
    ckn_o[...] = z[:, 0:128]
    cvn_o[...] = z[:, 128:256]
    z = seg(C_GA, C_GN)
    gs_o[...] = z * _sigmoid(z)
    gm_o[...] = _sigmoid(seg(C_GN, C_GN + 2 * d_model))
    gn_o[...] = _sigmoid(seg(C_GN + 2 * d_model, C_GN + 2 * d_model + LANE))

    yt = _group_rms_t(seg_t(0, 512), gaint_ref[0:512, :])
    dkt_o[...] = yt
    ktb_o[...] = yt.astype(BF16)
    zt = seg_t(512, N_T)
    ckt_o[...] = zt[0:128]
    cvt_o[...] = zt[128:256]
    sk = _group_rms_t(zt[256:384], gaint_ref[512:640, :])
    sv = zt[384:512]
    wk = _group_rms_t(zt[512:640], gaint_ref[640:768, :])
    wv = zt[640:768]
    skt_o[...] = sk
    svt_o[...] = sv
    wkt_o[...] = wk
    wvt_o[...] = wv
    tok = (pl.program_id(0) % n_tiles) * tm + lax.broadcasted_iota(jnp.int32, (HD_B, tm), 1)
    blk = lax.broadcasted_iota(jnp.int32, (HD_B, tm), 0)
    unsel_rows = jnp.where(blk == (tok >> SEL_SHIFT), NEG, 0.0).astype(BF16)
    skb = sk.astype(BF16)
    kvtb_o[0:HD_B, :] = skb[0:HD_B]
    kvtb_o[HD_B:LANE, :] = unsel_rows
    kvtb_o[LANE:LANE + HD_B, :] = skb[HD_B:LANE]
    kvtb_o[LANE + HD_B:2 * LANE, :] = unsel_rows
    kvtb_o[256:384, :] = sv.astype(BF16)
    kvtb_o[384:512, :] = wk.astype(BF16)
    kvtb_o[512:640, :] = wv.astype(BF16)


def _in_proj_prompt(x2d, l, prep, bsz, t, depth, stacked):
    n, d = x2d.shape
    tm = TQ
    nt = t // tm
    c = prep["w_in"].shape[-1]
    nw = min(WINDOW, t)
    nwt = nw // tm

    def row(w):
        return pl.BlockSpec((tm, w), lambda i: (i, 0))

    def lay(*shape):
        return pl.BlockSpec((None,) + shape, lambda i: (l,) + (0,) * len(shape))

    def tile_t(rows):
        return pl.BlockSpec((None, rows, tm), lambda i: (i // nt, 0, i % nt))

    def stack_t(rows):
        return pl.BlockSpec((None, None, rows, tm), lambda i: (l, i // nt, 0, i % nt))

    win_t = pl.BlockSpec((None, None, LANE, tm),
                         lambda i: (l, i // nt, 0, jnp.maximum(i % nt - (nt - nwt), 0)))
    out_specs = [row(512), row(512), row(512), row(LANE), row(LANE), row(1024), row(2 * d), row(LANE),
                 tile_t(512), tile_t(KV_ROWS),
                 stack_t(512), pl.BlockSpec((None, tm * N_HA, VD_A), lambda i: (l, i, 0)),
                 stack_t(LANE), stack_t(LANE), stack_t(LANE), stack_t(LANE), win_t, win_t]
    kv_t = jax.ShapeDtypeStruct((depth, bsz, LANE, t), F32)
    win_shape = jax.ShapeDtypeStruct((depth, bsz, LANE, nw), F32)
    out_shape = [jax.ShapeDtypeStruct((n, 512), BF16), jax.ShapeDtypeStruct((n, 512), BF16),
                 jax.ShapeDtypeStruct((n, 512), BF16), jax.ShapeDtypeStruct((n, LANE), F32),
                 jax.ShapeDtypeStruct((n, LANE), F32), jax.ShapeDtypeStruct((n, 1024), F32),
                 jax.ShapeDtypeStruct((n, 2 * d), F32), jax.ShapeDtypeStruct((n, LANE), F32),
                 jax.ShapeDtypeStruct((bsz, 512, t), BF16), jax.ShapeDtypeStruct((bsz, KV_ROWS, t), BF16),
                 jax.ShapeDtypeStruct((depth, bsz, 512, t), F32),
                 jax.ShapeDtypeStruct((depth, n * N_HA, VD_A), F32),
                 kv_t, kv_t, kv_t, kv_t, win_shape, win_shape]
    n_alias = 0 if stacked is None else N_STACKED
    n_in = 9
    n_plain = len(out_shape) - N_STACKED
    aliases = {n_in + k: n_plain + k for k in range(n_alias)}
    return pl.pallas_call(
        functools.partial(_in_proj_prompt_body, d_model=d, n_alias=n_alias, n_tiles=nt),
        grid=(n // tm,),
        in_specs=[row(d), lay(1, d), lay(d, c), lay(1, c), lay(N_T, d), lay(N_T, 1),
                  pl.BlockSpec((512, 512), lambda i: (0, 0)), lay(8, 512), lay(N_T, 1)]
                 + [pl.BlockSpec(memory_space=pl.ANY)] * n_alias,
        out_specs=out_specs,
        out_shape=out_shape,
        input_output_aliases=aliases,
        compiler_params=_params("arbitrary"),
        name="in_proj_prompt",
    )(x2d, prep["norm_g"], prep["w_in"], prep["b_in"], prep["w_t"], prep["b_t"], prep["bd512"],
      prep["gains"], prep["gains_t"], *(stacked or ()))


def _out_proj_body(x_ref, oa_ref, ob_ref, gm_ref, wa_ref, wb_ref, wo_ref, y_ref, *, d_model):
    ya = _mm(oa_ref[...].astype(BF16), wa_ref[...])
    yb = _mm(ob_ref[...].astype(BF16), wb_ref[...])
    gm = gm_ref[...]
    m = gm[:, 0:d_model] * ya + gm[:, d_model:2 * d_model] * yb
    y_ref[...] = x_ref[...] + _mm(m.astype(BF16), wo_ref[...])


def _out_proj(x2d, oa, ob, gm, l, prep):
    n, d = x2d.shape
    tm = min(512, n)

    def row(w):
        return pl.BlockSpec((tm, w), lambda i: (i, 0))

    def lay(*shape):
        return pl.BlockSpec((None,) + shape, lambda i: (l,) + (0,) * len(shape))

    return pl.pallas_call(
        functools.partial(_out_proj_body, d_model=d),
        grid=(n // tm,),
        in_specs=[row(d), row(512), row(512), row(2 * d), lay(512, d), lay(512, d), lay(d, d)],
        out_specs=row(d),
        out_shape=jax.ShapeDtypeStruct((n, d), F32),
        compiler_params=_params("arbitrary"),
        name="out_proj",
    )(x2d, oa, ob, gm, prep["w_a"], prep["w_b"], prep["w_out"])


def _lambda(lam_ref, lam_init):
    lv = lam_ref[...]
    a = jnp.sum(lv[0:1] * lv[1:2], axis=-1, keepdims=True)
    b = jnp.sum(lv[2:3] * lv[3:4], axis=-1, keepdims=True)
    return jnp.exp(a) - jnp.exp(b) + lam_init


def _diff_finish(o1, o2, lam, sub, lam_init):
    o = o1 - lam * o2
    ms = jnp.mean(o * o, axis=-1, keepdims=True)
    return o * lax.rsqrt(ms + EPS) * sub * (1.0 - lam_init)


def _diff_prompt_body(*refs, i, lam_init):
    q_ref, kt_ref, v_ref, gs_ref, bias_ref, lam_ref, sub_ref = refs[0:7]
    o_ref, s_sc = refs[-2:]
    q = q_ref[...]
    lane = lax.broadcasted_iota(jnp.int32, q.shape, 1)
    zero = jnp.zeros_like(q)
    qq = jnp.concatenate([jnp.where(lane < HD_A, q, zero), jnp.where(lane >= HD_A, q, zero)], axis=0)

    def score(c0, c1):
        s = _mm(qq, kt_ref[:, c0:c1])
        if c0 >= (i - 1) * TQ:
            b = bias_ref[0 if c0 == i * TQ else 1]
            s = s + jnp.concatenate([b, b], axis=0)
        return s

    def value(c0, c1, p):
        return _mm(p, v_ref[c0:c1, :])

    o = _two_pass_softmax(score, value, _key_spans(0, i), s_sc)
    y = _diff_finish(o[0:TQ], o[TQ:2 * TQ], _lambda(lam_ref, lam_init), sub_ref[...], lam_init)
    o_ref[...] = (y * gs_ref[...]).astype(BF16)


def _diff_prompt(dq_b, ktb, dv_b, gs, bias_p, l, prep, bsz, t, lam_init):
    nt = t // TQ
    out = None
    for i in range(nt):
        w = (i + 1) * TQ
        alias = [] if out is None else [out]
        out = pl.pallas_call(
            functools.partial(_diff_prompt_body, i=i, lam_init=lam_init),
            grid=(bsz, N_HA),
            in_specs=[pl.BlockSpec((TQ, LANE), lambda b, h, i=i: (b * nt + i, h)),
                      pl.BlockSpec((None, 2 * HD_A, w), lambda b, h: (b, h, 0)),
                      pl.BlockSpec((t, LANE), lambda b, h: (b, h)),
                      pl.BlockSpec((TQ, LANE), lambda b, h, i=i: (b * nt + i, h)),
                      pl.BlockSpec((None, 2, TQ, TQ), lambda b, h: (h, 0, 0, 0)),
                      pl.BlockSpec((None, 4, HD_A), lambda b, h: (l, 0, 0)),
                      pl.BlockSpec((None, 1, VD_A), lambda b, h: (l, 0, 0))]
                     + [pl.BlockSpec(memory_space=pl.ANY)] * len(alias),
            out_specs=pl.BlockSpec((TQ, LANE), lambda b, h, i=i: (b * nt + i, h)),
            out_shape=jax.ShapeDtypeStruct((bsz * t, N_HA * VD_A), BF16),
            input_output_aliases={7: 0} if alias else {},
            scratch_shapes=[pltpu.VMEM((2 * TQ, w), F32)],
            compiler_params=_params("arbitrary", "arbitrary"),
            name=f"diff_prompt_q{i}",
        )(dq_b, ktb, dv_b, gs, bias_p, prep["lam"], prep["subln"], *alias)
    return out


def _deinterleave(a):
    lane = lax.broadcasted_iota(jnp.int32, (a.shape[0], LANE), 1)
    low = lane < HD_B
    g0, g1 = [], []
    for k in range(CMP_STRIDE // 2):
        xe = a[:, 2 * LANE * k:2 * LANE * k + LANE]
        xo = a[:, 2 * LANE * k + LANE:2 * LANE * (k + 1)]
        g0.append(jnp.where(low, xe, pltpu.roll(xo, HD_B, 1)))
        g1.append(jnp.where(low, pltpu.roll(xe, HD_B, 1), xo))
    return jnp.concatenate(g0, axis=1).astype(BF16), jnp.concatenate(g1, axis=1).astype(BF16)


def _compress_partial(a, w1_ref):
    a0, a1 = _deinterleave(a)
    half = CMP_STRIDE * HD_B
    w1a = w1_ref[0:half, :]
    w1b = w1_ref[half:2 * half, :]
    return _mm(a0, w1a), _mm(a1, w1a), _mm(a0, w1b), _mm(a1, w1b)


def _compress_finish(parts, w1_ref, pos_ref, b1_ref, w2_ref, b2_ref):
    ha0, ha1, hb0, hb1 = parts
    m = ha0.shape[0]
    pos = jnp.broadcast_to(pos_ref[...], (8, pos_ref.shape[-1])).astype(BF16)
    c = _mm(pos, w1_ref[...])[0:1] + b1_ref[...]
    h0 = ha0 + pltpu.roll(hb0, m - 1, 0) + c
    h1 = ha1 + pltpu.roll(hb1, m - 1, 0) + c
    hid = jnp.concatenate([h0 * _sigmoid(h0), h1 * _sigmoid(h1)], axis=1).astype(BF16)
    return _mm(hid, w2_ref[...]) + b2_ref[...]


def _compress_prompt_body(ak_ref, av_ref, w1_ref, pos_ref, b1_ref, w2_ref, b2_ref, kn_ref, bd_ref,
                          ck_o, cv_o):
    outs = []
    for kv, a_ref in enumerate((ak_ref, av_ref)):
        parts = _compress_partial(a_ref[...], w1_ref.at[kv])
        outs.append(_compress_finish(parts, w1_ref.at[kv], pos_ref.at[kv], b1_ref.at[kv],
                                     w2_ref.at[kv], b2_ref.at[kv]))
    ck_o[...] = _group_rms(outs[0], bd_ref[...], kn_ref[...])
    cv_o[...] = outs[1]


def _cmp_weight_specs(l):
    def lay(*shape):
        return pl.BlockSpec((None,) + shape, lambda *a: (l,) + (0,) * len(shape))
    flat = CMP_BLOCK * HD_B
    return [lay(2, flat, 256), lay(2, 1, flat), lay(2, 1, 256), lay(2, 512, LANE), lay(2, 1, LANE),
            lay(1, LANE), pl.BlockSpec((LANE, LANE), lambda *a: (0, 0))]


def _cmp_weights(prep):
    return (prep["w1"], prep["pos"], prep["b1"], prep["w2bd"], prep["b2"], prep["cmp_kn"], prep["bd128"])


def _compress_prompt(ck, cv, l, prep, bsz, t):
    nch = t // CMP_STRIDE
    wdt = CMP_STRIDE * LANE
    a_spec = pl.BlockSpec((nch, wdt), lambda b: (b, 0))
    o_spec = pl.BlockSpec((nch, LANE), lambda b: (b, 0))
    return pl.pallas_call(
        _compress_prompt_body,
        grid=(bsz,),
        in_specs=[a_spec, a_spec] + _cmp_weight_specs(l),
        out_specs=[o_spec, o_spec],
        out_shape=[jax.ShapeDtypeStruct((bsz * nch, LANE), F32)] * 2,
        compiler_params=_params("arbitrary"),
        name="compress_prompt",
    )(ck.reshape(bsz * nch, wdt), cv.reshape(bsz * nch, wdt), *_cmp_weights(prep))


def _nsa_q_rows(qf, g):
    m = qf.shape[0]
    zero = jnp.zeros((m, HD_B), F32)
    rows = []
    for r in range(R_B):
        c0 = (g * R_B + r) * HD_B
        piece = qf[:, c0:c0 + HD_B]
        rows.append(jnp.concatenate([piece, zero] if g == 0 else [zero, piece], axis=1))
    return jnp.concatenate(rows, axis=0).astype(BF16)


def _cmp_attention(qrows, ckc, cvc, valid, anyv):
    hi, lo = _split(ckc)
    s = _mm_nt(qrows, hi) + _mm_nt(qrows, lo)
    s = jnp.where(valid, s, NEG)
    m = jnp.max(s, axis=-1, keepdims=True)
    e = jnp.exp(s - m)
    ncp = e.shape[1]
    ones = jnp.ones((ncp, LANE), BF16)
    e_hi, e_lo = _split(e)
    den = _mm(e_hi, ones) + _mm(e_lo, ones)
    p = e / jnp.concatenate([den] * (ncp // LANE), axis=1) * anyv
    return p, _mm(p.astype(BF16), cvc.astype(BF16))


def _select_blocks(imp_sum, ov, own, n_sel, top):
    hi, lo = _split(imp_sum)
    imp = _mm(hi, ov) + _mm(lo, ov)
    j = lax.broadcasted_iota(jnp.int32, imp.shape, 1)
    imp = jnp.where(j > own, -1.0, jnp.where((j == own) | (j == 0), FORCE, imp))
    imp = jnp.where(j >= n_sel, -2.0, imp)
    jf = j.astype(F32)
    rank = jnp.zeros(imp.shape, F32)
    for k in range(n_sel):
        col = imp[:, k:k + 1]
        rank = rank + jnp.where(col > imp, 1.0, jnp.where(col == imp, jnp.where(jf > k, 1.0, 0.0), 0.0))
    return jnp.where(rank < top, 1.0, 0.0)


def _select_blocks_t(imp_sum, ov_t, own_row, n_sel, top):
    m = imp_sum.shape[0]
    nsp = ov_t.shape[0]
    nsr = -(-n_sel // 8) * 8
    hi, lo = _split(imp_sum)
    imp = (_mm_nt(ov_t, hi) + _mm_nt(ov_t, lo))[0:nsr]
    j = lax.broadcasted_iota(jnp.int32, imp.shape, 0)
    imp = jnp.where(j > own_row, -1.0, jnp.where((j == own_row) | (j == 0), FORCE, imp))
    imp = jnp.where(j >= n_sel, -2.0, imp)
    jf = j.astype(F32)
    rank = jnp.zeros(imp.shape, F32)
    for k in range(n_sel):
        row = imp[k:k + 1, :]
        rank = rank + jnp.where(row > imp, 1.0, jnp.where(row == imp, jnp.where(jf > k, 1.0, 0.0), 0.0))
    sel_t = jnp.where(rank < top, 1.0, 0.0)
    if nsp > nsr:
        sel_t = jnp.concatenate([sel_t, jnp.zeros((nsp - nsr, m), F32)], axis=0)
    r = lax.broadcasted_iota(jnp.int32, (m, m), 0)
    c = lax.broadcasted_iota(jnp.int32, (m, m), 1)
    eye = jnp.where(r == c, 1.0, 0.0).astype(BF16)
    return _mm_nt(eye, sel_t.astype(BF16)).astype(BF16)


def _block_mask(sel_bf16, first_block, n_blocks):
    nb = sel_bf16.shape[1]
    width = n_blocks * SEL_BLOCK
    j = lax.broadcasted_iota(jnp.int32, (nb, width), 0)
    c = lax.broadcasted_iota(jnp.int32, (nb, width), 1)
    e = jnp.where(j == first_block + (c >> SEL_SHIFT), 1.0, 0.0).astype(BF16)
    return jnp.where(_mm(sel_bf16, e) > 0.5, 0.0, NEG)


def _nsa_prompt_body(*refs, i, n_cmp, n_sel, top):
    q_ref, ckc_ref, cvc_ref, kv_ref, gn_ref, gs_ref, pb_ref, ov_ref = refs[0:8]
    o_ref, s_sc = refs[-2:]
    rows = R_B * TQ
    qf = q_ref[...].astype(F32)
    gn = gn_ref[...]
    ckc = ckc_ref[...]
    cvc = cvc_ref[...]
    ncp = ckc.shape[0]
    qp = i * TQ + lax.broadcasted_iota(jnp.int32, (TQ, 1), 0)
    qp4 = i * TQ + (lax.broadcasted_iota(jnp.int32, (rows, 1), 0) & (TQ - 1))
    nidx = lax.broadcasted_iota(jnp.int32, (rows, ncp), 1)
    valid = (nidx * CMP_STRIDE + (CMP_BLOCK - 1) <= qp4) & (nidx < n_cmp)
    anyv = jnp.where(qp4 >= CMP_BLOCK - 1, 1.0, 0.0) * float(n_cmp > 0)
    own_row = (i * TQ + lax.broadcasted_iota(jnp.int32, (1, TQ), 1)) >> SEL_SHIFT
    rr = lax.broadcasted_iota(jnp.int32, (rows, TQ), 0) & (TQ - 1)
    cc = lax.broadcasted_iota(jnp.int32, (rows, TQ), 1)
    anti = jnp.where(cc > rr, 0.0, NEG)
    wt = WINDOW // TQ
    bpt = TQ // SEL_BLOCK
    pieces = []
    for g in range(N_KV):
        qrows = _nsa_q_rows(qf, g)

        def bias(d, g=g):
            return jnp.concatenate([pb_ref[g * R_B + r, d] for r in range(R_B)], axis=0)

        p_c, o_cmp = _cmp_attention(qrows, ckc, cvc, valid, anyv)
        imp_sum = p_c[0:TQ]
        for r in range(1, R_B):
            imp_sum = imp_sum + p_c[r * TQ:(r + 1) * TQ]
        sel = _select_blocks_t(imp_sum, ov_ref[...], own_row, n_sel, top)

        def near_bias(s, c0, bias=bias):
            if c0 == i * TQ:
                return s + bias(0)
            if c0 == (i - 1) * TQ:
                return s + bias(1)
            return s

        unsel = (1.0 - sel.astype(F32))[:, 0:HD_B]
        qsel = jnp.concatenate(
            [jnp.concatenate([qf[:, (g * R_B + r) * HD_B:(g * R_B + r + 1) * HD_B], unsel], axis=1)
             for r in range(R_B)], axis=0).astype(BF16)

        def sel_score(c0, c1, g=g, qsel=qsel, near_bias=near_bias):
            return near_bias(_mm(qsel, kv_ref[g * LANE:(g + 1) * LANE, c0:c1]), c0)

        def sel_value(c0, c1, p):
            return _mm_nt(p, kv_ref[2 * LANE:3 * LANE, c0:c1])

        o_sel = _two_pass_softmax(sel_score, sel_value, _key_spans(0, i), s_sc)

        def win_score(c0, c1, qrows=qrows, near_bias=near_bias):
            s = _mm(qrows, kv_ref[3 * LANE:4 * LANE, c0:c1])
            if c0 == (i - wt) * TQ:
                s = s + anti
            return near_bias(s, c0)

        def win_value(c0, c1, p):
            return _mm_nt(p, kv_ref[4 * LANE:5 * LANE, c0:c1])

        win_spans = [(j * TQ, (j + 1) * TQ) for j in range(max(i - wt, 0), i + 1)]
        o_win = _two_pass_softmax(win_score, win_value, win_spans, s_sc)

        for r in range(R_B):
            hh = g * R_B + r
            rs = slice(r * TQ, (r + 1) * TQ)
            o = (gn[:, 3 * hh:3 * hh + 1] * o_cmp[rs] + gn[:, 3 * hh + 1:3 * hh + 2] * o_sel[rs]
                 + gn[:, 3 * hh + 2:3 * hh + 3] * o_win[rs])
            pieces.append(o[:, g * HD_B:(g + 1) * HD_B])
    o_ref[...] = (jnp.concatenate(pieces, axis=1) * gs_ref[...]).astype(BF16)


def _nsa_prompt(nq_b, ckc, cvc, kvtb, gn, gs, bias_p, l, bsz, t):
    nt = t // TQ
    nch = t // CMP_STRIDE
    n_cmp = nch - 1
    n_sel = -(-t // SEL_BLOCK)
    top = min(SEL_TOPN, n_sel)
    nsp = -(-n_sel // LANE) * LANE
    ov = jnp.asarray(_overlap_np(nch, nsp, n_cmp, n_sel).T, BF16)
    rows = R_B * TQ
    out = None
    for i in range(nt):
        w = (i + 1) * TQ
        alias = [] if out is None else [out]
        out = pl.pallas_call(
            functools.partial(_nsa_prompt_body, i=i, n_cmp=n_cmp, n_sel=n_sel, top=top),
            grid=(bsz,),
            in_specs=[pl.BlockSpec((TQ, 512), lambda b, i=i: (b * nt + i, 0)),
                      pl.BlockSpec((nch, LANE), lambda b: (b, 0)),
                      pl.BlockSpec((nch, LANE), lambda b: (b, 0)),
                      pl.BlockSpec((None, KV_ROWS, w), lambda b: (b, 0, 0)),
                      pl.BlockSpec((TQ, LANE), lambda b, i=i: (b * nt + i, 0)),
                      pl.BlockSpec((TQ, 512), lambda b, i=i: (b * nt + i, 1)),
                      pl.BlockSpec((N_HB, 2, TQ, TQ), lambda b: (0, 0, 0, 0)),
                      pl.BlockSpec((nsp, nch), lambda b: (0, 0))]
                     + [pl.BlockSpec(memory_space=pl.ANY)] * len(alias),
            out_specs=pl.BlockSpec((TQ, 512), lambda b, i=i: (b * nt + i, 0)),
            out_shape=jax.ShapeDtypeStruct((bsz * t, N_HB * HD_B), BF16),
            input_output_aliases={8: 0} if alias else {},
            scratch_shapes=[pltpu.VMEM((rows, w), F32)],
            compiler_params=_params("arbitrary"),
            name=f"nsa_prompt_q{i}",
        )(nq_b, ckc, cvc, kvtb, gn, gs, bias_p, ov, *alias)
    return out


def _diff_decode_body(pt_ref, *refs, n_pg, dec_seq, lam_init):
    kt_pages = refs[0:n_pg]
    v_pages = refs[n_pg:2 * n_pg]
    ktt_ref, vt_ref, q_ref, gs_ref, bias_ref, lam_ref, sub_ref, o_ref, m_sc, l_sc, acc_sc = refs[2 * n_pg:]
    c = pl.program_id(1)
    last = pl.num_programs(1) - 1
    page = kt_pages[0].shape[1]
    hrows = 2 * dec_seq
    q = q_ref[...]
    blk = lax.broadcasted_iota(jnp.int32, q.shape, 1) >> HD_SHIFT
    qrows = jnp.concatenate([jnp.where(blk == hm, q, 0.0) for hm in range(2 * N_HA)], axis=0).astype(BF16)

    def values(refs_):
        def pv(p):
            outs = []
            for h in range(N_HA):
                vh = jnp.concatenate([r[pl.ds(h, page, stride=N_HA), :].astype(BF16) for r in refs_], axis=0)
                outs.append(_mm(p[h * hrows:(h + 1) * hrows], vh))
            return jnp.concatenate(outs, axis=0)
        return pv

    @pl.when(c == 0)
    def _():
        _flash_init(m_sc, l_sc, acc_sc)

    parts = [_mm(qrows, r[...].astype(BF16)) for r in kt_pages]
    parts[-1] = parts[-1] + jnp.where(c == last, bias_ref[:, 0:page], 0.0)
    _flash_update(jnp.concatenate(parts, axis=1), values(v_pages), m_sc, l_sc, acc_sc)

    @pl.when(c == last)
    def _():
        s = _mm(qrows, ktt_ref[...].astype(BF16)) + bias_ref[:, page:2 * page]
        _flash_update(s, values([vt_ref]), m_sc, l_sc, acc_sc)
        o = acc_sc[...] / l_sc[...]
        lam = _lambda(lam_ref, lam_init)
        outs = []
        for h in range(N_HA):
            r0 = h * hrows
            outs.append(_diff_finish(o[r0:r0 + dec_seq], o[r0 + dec_seq:r0 + hrows], lam, sub_ref[...],
                                     lam_init))
        o_ref[...] = jnp.concatenate(outs, axis=1) * gs_ref[...]


def _page_specs(n_pg, l, shape):
    return [pl.BlockSpec((None, None) + shape, (lambda b, c, pt, k=k: (l, pt[b, c * n_pg + k], 0, 0)))
            for k in range(n_pg)]


def _diff_decode(pt, pool_kt, pool_v, kt_tail, v_tail, dq, gs, bias_d, l, prep, lam_init):
    db, n_pages = pt.shape
    page = pool_kt.shape[3]
    dec_seq = dq.shape[0] // db
    n_pg = min(DIFF_PAGES_PER_STEP, n_pages)
    rows = 2 * N_HA * dec_seq
    wd = bias_d.shape[1]
    grid_spec = pltpu.PrefetchScalarGridSpec(
        num_scalar_prefetch=1,
        grid=(db, n_pages // n_pg),
        in_specs=_page_specs(n_pg, l, (512, page)) + _page_specs(n_pg, l, (page * N_HA, VD_A)) + [
            pl.BlockSpec((None, 512, page), lambda b, c, pt: (b, 0, 0)),
            pl.BlockSpec((None, page * N_HA, VD_A), lambda b, c, pt: (b, 0, 0)),
            pl.BlockSpec((dec_seq, 512), lambda b, c, pt: (b, 0)),
            pl.BlockSpec((dec_seq, 512), lambda b, c, pt: (b, 0)),
            pl.BlockSpec((rows, wd), lambda b, c, pt: (0, 0)),
            pl.BlockSpec((None, 4, HD_A), lambda b, c, pt: (l, 0, 0)),
            pl.BlockSpec((None, 1, VD_A), lambda b, c, pt: (l, 0, 0))],
        out_specs=pl.BlockSpec((dec_seq, 512), lambda b, c, pt: (b, 0)),
        scratch_shapes=[pltpu.VMEM((rows, 1), F32), pltpu.VMEM((rows, 1), F32),
                        pltpu.VMEM((rows, VD_A), F32)])
    return pl.pallas_call(
        functools.partial(_diff_decode_body, n_pg=n_pg, dec_seq=dec_seq, lam_init=lam_init),
        grid_spec=grid_spec,
        out_shape=jax.ShapeDtypeStruct((db * dec_seq, N_HA * VD_A), F32),
        compiler_params=_params("arbitrary", "arbitrary"),
        name="diff_decode",
    )(pt, *([pool_kt] * n_pg), *([pool_v] * n_pg), kt_tail, v_tail, dq, gs, bias_d,
      prep["lam"], prep["subln"])


def _chunk_rows(pages, perm):
    cpp = perm.shape[0] // CMP_STRIDE
    rows = []
    for k in range(0, len(pages), 2):
        x2 = jnp.concatenate([pages[k][...], pages[k + 1][...]], axis=0).astype(BF16)
        t2 = _mm_nt(perm, x2)
        for half in range(2):
            t = t2[:, half * LANE:(half + 1) * LANE]
            rows.append(jnp.concatenate([t[cpp * r:cpp * (r + 1)] for r in range(CMP_STRIDE)], axis=1))
    return jnp.concatenate(rows, axis=0)


def _nsa_select_decode_body(pt_ref, *refs, n_pg, n_cmp, n_sel, top, pos0, dec_seq):
    page_sets = (refs[0:n_pg], refs[n_pg:2 * n_pg])
    (perm_ref, w1_ref, pos_ref, b1_ref, w2_ref, b2_ref, kn_ref, bd_ref, q_ref, ov_ref,
     ocmp_o, sel_o) = refs[2 * n_pg:]
    outs = []
    for kv, pages in enumerate(page_sets):
        parts = _compress_partial(_chunk_rows(pages, perm_ref[...]), w1_ref.at[kv])
        outs.append(_compress_finish(parts, w1_ref.at[kv], pos_ref.at[kv], b1_ref.at[kv],
                                     w2_ref.at[kv], b2_ref.at[kv]))
    ckc = _group_rms(outs[0], bd_ref[...], kn_ref[...])
    cvc = outs[1]
    ncp = ckc.shape[0]
    qf = q_ref[...]
    rows = R_B * dec_seq
    qp = pos0 + lax.broadcasted_iota(jnp.int32, (dec_seq, 1), 0)
    qp4 = pos0 + (lax.broadcasted_iota(jnp.int32, (rows, 1), 0) & (dec_seq - 1))
    nidx = lax.broadcasted_iota(jnp.int32, (rows, ncp), 1)
    valid = (nidx * CMP_STRIDE + (CMP_BLOCK - 1) <= qp4) & (nidx < n_cmp)
    own = qp >> SEL_SHIFT
    anyv = jnp.where(qp4 >= CMP_BLOCK - 1, 1.0, 0.0) * float(n_cmp > 0)
    for g in range(N_KV):
        qrows = _nsa_q_rows(qf, g)
        p_c, o_cmp = _cmp_attention(qrows, ckc, cvc, valid, anyv)
        imp_sum = p_c[0:dec_seq]
        for r in range(1, R_B):
            imp_sum = imp_sum + p_c[r * dec_seq:(r + 1) * dec_seq]
        sel = _select_blocks(imp_sum, ov_ref[...], own, n_sel, top)
        ocmp_o[g * rows:(g + 1) * rows, :] = o_cmp
        sel_o[g * rows:(g + 1) * rows, :] = jnp.concatenate([sel] * R_B, axis=0)


def _nsa_select_decode(pt, pool_kt, pool_vt, nq, l, prep, pos0, dec_seq):
    db, n_pages = pt.shape
    page = pool_kt.shape[3]
    nch = n_pages * (page // CMP_STRIDE)
    n_cmp = nch - 1
    n_sel = -(-(pos0 + dec_seq) // SEL_BLOCK)
    top = min(SEL_TOPN, n_sel)
    nsp = -(-n_sel // LANE) * LANE
    ov = jnp.asarray(_overlap_np(nch, nsp, n_cmp, n_sel), BF16)
    rows = N_HB * dec_seq
    pages = [pl.BlockSpec((None, None, LANE, page), (lambda b, pt, k=k: (l, pt[b, k], 0, 0)))
             for k in range(n_pages)]
    grid_spec = pltpu.PrefetchScalarGridSpec(
        num_scalar_prefetch=1,
        grid=(db,),
        in_specs=pages + pages + [pl.BlockSpec((page, page), lambda b, pt: (0, 0))] + _cmp_weight_specs(l) + [
            pl.BlockSpec((dec_seq, 512), lambda b, pt: (b, 0)),
            pl.BlockSpec((nch, nsp), lambda b, pt: (0, 0))],
        out_specs=[pl.BlockSpec((rows, LANE), lambda b, pt: (b, 0)),
                   pl.BlockSpec((rows, nsp), lambda b, pt: (b, 0))])
    return pl.pallas_call(
        functools.partial(_nsa_select_decode_body, n_pg=n_pages, n_cmp=n_cmp, n_sel=n_sel, top=top,
                          pos0=pos0, dec_seq=dec_seq),
        grid_spec=grid_spec,
        out_shape=[jax.ShapeDtypeStruct((db * rows, LANE), F32),
                   jax.ShapeDtypeStruct((db * rows, nsp), F32)],
        compiler_params=_params("arbitrary"),
        name="nsa_select_decode",
    )(pt, *([pool_kt] * n_pages), *([pool_vt] * n_pages), prep["perm"], *_cmp_weights(prep), nq, ov)


def _nsa_decode_body(pt_ref, *refs, n_pg, n_pages, dec_seq):
    kt_pages = refs[0:n_pg]
    vt_pages = refs[n_pg:2 * n_pg]
    (ktt_ref, vtt_ref, wks_ref, wvs_ref, wkt_ref, wvt_ref, q_ref, sel_ref, ocmp_ref, gn_ref, gs_ref,
     bias_ref) = refs[2 * n_pg:2 * n_pg + 12]
    o_ref, swk_o, swv_o, m_sc, l_sc, acc_sc = refs[-6:]
    c = pl.program_id(1)
    last = pl.num_programs(1) - 1
    page = kt_pages[0].shape[1]
    bpp = page // SEL_BLOCK
    qf = q_ref[...]
    qrows = jnp.concatenate([_nsa_q_rows(qf, g) for g in range(N_KV)], axis=0)
    sel = sel_ref[...].astype(BF16)

    def values(refs_):
        def pv(p):
            out = _mm_nt(p[:, 0:page], refs_[0][...].astype(BF16))
            for k in range(1, len(refs_)):
                out = out + _mm_nt(p[:, k * page:(k + 1) * page], refs_[k][...].astype(BF16))
            return out
        return pv

    @pl.when(c == 0)
    def _():
        _flash_init(m_sc, l_sc, acc_sc)

    parts = [_mm(qrows, r[...].astype(BF16)) for r in kt_pages]
    parts[-1] = parts[-1] + jnp.where(c == last, bias_ref[:, 0:page], 0.0)
    s = jnp.concatenate(parts, axis=1) + _block_mask(sel, c * (n_pg * bpp), n_pg * bpp)
    _flash_update(s, values(vt_pages), m_sc, l_sc, acc_sc)

    @pl.when(c == last)
    def _():
        tail_bias = bias_ref[:, page:2 * page]
        s = _mm(qrows, ktt_ref[...].astype(BF16)) + _block_mask(sel, n_pages * bpp, bpp) + tail_bias
        _flash_update(s, values([vtt_ref]), m_sc, l_sc, acc_sc)
        o_sel = acc_sc[...] / l_sc[...]
        s_w = _mm(qrows, wks_ref[...].astype(BF16)) + bias_ref[:, 2 * page:]
        s_t = _mm(qrows, wkt_ref[...].astype(BF16)) + tail_bias
        m = jnp.maximum(jnp.max(s_w, axis=-1, keepdims=True), jnp.max(s_t, axis=-1, keepdims=True))
        p_w = jnp.exp(s_w - m)
        p_t = jnp.exp(s_t - m)
        den = jnp.sum(p_w, axis=-1, keepdims=True) + jnp.sum(p_t, axis=-1, keepdims=True)
        o_win = (_mm_nt(p_w.astype(BF16), wvs_ref[...].astype(BF16))
                 + _mm_nt(p_t.astype(BF16), wvt_ref[...].astype(BF16))) / den
        o_cmp = ocmp_ref[...]
        gn = gn_ref[...]
        pieces = []
        for hh in range(N_HB):
            g = hh // R_B
            rs = slice(hh * dec_seq, (hh + 1) * dec_seq)
            o = (gn[:, 3 * hh:3 * hh + 1] * o_cmp[rs] + gn[:, 3 * hh + 1:3 * hh + 2] * o_sel[rs]
                 + gn[:, 3 * hh + 2:3 * hh + 3] * o_win[rs])
            pieces.append(o[:, g * HD_B:(g + 1) * HD_B])
        o_ref[...] = jnp.concatenate(pieces, axis=1) * gs_ref[...]
        wb = wks_ref.shape[1]
        for state_ref, tail_ref, out in ((wks_ref, wkt_ref, swk_o), (wvs_ref, wvt_ref, swv_o)):
            out[...] = jnp.concatenate([state_ref[...], tail_ref[...]], axis=1)[:, dec_seq:dec_seq + wb]


def _nsa_decode(pt, pool_kt, pool_vt, kt_tail, vt_tail, win_kt, win_vt, wkt_tail, wvt_tail, nq, sel, ocmp,
                gn, gs, bias_d, l, next_win):
    db, n_pages = pt.shape
    page = pool_kt.shape[3]
    dec_seq = nq.shape[0] // db
    n_pg = min(SEL_PAGES_PER_STEP, n_pages)
    rows = N_HB * dec_seq
    wd = bias_d.shape[1]
    wb = win_kt.shape[3]
    nsp = sel.shape[1]
    per_b = pl.BlockSpec((None, LANE, page), lambda b, c, pt: (b, 0, 0))
    win = pl.BlockSpec((None, None, LANE, wb), lambda b, c, pt: (l, b, 0, 0))
    grid_spec = pltpu.PrefetchScalarGridSpec(
        num_scalar_prefetch=1,
        grid=(db, n_pages // n_pg),
        in_specs=_page_specs(n_pg, l, (LANE, page)) + _page_specs(n_pg, l, (LANE, page)) + [
            per_b, per_b, win, win, per_b, per_b,
            pl.BlockSpec((dec_seq, 512), lambda b, c, pt: (b, 0)),
            pl.BlockSpec((rows, nsp), lambda b, c, pt: (b, 0)),
            pl.BlockSpec((rows, LANE), lambda b, c, pt: (b, 0)),
            pl.BlockSpec((dec_seq, LANE), lambda b, c, pt: (b, 0)),
            pl.BlockSpec((dec_seq, 512), lambda b, c, pt: (b, 1)),
            pl.BlockSpec((rows, wd), lambda b, c, pt: (0, 0))]
                 + [pl.BlockSpec(memory_space=pl.ANY)] * len(next_win),
        out_specs=[pl.BlockSpec((dec_seq, 512), lambda b, c, pt: (b, 0)), win, win],
        scratch_shapes=[pltpu.VMEM((rows, 1), F32), pltpu.VMEM((rows, 1), F32),
                        pltpu.VMEM((rows, LANE), F32)])
    n_in = 1 + 2 * n_pg + 12
    return pl.pallas_call(
        functools.partial(_nsa_decode_body, n_pg=n_pg, n_pages=n_pages, dec_seq=dec_seq),
        grid_spec=grid_spec,
        out_shape=[jax.ShapeDtypeStruct((db * dec_seq, N_HB * HD_B), F32),
                   jax.ShapeDtypeStruct(win_kt.shape, F32), jax.ShapeDtypeStruct(win_vt.shape, F32)],
        input_output_aliases={n_in + k: 1 + k for k in range(len(next_win))},
        compiler_params=_params("arbitrary", "arbitrary"),
        name="nsa_decode",
    )(pt, *([pool_kt] * n_pg), *([pool_vt] * n_pg), kt_tail, vt_tail, win_kt, win_vt, wkt_tail, wvt_tail,
      nq, sel, ocmp, gn, gs, bias_d, *next_win)


def _prepare(norm_g, w_in, b_in, diff_q_norm, diff_k_norm, diff_lambda, diff_subln, nsa_q_norm,
             nsa_k_norm, cmp_pos, cmp_w1, cmp_b1, cmp_w2, cmp_b2, w_branch_a, w_branch_b, w_out, page):
    depth, d, _ = w_in.shape
    c_gm = C_GN + N_GN
    pad = LANE - N_GN

    def reorder(a):
        z = jnp.zeros(a.shape[:-1] + (pad,), a.dtype)
        return jnp.concatenate([a[..., :C_GN], a[..., c_gm:c_gm + 2 * d], a[..., C_GN:c_gm], z], axis=-1)

    def t_rows(a):
        return jnp.concatenate([a[..., C_DK:C_DV], a[..., C_KV:C_GA]], axis=-1)

    scale_a = HD_A ** -0.5
    scale_b = HD_B ** -0.5
    gains = jnp.zeros((depth, 8, 512), F32)
    gains = gains.at[:, 0].set(jnp.tile(diff_q_norm, (1, 8)) * scale_a)
    gains = gains.at[:, 1].set(jnp.tile(diff_k_norm, (1, 8)))
    gains = gains.at[:, 2].set(jnp.tile(nsa_q_norm, (1, 8)) * scale_b)
    gains = gains.at[:, 3, 0:128].set(jnp.tile(nsa_k_norm[:, 1], (1, 2)))
    gains = gains.at[:, 3, 128:256].set(jnp.tile(nsa_k_norm[:, 2], (1, 2)))
    gains_t = jnp.concatenate([jnp.tile(diff_k_norm, (1, 8)), jnp.tile(nsa_k_norm[:, 1], (1, 2)),
                               jnp.tile(nsa_k_norm[:, 2], (1, 2)), jnp.zeros((depth, N_T - 768), F32)], axis=-1)
    zero = jnp.zeros_like(cmp_w2)
    w2bd = jnp.concatenate([jnp.concatenate([cmp_w2, zero], axis=-1),
                            jnp.concatenate([zero, cmp_w2], axis=-1)], axis=-2)
    return {
        "norm_g": norm_g.reshape(depth, 1, d),
        "w_in": reorder(w_in).astype(BF16),
        "b_in": reorder(b_in).reshape(depth, 1, -1),
        "w_t": jnp.swapaxes(t_rows(w_in), 1, 2).astype(BF16),
        "b_t": t_rows(b_in).reshape(depth, N_T, 1),
        "gains": gains,
        "gains_t": gains_t.reshape(depth, N_T, 1),
        "bd512": jnp.asarray(_blockdiag_np(512, HD_A), BF16),
        "bd128": jnp.asarray(_blockdiag_np(LANE, HD_B), BF16),
        "perm": jnp.asarray(_chunk_perm_np(page), BF16),
        "lam": diff_lambda,
        "subln": diff_subln.reshape(depth, 1, VD_A),
        "w1": cmp_w1.astype(BF16),
        "pos": cmp_pos.reshape(depth, 2, 1, CMP_BLOCK * HD_B),
        "b1": cmp_b1.reshape(depth, 2, 1, -1),
        "w2bd": w2bd.astype(BF16),
        "b2": jnp.tile(cmp_b2, (1, 1, 2)).reshape(depth, 2, 1, LANE),
        "cmp_kn": jnp.tile(nsa_k_norm[:, 0], (1, 2)).reshape(depth, 1, LANE),
        "w_a": w_branch_a.astype(BF16),
        "w_b": w_branch_b.astype(BF16),
        "w_out": w_out.astype(BF16),
    }


def kernel(x_prompt, x_sample, cache_diff_k, cache_diff_v, cache_cmp_k, cache_cmp_v, cache_sel_k, cache_sel_v,
           state_win_k, state_win_v, page_table, rel_bias, norm_g, w_in, b_in, diff_q_norm, diff_k_norm,
           diff_lambda, diff_subln, nsa_q_norm, nsa_k_norm, cmp_pos, cmp_w1, cmp_b1, cmp_w2, cmp_b2,
           w_branch_a, w_branch_b, w_out):
    bsz, t, d = x_prompt.shape
    db, dec_seq, _ = x_sample.shape
    depth, n_pool, page = cache_diff_k.shape[:3]
    n_pages = page_table.shape[1]
    past = n_pages * page
    wb = state_win_k.shape[2]
    assert t % TQ == 0 and WINDOW % TQ == 0 and page == LANE and wb == WINDOW and past % SEL_BLOCK == 0
    assert dec_seq <= CMP_STRIDE and dec_seq % 8 == 0 and min(WINDOW, t) % TQ == 0
    assert -(-t // SEL_BLOCK) <= HD_B

    prep = _prepare(norm_g, w_in, b_in, diff_q_norm, diff_k_norm, diff_lambda, diff_subln, nsa_q_norm,
                    nsa_k_norm, cmp_pos, cmp_w1, cmp_b1, cmp_w2, cmp_b2, w_branch_a, w_branch_b, w_out, page)
    bias_p, bias_d = _bias_tiles(rel_bias, jnp.asarray(_prompt_bucket_tiles()),
                                 jnp.asarray(_decode_bucket_tiles(dec_seq, page, wb)))
    wd = bias_d.shape[-1]
    bias_d_diff = jnp.repeat(bias_d[:N_HA], 2, axis=0).reshape(2 * N_HA * dec_seq, wd)
    bias_d_nsa = bias_d[N_HA:].reshape(N_HB * dec_seq, wd)
    bias_p_nsa = bias_p[N_HA:]

    pool_dkt = jnp.transpose(cache_diff_k, (0, 1, 3, 4, 5, 2)).reshape(depth, n_pool, 512, page)
    pool_dv = cache_diff_v.reshape(depth, n_pool, page * N_HA, VD_A)
    def pool_t(a):
        return jnp.transpose(a, (0, 1, 3, 4, 2)).reshape(depth, a.shape[1], LANE, a.shape[2])
    pool_ckt, pool_cvt, pool_skt, pool_svt = (pool_t(a) for a in (cache_cmp_k, cache_cmp_v, cache_sel_k,
                                                                 cache_sel_v))
    win_kt, win_vt = pool_t(state_win_k), pool_t(state_win_v)

    def tail_t(a):
        a = jnp.swapaxes(a.reshape(db, dec_seq, a.shape[-1]), 1, 2)
        return jnp.pad(a, ((0, 0), (0, 0), (0, page - dec_seq)))

    xp = x_prompt.reshape(bsz * t, d)
    xs = x_sample.reshape(db * dec_seq, d)
    stacked = None
    next_win = []
    rows_s = []
    for l in range(depth):
        lam_init = 0.8 - 0.6 * math.exp(-0.3 * l)

        outs = _in_proj_prompt(xp, l, prep, bsz, t, depth, stacked)
        dq_b, nq_b, dv_b, ck_n, cv_n, gs, gm, gn, ktb, kvtb = outs[:10]
        stacked = tuple(outs[10:])
        oa = _diff_prompt(dq_b, ktb, dv_b, gs, bias_p, l, prep, bsz, t, lam_init)
        ckc, cvc = _compress_prompt(ck_n, cv_n, l, prep, bsz, t)
        ob = _nsa_prompt(nq_b, ckc, cvc, kvtb, gn, gs, bias_p_nsa, l, bsz, t)
        xp = _out_proj(xp, oa, ob, gm, l, prep)

        dq, dk, dv, nq, kv6, gs, gm, gn = _in_proj_sample(xs, l, prep)
        ck, cv, sk, sv, wk, wv = (kv6[:, LANE * k:LANE * (k + 1)] for k in range(6))
        v_tail = jnp.pad(dv.reshape(db, dec_seq * N_HA, VD_A), ((0, 0), (0, (page - dec_seq) * N_HA), (0, 0)))
        oa = _diff_decode(page_table, pool_dkt, pool_dv, tail_t(dk), v_tail, dq, gs, bias_d_diff, l, prep,
                          lam_init)
        ocmp, sel = _nsa_select_decode(page_table, pool_ckt, pool_cvt, nq, l, prep, past, dec_seq)
        ob, *next_win = _nsa_decode(page_table, pool_skt, pool_svt, tail_t(sk), tail_t(sv), win_kt, win_vt,
                                    tail_t(wk), tail_t(wv), nq, sel, ocmp, gn, gs, bias_d_nsa, l, next_win)
        xs = _out_proj(xs, oa, ob, gm, l, prep)
        rows_s.append((dk.reshape(db, dec_seq, N_HA, 2, HD_A), dv.reshape(db, dec_seq, N_HA, VD_A),
                       ck.reshape(db, dec_seq, N_KV, HD_B), cv.reshape(db, dec_seq, N_KV, HD_B),
                       sk.reshape(db, dec_seq, N_KV, HD_B), sv.reshape(db, dec_seq, N_KV, HD_B)))

    dkt, dvr, ckt, cvt, skt, svt, wkt, wvt = stacked

    def untranspose(a):
        return jnp.transpose(a.reshape(depth, a.shape[1], N_KV, HD_B, a.shape[-1]), (0, 1, 4, 2, 3))

    outs_p = [jnp.transpose(dkt.reshape(depth, bsz, N_HA, 2, HD_A, t), (0, 1, 5, 2, 3, 4)),
              dvr.reshape(depth, bsz, t, N_HA, VD_A),
              untranspose(ckt), untranspose(cvt), untranspose(skt), untranspose(svt),
              untranspose(wkt), untranspose(wvt)]
    outs_s = [jnp.stack(r, axis=0) for r in zip(*rows_s)] + [untranspose(a) for a in next_win]
    return (xp.reshape(bsz, t, d), xs.reshape(db, dec_seq, d), *outs_p, *outs_s)
```

```python
import functools
import math

import numpy as np
import jax
import jax.numpy as jnp
from jax import lax
from jax.experimental import pallas as pl
from jax.experimental.pallas import tpu as pltpu

F32 = jnp.float32
BF16 = jnp.bfloat16

N_HA, HD_A, VD_A = 4, 64, 128
N_HB, N_KV, HD_B = 8, 2, 64
R_B = N_HB // N_KV
CMP_STRIDE, CMP_BLOCK = 16, 32
SEL_BLOCK, SEL_TOPN, WINDOW = 64, 16, 512
NUM_BUCKETS, MAX_DISTANCE = 32, 128
EPS, NEG, FORCE = 1e-6, -1e30, 1e4
MASKED_BUCKET = NUM_BUCKETS
SEL_SHIFT = SEL_BLOCK.bit_length() - 1
HD_SHIFT = HD_A.bit_length() - 1

LANE = 128
TQ = 256
FAR_CHUNK = 512
DIFF_HEADS_PER_STEP = 4
VMEM_LIMIT = 56 * 1024 * 1024
DIFF_PAGES_PER_STEP = 16
SEL_PAGES_PER_STEP = 64

C_DK, C_DV, C_NQ, C_KV, C_GA, C_GN = 512, 1024, 1536, 2048, 2816, 3840
N_GN = N_HB * 3
N_T = 512 + 768
KV_ROWS = 640


def _mm(a, b):
    return jnp.dot(a, b, preferred_element_type=F32)


def _mm_nt(a, b):
    return lax.dot_general(a, b, (((1,), (1,)), ((), ())), preferred_element_type=F32)


def _split(x):
    hi = x.astype(BF16)
    lo = (x - hi.astype(F32)).astype(BF16)
    return hi, lo


def _sigmoid(z):
    return 1.0 / (1.0 + jnp.exp(-z))


def _group_rms(z, bd, gain):
    hi, lo = _split(z * z)
    ss = _mm(hi, bd) + _mm(lo, bd)
    return z * lax.rsqrt(ss * (1.0 / HD_A) + EPS) * gain


def _group_rms_t(z, gain):
    r, n = z.shape
    z3 = z.reshape(r // HD_A, HD_A, n)
    ss = jnp.sum(z3 * z3, axis=1, keepdims=True)
    return (z3 * lax.rsqrt(ss * (1.0 / HD_A) + EPS)).reshape(r, n) * gain


def _params(*sem):
    return pltpu.CompilerParams(dimension_semantics=sem, vmem_limit_bytes=VMEM_LIMIT)


def _lane_fold(x, op):
    out = x[:, 0:LANE]
    for k in range(1, x.shape[1] // LANE):
        out = op(out, x[:, k * LANE:(k + 1) * LANE])
    return out


def _key_spans(lo_tile, i):
    spans = []
    far_end = max(i - 1, lo_tile) * TQ
    c0 = lo_tile * TQ
    while c0 < far_end:
        spans.append((c0, min(c0 + FAR_CHUNK, far_end)))
        c0 = spans[-1][1]
    spans += [(j * TQ, (j + 1) * TQ) for j in range(max(i - 1, lo_tile), i + 1)]
    return spans


def _two_pass_softmax(score, value, spans, s_sc):
    mp = None
    for c0, c1 in spans:
        s = score(c0, c1)
        s_sc[:, c0:c1] = s
        f = _lane_fold(s, jnp.maximum)
        mp = f if mp is None else jnp.maximum(mp, f)
    m = jnp.max(mp, axis=-1, keepdims=True)
    lp = None
    acc = None
    for c0, c1 in spans:
        p = jnp.exp(s_sc[:, c0:c1] - m)
        f = _lane_fold(p, jnp.add)
        lp = f if lp is None else lp + f
        pv = value(c0, c1, p.astype(BF16))
        acc = pv if acc is None else acc + pv
    return acc * (1.0 / jnp.sum(lp, axis=-1, keepdims=True))


def _flash_update(s, pv, m_ref, l_ref, acc_ref):
    m_old = m_ref[...]
    m_new = jnp.maximum(m_old, jnp.max(s, axis=-1, keepdims=True))
    alpha = jnp.exp(m_old - m_new)
    p = jnp.exp(s - m_new)
    l_ref[...] = alpha * l_ref[...] + jnp.sum(p, axis=-1, keepdims=True)
    acc_ref[...] = alpha * acc_ref[...] + pv(p.astype(BF16))
    m_ref[...] = m_new


def _flash_init(m_ref, l_ref, acc_ref):
    m_ref[...] = jnp.full(m_ref.shape, -3e38, F32)
    l_ref[...] = jnp.zeros(l_ref.shape, F32)
    acc_ref[...] = jnp.zeros(acc_ref.shape, F32)


def _bucket_np(rel):
    n = np.maximum(rel, 0)
    exact = NUM_BUCKETS // 2
    nf = np.maximum(n, 1).astype(np.float32)
    large = exact + (np.log(nf / np.float32(exact)) / np.float32(math.log(MAX_DISTANCE / exact))
                     * np.float32(NUM_BUCKETS - exact)).astype(np.int32)
    large = np.minimum(large, NUM_BUCKETS - 1)
    return np.where(n < exact, n, large).astype(np.int32)


def _prompt_bucket_tiles():
    r = np.arange(TQ)[:, None]
    c = np.arange(TQ)[None, :]
    d0 = np.where(r - c >= 0, _bucket_np(r - c), MASKED_BUCKET)
    d1 = _bucket_np(TQ + r - c)
    return np.stack([d0, d1]).astype(np.int32)


def _decode_bucket_tiles(dec_seq, page, wb):
    q = np.arange(dec_seq)[:, None]
    c = np.arange(page)[None, :]
    last = _bucket_np(page + q - c)
    tail = np.where((c < dec_seq) & (c <= q), _bucket_np(q - c), MASKED_BUCKET)
    i = np.arange(wb)[None, :]
    relw = wb + q - i
    win = np.where(relw < WINDOW, _bucket_np(relw), MASKED_BUCKET)
    return np.concatenate([last, tail, win], axis=1).astype(np.int32)


def _overlap_np(n_rows, n_cols, n_cmp, n_sel):
    n = np.arange(n_rows)[:, None]
    j = np.arange(n_cols)[None, :]
    ov = (n * CMP_STRIDE < j * SEL_BLOCK + SEL_BLOCK) & (n * CMP_STRIDE + CMP_BLOCK > j * SEL_BLOCK)
    ov &= (n < n_cmp) & (j < n_sel)
    return ov.astype(np.float32)


def _blockdiag_np(n, group):
    i = np.arange(n)
    return (i[:, None] // group == i[None, :] // group).astype(np.float32)


def _chunk_perm_np(page):
    rows = np.arange(page)
    tok = (rows % (page // CMP_STRIDE)) * CMP_STRIDE + rows // (page // CMP_STRIDE)
    return (tok[:, None] == np.arange(page)[None, :]).astype(np.float32)


def _bias_body(tbl_ref, idxp_ref, idxd_ref, p_ref, d_ref):
    h = pl.program_id(0)
    far = tbl_ref[h, NUM_BUCKETS - 1]

    def expand(idx):
        acc = jnp.zeros(idx.shape, F32)
        for b in range(NUM_BUCKETS - 1):
            acc = jnp.where(idx == b, tbl_ref[h, b] - far, acc)
        return jnp.where(idx == MASKED_BUCKET, NEG, acc)

    p_ref[0] = expand(idxp_ref[0])
    p_ref[1] = expand(idxp_ref[1])
    d_ref[...] = expand(idxd_ref[...])


def _bias_tiles(rel_bias, idx_p, idx_d):
    nh = rel_bias.shape[0]
    dq, wd = idx_d.shape
    return pl.pallas_call(
        _bias_body,
        grid=(nh,),
        in_specs=[pl.BlockSpec(memory_space=pltpu.SMEM),
                  pl.BlockSpec((2, TQ, TQ), lambda h: (0, 0, 0)),
                  pl.BlockSpec((dq, wd), lambda h: (0, 0))],
        out_specs=[pl.BlockSpec((None, 2, TQ, TQ), lambda h: (h, 0, 0, 0)),
                   pl.BlockSpec((None, dq, wd), lambda h: (h, 0, 0))],
        out_shape=[jax.ShapeDtypeStruct((nh, 2, TQ, TQ), F32),
                   jax.ShapeDtypeStruct((nh, dq, wd), F32)],
        compiler_params=_params("arbitrary"),
        name="bias_tiles",
    )(rel_bias, idx_p, idx_d)


def _normed_input(x_ref, g_ref):
    x = x_ref[...]
    ms = jnp.mean(x * x, axis=-1, keepdims=True)
    return (x * lax.rsqrt(ms + EPS) * g_ref[...]).astype(BF16)


def _in_proj_sample_body(x_ref, g_ref, w_ref, b_ref, bd_ref, gain_ref,
                         dq_o, dk_o, dv_o, nq_o, kv_o, gs_o, gm_o, gn_o, *, d_model):
    h = _normed_input(x_ref, g_ref)

    def seg(a, b):
        return _mm(h, w_ref[:, a:b]) + b_ref[:, a:b]

    bd = bd_ref[...]
    bd2 = bd[0:LANE, 0:LANE]
    dq_o[...] = _group_rms(seg(0, C_DK), bd, gain_ref[0:1, :])
    dk_o[...] = _group_rms(seg(C_DK, C_DV), bd, gain_ref[1:2, :])
    dv_o[...] = seg(C_DV, C_NQ)
    nq_o[...] = _group_rms(seg(C_NQ, C_KV), bd, gain_ref[2:3, :])
    z = seg(C_KV, C_GA)
    kv_o[:, 0:256] = z[:, 0:256]
    kv_o[:, 256:384] = _group_rms(z[:, 256:384], bd2, gain_ref[3:4, 0:128])
    kv_o[:, 384:512] = z[:, 384:512]
    kv_o[:, 512:640] = _group_rms(z[:, 512:640], bd2, gain_ref[3:4, 128:256])
    kv_o[:, 640:768] = z[:, 640:768]
    z = seg(C_GA, C_GN)
    gs_o[...] = z * _sigmoid(z)
    gm_o[...] = _sigmoid(seg(C_GN, C_GN + 2 * d_model))
    gn_o[...] = _sigmoid(seg(C_GN + 2 * d_model, C_GN + 2 * d_model + LANE))


def _in_proj_sample(x2d, l, prep):
    n, d = x2d.shape
    tm = min(256, n)
    c = prep["w_in"].shape[-1]

    def row(w):
        return pl.BlockSpec((tm, w), lambda i: (i, 0))

    def lay(*shape):
        return pl.BlockSpec((None,) + shape, lambda i: (l,) + (0,) * len(shape))

    widths = [512, 512, 512, 512, 768, 1024, 2 * d, LANE]
    return pl.pallas_call(
        functools.partial(_in_proj_sample_body, d_model=d),
        grid=(n // tm,),
        in_specs=[row(d), lay(1, d), lay(d, c), lay(1, c),
                  pl.BlockSpec((512, 512), lambda i: (0, 0)), lay(8, 512)],
        out_specs=[row(w) for w in widths],
        out_shape=[jax.ShapeDtypeStruct((n, w), F32) for w in widths],
        compiler_params=_params("arbitrary"),
        name="in_proj_sample",
    )(x2d, prep["norm_g"], prep["w_in"], prep["b_in"], prep["bd512"], prep["gains"])


N_STACKED = 8


def _in_proj_prompt_body(*refs, d_model, n_alias, n_tiles):
    x_ref, g_ref, w_ref, b_ref, wt_ref, bt_ref, bd_ref, gain_ref, gaint_ref = refs[0:9]
    (dq_o, nq_o, dvb_o, ckn_o, cvn_o, gs_o, gm_o, gn_o, ktb_o, kvtb_o,
     dkt_o, dv_o, ckt_o, cvt_o, skt_o, svt_o, wkt_o, wvt_o) = refs[9 + n_alias:]
    h = _normed_input(x_ref, g_ref)
    tm = h.shape[0]

    def seg(a, b):
        return _mm(h, w_ref[:, a:b]) + b_ref[:, a:b]

    def seg_t(a, b):
        return _mm_nt(wt_ref[a:b, :], h) + bt_ref[a:b, :]

    bd = bd_ref[...]
    dq_o[...] = _group_rms(seg(0, C_DK), bd, gain_ref[0:1, :]).astype(BF16)
    nq_o[...] = _group_rms(seg(C_NQ, C_KV), bd, gain_ref[2:3, :]).astype(BF16)
    y = seg(C_DV, C_NQ)
    dvb_o[...] = y.astype(BF16)
    for hh in range(N_HA):
        dv_o[pl.ds(hh, tm, stride=N_HA), :] = y[:, hh * VD_A:(hh + 1) * VD_A]
    z = seg(C_KV, C_KV + 256)
    ckn_o[...] = z[:, 0:128]
    cvn_o[...] = z[:, 128:256]
    z = seg(C_GA, C_GN)
    gs_o[...] = z * _sigmoid(z)
    gm_o[...] = _sigmoid(seg(C_GN, C_GN + 2 * d_model))
    gn_o[...] = _sigmoid(seg(C_GN + 2 * d_model, C_GN + 2 * d_model + LANE))

    yt = _group_rms_t(seg_t(0, 512), gaint_ref[0:512, :])
    dkt_o[...] = yt
    ktb_o[...] = yt.astype(BF16)
    zt = seg_t(512, N_T)
    ckt_o[...] = zt[0:128]
    cvt_o[...] = zt[128:256]
    sk = _group_rms_t(zt[256:384], gaint_ref[512:640, :])
    sv = zt[384:512]
    wk = _group_rms_t(zt[512:640], gaint_ref[640:768, :])
    wv = zt[640:768]
    skt_o[...] = sk
    svt_o[...] = sv
    wkt_o[...] = wk
    wvt_o[...] = wv
    tok = (pl.program_id(0) % n_tiles) * tm + lax.broadcasted_iota(jnp.int32, (HD_B, tm), 1)
    blk = lax.broadcasted_iota(jnp.int32, (HD_B, tm), 0)
    unsel_rows = jnp.where(blk == (tok >> SEL_SHIFT), NEG, 0.0).astype(BF16)
    skb = sk.astype(BF16)
    kvtb_o[0:HD_B, :] = skb[0:HD_B]
    kvtb_o[HD_B:LANE, :] = unsel_rows
    kvtb_o[LANE:LANE + HD_B, :] = skb[HD_B:LANE]
    kvtb_o[LANE + HD_B:2 * LANE, :] = unsel_rows
    kvtb_o[256:384, :] = sv.astype(BF16)
    kvtb_o[384:512, :] = wk.astype(BF16)
    kvtb_o[512:640, :] = wv.astype(BF16)


def _in_proj_prompt(x2d, l, prep, bsz, t, depth, stacked):
    n, d = x2d.shape
    tm = TQ
    nt = t // tm
    c = prep["w_in"].shape[-1]
    nw = min(WINDOW, t)
    nwt = nw // tm

    def row(w):
        return pl.BlockSpec((tm, w), lambda i: (i, 0))

    def lay(*shape):
        return pl.BlockSpec((None,) + shape, lambda i: (l,) + (0,) * len(shape))

    def tile_t(rows):
        return pl.BlockSpec((None, rows, tm), lambda i: (i // nt, 0, i % nt))

    def stack_t(rows):
        return pl.BlockSpec((None, None, rows, tm), lambda i: (l, i // nt, 0, i % nt))

    win_t = pl.BlockSpec((None, None, LANE, tm),
                         lambda i: (l, i // nt, 0, jnp.maximum(i % nt - (nt - nwt), 0)))
    out_specs = [row(512), row(512), row(512), row(LANE), row(LANE), row(1024), row(2 * d), row(LANE),
                 tile_t(512), tile_t(KV_ROWS),
                 stack_t(512), pl.BlockSpec((None, tm * N_HA, VD_A), lambda i: (l, i, 0)),
                 stack_t(LANE), stack_t(LANE), stack_t(LANE), stack_t(LANE), win_t, win_t]
    kv_t = jax.ShapeDtypeStruct((depth, bsz, LANE, t), F32)
    win_shape = jax.ShapeDtypeStruct((depth, bsz, LANE, nw), F32)
    out_shape = [jax.ShapeDtypeStruct((n, 512), BF16), jax.ShapeDtypeStruct((n, 512), BF16),
                 jax.ShapeDtypeStruct((n, 512), BF16), jax.ShapeDtypeStruct((n, LANE), F32),
                 jax.ShapeDtypeStruct((n, LANE), F32), jax.ShapeDtypeStruct((n, 1024), F32),
                 jax.ShapeDtypeStruct((n, 2 * d), F32), jax.ShapeDtypeStruct((n, LANE), F32),
                 jax.ShapeDtypeStruct((bsz, 512, t), BF16), jax.ShapeDtypeStruct((bsz, KV_ROWS, t), BF16),
                 jax.ShapeDtypeStruct((depth, bsz, 512, t), F32),
                 jax.ShapeDtypeStruct((depth, n * N_HA, VD_A), F32),
                 kv_t, kv_t, kv_t, kv_t, win_shape, win_shape]
    n_alias = 0 if stacked is None else N_STACKED
    n_in = 9
    n_plain = len(out_shape) - N_STACKED
    aliases = {n_in + k: n_plain + k for k in range(n_alias)}
    return pl.pallas_call(
        functools.partial(_in_proj_prompt_body, d_model=d, n_alias=n_alias, n_tiles=nt),
        grid=(n // tm,),
        in_specs=[row(d), lay(1, d), lay(d, c), lay(1, c), lay(N_T, d), lay(N_T, 1),
                  pl.BlockSpec((512, 512), lambda i: (0, 0)), lay(8, 512), lay(N_T, 1)]
                 + [pl.BlockSpec(memory_space=pl.ANY)] * n_alias,
        out_specs=out_specs,
        out_shape=out_shape,
        input_output_aliases=aliases,
        compiler_params=_params("arbitrary"),
        name="in_proj_prompt",
    )(x2d, prep["norm_g"], prep["w_in"], prep["b_in"], prep["w_t"], prep["b_t"], prep["bd512"],
      prep["gains"], prep["gains_t"], *(stacked or ()))


def _out_proj_body(x_ref, oa_ref, ob_ref, gm_ref, wa_ref, wb_ref, wo_ref, y_ref, *, d_model):
    ya = _mm(oa_ref[...].astype(BF16), wa_ref[...])
    yb = _mm(ob_ref[...].astype(BF16), wb_ref[...])
    gm = gm_ref[...]
    m = gm[:, 0:d_model] * ya + gm[:, d_model:2 * d_model] * yb
    y_ref[...] = x_ref[...] + _mm(m.astype(BF16), wo_ref[...])


def _out_proj(x2d, oa, ob, gm, l, prep):
    n, d = x2d.shape
    tm = min(512, n)

    def row(w):
        return pl.BlockSpec((tm, w), lambda i: (i, 0))

    def lay(*shape):
        return pl.BlockSpec((None,) + shape, lambda i: (l,) + (0,) * len(shape))

    return pl.pallas_call(
        functools.partial(_out_proj_body, d_model=d),
        grid=(n // tm,),
        in_specs=[row(d), row(512), row(512), row(2 * d), lay(512, d), lay(512, d), lay(d, d)],
        out_specs=row(d),
        out_shape=jax.ShapeDtypeStruct((n, d), F32),
        compiler_params=_params("arbitrary"),
        name="out_proj",
    )(x2d, oa, ob, gm, prep["w_a"], prep["w_b"], prep["w_out"])


def _lambda(lam_ref, lam_init):
    lv = lam_ref[...]
    a = jnp.sum(lv[0:1] * lv[1:2], axis=-1, keepdims=True)
    b = jnp.sum(lv[2:3] * lv[3:4], axis=-1, keepdims=True)
    return jnp.exp(a) - jnp.exp(b) + lam_init


def _diff_finish(o1, o2, lam, sub, lam_init):
    o = o1 - lam * o2
    ms = jnp.mean(o * o, axis=-1, keepdims=True)
    return o * lax.rsqrt(ms + EPS) * sub * (1.0 - lam_init)


def _diff_prompt_body(*refs, i, lam_init):
    q_ref, kt_ref, v_ref, gs_ref, bias_ref, lam_ref, sub_ref = refs[0:7]
    o_ref, s_sc = refs[-2:]
    lam = _lambda(lam_ref, lam_init)
    for hh in range(DIFF_HEADS_PER_STEP):
        cols = slice(hh * LANE, (hh + 1) * LANE)
        q = q_ref[:, cols]
        lane = lax.broadcasted_iota(jnp.int32, q.shape, 1)
        zero = jnp.zeros_like(q)
        qq = jnp.concatenate([jnp.where(lane < HD_A, q, zero), jnp.where(lane >= HD_A, q, zero)], axis=0)

        def score(c0, c1, hh=hh, cols=cols, qq=qq):
            s = _mm(qq, kt_ref[cols, c0:c1])
            if c0 >= (i - 1) * TQ:
                b = bias_ref[hh, 0 if c0 == i * TQ else 1]
                s = s + jnp.concatenate([b, b], axis=0)
            return s

        def value(c0, c1, p, cols=cols):
            return _mm(p, v_ref[c0:c1, cols])

        o = _two_pass_softmax(score, value, _key_spans(0, i), s_sc.at[hh])
        y = _diff_finish(o[0:TQ], o[TQ:2 * TQ], lam, sub_ref[...], lam_init)
        o_ref[:, cols] = (y * gs_ref[:, cols]).astype(BF16)


def _diff_prompt(dq_b, ktb, dv_b, gs, bias_p, l, prep, bsz, t, lam_init):
    nt = t // TQ
    hs = DIFF_HEADS_PER_STEP
    out = None
    for i in range(nt):
        w = (i + 1) * TQ
        alias = [] if out is None else [out]
        out = pl.pallas_call(
            functools.partial(_diff_prompt_body, i=i, lam_init=lam_init),
            grid=(bsz, N_HA // hs),
            in_specs=[pl.BlockSpec((TQ, hs * LANE), lambda b, h, i=i: (b * nt + i, h)),
                      pl.BlockSpec((None, hs * LANE, w), lambda b, h: (b, h, 0)),
                      pl.BlockSpec((t, hs * LANE), lambda b, h: (b, h)),
                      pl.BlockSpec((TQ, hs * LANE), lambda b, h, i=i: (b * nt + i, h)),
                      pl.BlockSpec((hs, 2, TQ, TQ), lambda b, h: (h, 0, 0, 0)),
                      pl.BlockSpec((None, 4, HD_A), lambda b, h: (l, 0, 0)),
                      pl.BlockSpec((None, 1, VD_A), lambda b, h: (l, 0, 0))]
                     + [pl.BlockSpec(memory_space=pl.ANY)] * len(alias),
            out_specs=pl.BlockSpec((TQ, hs * LANE), lambda b, h, i=i: (b * nt + i, h)),
            out_shape=jax.ShapeDtypeStruct((bsz * t, N_HA * VD_A), BF16),
            input_output_aliases={7: 0} if alias else {},
            scratch_shapes=[pltpu.VMEM((hs, 2 * TQ, w), F32)],
            compiler_params=_params("arbitrary", "arbitrary"),
            name=f"diff_prompt_q{i}",
        )(dq_b, ktb, dv_b, gs, bias_p, prep["lam"], prep["subln"], *alias)
    return out


def _deinterleave(a):
    lane = lax.broadcasted_iota(jnp.int32, (a.shape[0], LANE), 1)
    low = lane < HD_B
    g0, g1 = [], []
    for k in range(CMP_STRIDE // 2):
        xe = a[:, 2 * LANE * k:2 * LANE * k + LANE]
        xo = a[:, 2 * LANE * k + LANE:2 * LANE * (k + 1)]
        g0.append(jnp.where(low, xe, pltpu.roll(xo, HD_B, 1)))
        g1.append(jnp.where(low, pltpu.roll(xe, HD_B, 1), xo))
    return jnp.concatenate(g0, axis=1).astype(BF16), jnp.concatenate(g1, axis=1).astype(BF16)


def _compress_partial(a, w1_ref):
    a0, a1 = _deinterleave(a)
    half = CMP_STRIDE * HD_B
    w1a = w1_ref[0:half, :]
    w1b = w1_ref[half:2 * half, :]
    return _mm(a0, w1a), _mm(a1, w1a), _mm(a0, w1b), _mm(a1, w1b)


def _compress_finish(parts, w1_ref, pos_ref, b1_ref, w2_ref, b2_ref):
    ha0, ha1, hb0, hb1 = parts
    m = ha0.shape[0]
    pos = jnp.broadcast_to(pos_ref[...], (8, pos_ref.shape[-1])).astype(BF16)
    c = _mm(pos, w1_ref[...])[0:1] + b1_ref[...]
    h0 = ha0 + pltpu.roll(hb0, m - 1, 0) + c
    h1 = ha1 + pltpu.roll(hb1, m - 1, 0) + c
    hid = jnp.concatenate([h0 * _sigmoid(h0), h1 * _sigmoid(h1)], axis=1).astype(BF16)
    return _mm(hid, w2_ref[...]) + b2_ref[...]


def _compress_prompt_body(ak_ref, av_ref, w1_ref, pos_ref, b1_ref, w2_ref, b2_ref, kn_ref, bd_ref,
                          ck_o, cv_o):
    outs = []
    for kv, a_ref in enumerate((ak_ref, av_ref)):
        parts = _compress_partial(a_ref[...], w1_ref.at[kv])
        outs.append(_compress_finish(parts, w1_ref.at[kv], pos_ref.at[kv], b1_ref.at[kv],
                                     w2_ref.at[kv], b2_ref.at[kv]))
    ck_o[...] = _group_rms(outs[0], bd_ref[...], kn_ref[...])
    cv_o[...] = outs[1]


def _cmp_weight_specs(l):
    def lay(*shape):
        return pl.BlockSpec((None,) + shape, lambda *a: (l,) + (0,) * len(shape))
    flat = CMP_BLOCK * HD_B
    return [lay(2, flat, 256), lay(2, 1, flat), lay(2, 1, 256), lay(2, 512, LANE), lay(2, 1, LANE),
            lay(1, LANE), pl.BlockSpec((LANE, LANE), lambda *a: (0, 0))]


def _cmp_weights(prep):
    return (prep["w1"], prep["pos"], prep["b1"], prep["w2bd"], prep["b2"], prep["cmp_kn"], prep["bd128"])


def _compress_prompt(ck, cv, l, prep, bsz, t):
    nch = t // CMP_STRIDE
    wdt = CMP_STRIDE * LANE
    a_spec = pl.BlockSpec((nch, wdt), lambda b: (b, 0))
    o_spec = pl.BlockSpec((nch, LANE), lambda b: (b, 0))
    return pl.pallas_call(
        _compress_prompt_body,
        grid=(bsz,),
        in_specs=[a_spec, a_spec] + _cmp_weight_specs(l),
        out_specs=[o_spec, o_spec],
        out_shape=[jax.ShapeDtypeStruct((bsz * nch, LANE), F32)] * 2,
        compiler_params=_params("arbitrary"),
        name="compress_prompt",
    )(ck.reshape(bsz * nch, wdt), cv.reshape(bsz * nch, wdt), *_cmp_weights(prep))


def _nsa_q_rows(qf, g):
    m = qf.shape[0]
    zero = jnp.zeros((m, HD_B), F32)
    rows = []
    for r in range(R_B):
        c0 = (g * R_B + r) * HD_B
        piece = qf[:, c0:c0 + HD_B]
        rows.append(jnp.concatenate([piece, zero] if g == 0 else [zero, piece], axis=1))
    return jnp.concatenate(rows, axis=0).astype(BF16)


def _cmp_attention(qrows, ckc, cvc, valid, anyv):
    hi, lo = _split(ckc)
    s = _mm_nt(qrows, hi) + _mm_nt(qrows, lo)
    s = jnp.where(valid, s, NEG)
    m = jnp.max(s, axis=-1, keepdims=True)
    e = jnp.exp(s - m)
    ncp = e.shape[1]
    ones = jnp.ones((ncp, LANE), BF16)
    e_hi, e_lo = _split(e)
    den = _mm(e_hi, ones) + _mm(e_lo, ones)
    p = e / jnp.concatenate([den] * (ncp // LANE), axis=1) * anyv
    return p, _mm(p.astype(BF16), cvc.astype(BF16))


def _select_blocks(imp_sum, ov, own, n_sel, top):
    hi, lo = _split(imp_sum)
    imp = _mm(hi, ov) + _mm(lo, ov)
    j = lax.broadcasted_iota(jnp.int32, imp.shape, 1)
    imp = jnp.where(j > own, -1.0, jnp.where((j == own) | (j == 0), FORCE, imp))
    imp = jnp.where(j >= n_sel, -2.0, imp)
    jf = j.astype(F32)
    rank = jnp.zeros(imp.shape, F32)
    for k in range(n_sel):
        col = imp[:, k:k + 1]
        rank = rank + jnp.where(col > imp, 1.0, jnp.where(col == imp, jnp.where(jf > k, 1.0, 0.0), 0.0))
    return jnp.where(rank < top, 1.0, 0.0)


def _select_blocks_t(imp_sum, ov_t, own_row, n_sel, top):
    m = imp_sum.shape[0]
    nsp = ov_t.shape[0]
    nsr = -(-n_sel // 8) * 8
    hi, lo = _split(imp_sum)
    imp = (_mm_nt(ov_t, hi) + _mm_nt(ov_t, lo))[0:nsr]
    j = lax.broadcasted_iota(jnp.int32, imp.shape, 0)
    imp = jnp.where(j > own_row, -1.0, jnp.where((j == own_row) | (j == 0), FORCE, imp))
    imp = jnp.where(j >= n_sel, -2.0, imp)
    jf = j.astype(F32)
    rank = jnp.zeros(imp.shape, F32)
    for k in range(n_sel):
        row = imp[k:k + 1, :]
        rank = rank + jnp.where(row > imp, 1.0, jnp.where(row == imp, jnp.where(jf > k, 1.0, 0.0), 0.0))
    sel_t = jnp.where(rank < top, 1.0, 0.0)
    if nsp > nsr:
        sel_t = jnp.concatenate([sel_t, jnp.zeros((nsp - nsr, m), F32)], axis=0)
    r = lax.broadcasted_iota(jnp.int32, (m, m), 0)
    c = lax.broadcasted_iota(jnp.int32, (m, m), 1)
    eye = jnp.where(r == c, 1.0, 0.0).astype(BF16)
    return _mm_nt(eye, sel_t.astype(BF16)).astype(BF16)


def _block_mask(sel_bf16, first_block, n_blocks):
    nb = sel_bf16.shape[1]
    width = n_blocks * SEL_BLOCK
    j = lax.broadcasted_iota(jnp.int32, (nb, width), 0)
    c = lax.broadcasted_iota(jnp.int32, (nb, width), 1)
    e = jnp.where(j == first_block + (c >> SEL_SHIFT), 1.0, 0.0).astype(BF16)
    return jnp.where(_mm(sel_bf16, e) > 0.5, 0.0, NEG)


def _nsa_prompt_body(*refs, i, n_cmp, n_sel, top):
    q_ref, ckc_ref, cvc_ref, kv_ref, gn_ref, gs_ref, pb_ref, ov_ref = refs[0:8]
    o_ref, s_sc = refs[-2:]
    rows = R_B * TQ
    qf = q_ref[...].astype(F32)
    gn = gn_ref[...]
    ckc = ckc_ref[...]
    cvc = cvc_ref[...]
    ncp = ckc.shape[0]
    qp = i * TQ + lax.broadcasted_iota(jnp.int32, (TQ, 1), 0)
    qp4 = i * TQ + (lax.broadcasted_iota(jnp.int32, (rows, 1), 0) & (TQ - 1))
    nidx = lax.broadcasted_iota(jnp.int32, (rows, ncp), 1)
    valid = (nidx * CMP_STRIDE + (CMP_BLOCK - 1) <= qp4) & (nidx < n_cmp)
    anyv = jnp.where(qp4 >= CMP_BLOCK - 1, 1.0, 0.0) * float(n_cmp > 0)
    own_row = (i * TQ + lax.broadcasted_iota(jnp.int32, (1, TQ), 1)) >> SEL_SHIFT
    rr = lax.broadcasted_iota(jnp.int32, (rows, TQ), 0) & (TQ - 1)
    cc = lax.broadcasted_iota(jnp.int32, (rows, TQ), 1)
    anti = jnp.where(cc > rr, 0.0, NEG)
    wt = WINDOW // TQ
    bpt = TQ // SEL_BLOCK
    pieces = []
    for g in range(N_KV):
        qrows = _nsa_q_rows(qf, g)

        def bias(d, g=g):
            return jnp.concatenate([pb_ref[g * R_B + r, d] for r in range(R_B)], axis=0)

        p_c, o_cmp = _cmp_attention(qrows, ckc, cvc, valid, anyv)
        imp_sum = p_c[0:TQ]
        for r in range(1, R_B):
            imp_sum = imp_sum + p_c[r * TQ:(r + 1) * TQ]
        sel = _select_blocks_t(imp_sum, ov_ref[...], own_row, n_sel, top)

        def near_bias(s, c0, bias=bias):
            if c0 == i * TQ:
                return s + bias(0)
            if c0 == (i - 1) * TQ:
                return s + bias(1)
            return s

        unsel = (1.0 - sel.astype(F32))[:, 0:HD_B]
        qsel = jnp.concatenate(
            [jnp.concatenate([qf[:, (g * R_B + r) * HD_B:(g * R_B + r + 1) * HD_B], unsel], axis=1)
             for r in range(R_B)], axis=0).astype(BF16)

        def sel_score(c0, c1, g=g, qsel=qsel, near_bias=near_bias):
            return near_bias(_mm(qsel, kv_ref[g * LANE:(g + 1) * LANE, c0:c1]), c0)

        def sel_value(c0, c1, p):
            return _mm_nt(p, kv_ref[2 * LANE:3 * LANE, c0:c1])

        o_sel = _two_pass_softmax(sel_score, sel_value, _key_spans(0, i), s_sc)

        def win_score(c0, c1, qrows=qrows, near_bias=near_bias):
            s = _mm(qrows, kv_ref[3 * LANE:4 * LANE, c0:c1])
            if c0 == (i - wt) * TQ:
                s = s + anti
            return near_bias(s, c0)

        def win_value(c0, c1, p):
            return _mm_nt(p, kv_ref[4 * LANE:5 * LANE, c0:c1])

        win_spans = [(j * TQ, (j + 1) * TQ) for j in range(max(i - wt, 0), i + 1)]
        o_win = _two_pass_softmax(win_score, win_value, win_spans, s_sc)

        for r in range(R_B):
            hh = g * R_B + r
            rs = slice(r * TQ, (r + 1) * TQ)
            o = (gn[:, 3 * hh:3 * hh + 1] * o_cmp[rs] + gn[:, 3 * hh + 1:3 * hh + 2] * o_sel[rs]
                 + gn[:, 3 * hh + 2:3 * hh + 3] * o_win[rs])
            pieces.append(o[:, g * HD_B:(g + 1) * HD_B])
    o_ref[...] = (jnp.concatenate(pieces, axis=1) * gs_ref[...]).astype(BF16)


def _nsa_prompt(nq_b, ckc, cvc, kvtb, gn, gs, bias_p, l, bsz, t):
    nt = t // TQ
    nch = t // CMP_STRIDE
    n_cmp = nch - 1
    n_sel = -(-t // SEL_BLOCK)
    top = min(SEL_TOPN, n_sel)
    nsp = -(-n_sel // LANE) * LANE
    ov = jnp.asarray(_overlap_np(nch, nsp, n_cmp, n_sel).T, BF16)
    rows = R_B * TQ
    out = None
    for i in range(nt):
        w = (i + 1) * TQ
        alias = [] if out is None else [out]
        out = pl.pallas_call(
            functools.partial(_nsa_prompt_body, i=i, n_cmp=n_cmp, n_sel=n_sel, top=top),
            grid=(bsz,),
            in_specs=[pl.BlockSpec((TQ, 512), lambda b, i=i: (b * nt + i, 0)),
                      pl.BlockSpec((nch, LANE), lambda b: (b, 0)),
                      pl.BlockSpec((nch, LANE), lambda b: (b, 0)),
                      pl.BlockSpec((None, KV_ROWS, w), lambda b: (b, 0, 0)),
                      pl.BlockSpec((TQ, LANE), lambda b, i=i: (b * nt + i, 0)),
                      pl.BlockSpec((TQ, 512), lambda b, i=i: (b * nt + i, 1)),
                      pl.BlockSpec((N_HB, 2, TQ, TQ), lambda b: (0, 0, 0, 0)),
                      pl.BlockSpec((nsp, nch), lambda b: (0, 0))]
                     + [pl.BlockSpec(memory_space=pl.ANY)] * len(alias),
            out_specs=pl.BlockSpec((TQ, 512), lambda b, i=i: (b * nt + i, 0)),
            out_shape=jax.ShapeDtypeStruct((bsz * t, N_HB * HD_B), BF16),
            input_output_aliases={8: 0} if alias else {},
            scratch_shapes=[pltpu.VMEM((rows, w), F32)],
            compiler_params=_params("arbitrary"),
            name=f"nsa_prompt_q{i}",
        )(nq_b, ckc, cvc, kvtb, gn, gs, bias_p, ov, *alias)
    return out


def _diff_decode_body(pt_ref, *refs, n_pg, dec_seq, lam_init):
    kt_pages = refs[0:n_pg]
    v_pages = refs[n_pg:2 * n_pg]
    ktt_ref, vt_ref, q_ref, gs_ref, bias_ref, lam_ref, sub_ref, o_ref, m_sc, l_sc, acc_sc = refs[2 * n_pg:]
    c = pl.program_id(1)
    last = pl.num_programs(1) - 1
    page = kt_pages[0].shape[1]
    hrows = 2 * dec_seq
    q = q_ref[...]
    blk = lax.broadcasted_iota(jnp.int32, q.shape, 1) >> HD_SHIFT
    qrows = jnp.concatenate([jnp.where(blk == hm, q, 0.0) for hm in range(2 * N_HA)], axis=0).astype(BF16)

    def values(refs_):
        def pv(p):
            outs = []
            for h in range(N_HA):
                vh = jnp.concatenate([r[pl.ds(h, page, stride=N_HA), :].astype(BF16) for r in refs_], axis=0)
                outs.append(_mm(p[h * hrows:(h + 1) * hrows], vh))
            return jnp.concatenate(outs, axis=0)
        return pv

    @pl.when(c == 0)
    def _():
        _flash_init(m_sc, l_sc, acc_sc)

    parts = [_mm(qrows, r[...].astype(BF16)) for r in kt_pages]
    parts[-1] = parts[-1] + jnp.where(c == last, bias_ref[:, 0:page], 0.0)
    _flash_update(jnp.concatenate(parts, axis=1), values(v_pages), m_sc, l_sc, acc_sc)

    @pl.when(c == last)
    def _():
        s = _mm(qrows, ktt_ref[...].astype(BF16)) + bias_ref[:, page:2 * page]
        _flash_update(s, values([vt_ref]), m_sc, l_sc, acc_sc)
        o = acc_sc[...] / l_sc[...]
        lam = _lambda(lam_ref, lam_init)
        outs = []
        for h in range(N_HA):
            r0 = h * hrows
            outs.append(_diff_finish(o[r0:r0 + dec_seq], o[r0 + dec_seq:r0 + hrows], lam, sub_ref[...],
                                     lam_init))
        o_ref[...] = jnp.concatenate(outs, axis=1) * gs_ref[...]


def _page_specs(n_pg, l, shape):
    return [pl.BlockSpec((None, None) + shape, (lambda b, c, pt, k=k: (l, pt[b, c * n_pg + k], 0, 0)))
            for k in range(n_pg)]


def _diff_decode(pt, pool_kt, pool_v, kt_tail, v_tail, dq, gs, bias_d, l, prep, lam_init):
    db, n_pages = pt.shape
    page = pool_kt.shape[3]
    dec_seq = dq.shape[0] // db
    n_pg = min(DIFF_PAGES_PER_STEP, n_pages)
    rows = 2 * N_HA * dec_seq
    wd = bias_d.shape[1]
    grid_spec = pltpu.PrefetchScalarGridSpec(
        num_scalar_prefetch=1,
        grid=(db, n_pages // n_pg),
        in_specs=_page_specs(n_pg, l, (512, page)) + _page_specs(n_pg, l, (page * N_HA, VD_A)) + [
            pl.BlockSpec((None, 512, page), lambda b, c, pt: (b, 0, 0)),
            pl.BlockSpec((None, page * N_HA, VD_A), lambda b, c, pt: (b, 0, 0)),
            pl.BlockSpec((dec_seq, 512), lambda b, c, pt: (b, 0)),
            pl.BlockSpec((dec_seq, 512), lambda b, c, pt: (b, 0)),
            pl.BlockSpec((rows, wd), lambda b, c, pt: (0, 0)),
            pl.BlockSpec((None, 4, HD_A), lambda b, c, pt: (l, 0, 0)),
            pl.BlockSpec((None, 1, VD_A), lambda b, c, pt: (l, 0, 0))],
        out_specs=pl.BlockSpec((dec_seq, 512), lambda b, c, pt: (b, 0)),
        scratch_shapes=[pltpu.VMEM((rows, 1), F32), pltpu.VMEM((rows, 1), F32),
                        pltpu.VMEM((rows, VD_A), F32)])
    return pl.pallas_call(
        functools.partial(_diff_decode_body, n_pg=n_pg, dec_seq=dec_seq, lam_init=lam_init),
        grid_spec=grid_spec,
        out_shape=jax.ShapeDtypeStruct((db * dec_seq, N_HA * VD_A), F32),
        compiler_params=_params("arbitrary", "arbitrary"),
        name="diff_decode",
    )(pt, *([pool_kt] * n_pg), *([pool_v] * n_pg), kt_tail, v_tail, dq, gs, bias_d,
      prep["lam"], prep["subln"])


def _chunk_rows(pages, perm):
    cpp = perm.shape[0] // CMP_STRIDE
    rows = []
    for k in range(0, len(pages), 2):
        x2 = jnp.concatenate([pages[k][...], pages[k + 1][...]], axis=0).astype(BF16)
        t2 = _mm_nt(perm, x2)
        for half in range(2):
            t = t2[:, half * LANE:(half + 1) * LANE]
            rows.append(jnp.concatenate([t[cpp * r:cpp * (r + 1)] for r in range(CMP_STRIDE)], axis=1))
    return jnp.concatenate(rows, axis=0)


def _nsa_select_decode_body(pt_ref, *refs, n_pg, n_cmp, n_sel, top, pos0, dec_seq):
    page_sets = (refs[0:n_pg], refs[n_pg:2 * n_pg])
    (perm_ref, w1_ref, pos_ref, b1_ref, w2_ref, b2_ref, kn_ref, bd_ref, q_ref, ov_ref,
     ocmp_o, sel_o) = refs[2 * n_pg:]
    outs = []
    for kv, pages in enumerate(page_sets):
        parts = _compress_partial(_chunk_rows(pages, perm_ref[...]), w1_ref.at[kv])
        outs.append(_compress_finish(parts, w1_ref.at[kv], pos_ref.at[kv], b1_ref.at[kv],
                                     w2_ref.at[kv], b2_ref.at[kv]))
    ckc = _group_rms(outs[0], bd_ref[...], kn_ref[...])
    cvc = outs[1]
    ncp = ckc.shape[0]
    qf = q_ref[...]
    rows = R_B * dec_seq
    qp = pos0 + lax.broadcasted_iota(jnp.int32, (dec_seq, 1), 0)
    qp4 = pos0 + (lax.broadcasted_iota(jnp.int32, (rows, 1), 0) & (dec_seq - 1))
    nidx = lax.broadcasted_iota(jnp.int32, (rows, ncp), 1)
    valid = (nidx * CMP_STRIDE + (CMP_BLOCK - 1) <= qp4) & (nidx < n_cmp)
    own = qp >> SEL_SHIFT
    anyv = jnp.where(qp4 >= CMP_BLOCK - 1, 1.0, 0.0) * float(n_cmp > 0)
    for g in range(N_KV):
        qrows = _nsa_q_rows(qf, g)
        p_c, o_cmp = _cmp_attention(qrows, ckc, cvc, valid, anyv)
        imp_sum = p_c[0:dec_seq]
        for r in range(1, R_B):
            imp_sum = imp_sum + p_c[r * dec_seq:(r + 1) * dec_seq]
        sel = _select_blocks(imp_sum, ov_ref[...], own, n_sel, top)
        ocmp_o[g * rows:(g + 1) * rows, :] = o_cmp
        sel_o[g * rows:(g + 1) * rows, :] = jnp.concatenate([sel] * R_B, axis=0)


def _nsa_select_decode(pt, pool_kt, pool_vt, nq, l, prep, pos0, dec_seq):
    db, n_pages = pt.shape
    page = pool_kt.shape[3]
    nch = n_pages * (page // CMP_STRIDE)
    n_cmp = nch - 1
    n_sel = -(-(pos0 + dec_seq) // SEL_BLOCK)
    top = min(SEL_TOPN, n_sel)
    nsp = -(-n_sel // LANE) * LANE
    ov = jnp.asarray(_overlap_np(nch, nsp, n_cmp, n_sel), BF16)
    rows = N_HB * dec_seq
    pages = [pl.BlockSpec((None, None, LANE, page), (lambda b, pt, k=k: (l, pt[b, k], 0, 0)))
             for k in range(n_pages)]
    grid_spec = pltpu.PrefetchScalarGridSpec(
        num_scalar_prefetch=1,
        grid=(db,),
        in_specs=pages + pages + [pl.BlockSpec((page, page), lambda b, pt: (0, 0))] + _cmp_weight_specs(l) + [
            pl.BlockSpec((dec_seq, 512), lambda b, pt: (b, 0)),
            pl.BlockSpec((nch, nsp), lambda b, pt: (0, 0))],
        out_specs=[pl.BlockSpec((rows, LANE), lambda b, pt: (b, 0)),
                   pl.BlockSpec((rows, nsp), lambda b, pt: (b, 0))])
    return pl.pallas_call(
        functools.partial(_nsa_select_decode_body, n_pg=n_pages, n_cmp=n_cmp, n_sel=n_sel, top=top,
                          pos0=pos0, dec_seq=dec_seq),
        grid_spec=grid_spec,
        out_shape=[jax.ShapeDtypeStruct((db * rows, LANE), F32),
                   jax.ShapeDtypeStruct((db * rows, nsp), F32)],
        compiler_params=_params("arbitrary"),
        name="nsa_select_decode",
    )(pt, *([pool_kt] * n_pages), *([pool_vt] * n_pages), prep["perm"], *_cmp_weights(prep), nq, ov)


def _nsa_decode_body(pt_ref, *refs, n_pg, n_pages, dec_seq):
    kt_pages = refs[0:n_pg]
    vt_pages = refs[n_pg:2 * n_pg]
    (ktt_ref, vtt_ref, wks_ref, wvs_ref, wkt_ref, wvt_ref, q_ref, sel_ref, ocmp_ref, gn_ref, gs_ref,
     bias_ref) = refs[2 * n_pg:2 * n_pg + 12]
    o_ref, swk_o, swv_o, m_sc, l_sc, acc_sc = refs[-6:]
    c = pl.program_id(1)
    last = pl.num_programs(1) - 1
    page = kt_pages[0].shape[1]
    bpp = page // SEL_BLOCK
    qf = q_ref[...]
    qrows = jnp.concatenate([_nsa_q_rows(qf, g) for g in range(N_KV)], axis=0)
    sel = sel_ref[...].astype(BF16)

    def values(refs_):
        def pv(p):
            out = _mm_nt(p[:, 0:page], refs_[0][...].astype(BF16))
            for k in range(1, len(refs_)):
                out = out + _mm_nt(p[:, k * page:(k + 1) * page], refs_[k][...].astype(BF16))
            return out
        return pv

    @pl.when(c == 0)
    def _():
        _flash_init(m_sc, l_sc, acc_sc)

    parts = [_mm(qrows, r[...].astype(BF16)) for r in kt_pages]
    parts[-1] = parts[-1] + jnp.where(c == last, bias_ref[:, 0:page], 0.0)
    s = jnp.concatenate(parts, axis=1) + _block_mask(sel, c * (n_pg * bpp), n_pg * bpp)
    _flash_update(s, values(vt_pages), m_sc, l_sc, acc_sc)

    @pl.when(c == last)
    def _():
        tail_bias = bias_ref[:, page:2 * page]
        s = _mm(qrows, ktt_ref[...].astype(BF16)) + _block_mask(sel, n_pages * bpp, bpp) + tail_bias
        _flash_update(s, values([vtt_ref]), m_sc, l_sc, acc_sc)
        o_sel = acc_sc[...] / l_sc[...]
        s_w = _mm(qrows, wks_ref[...].astype(BF16)) + bias_ref[:, 2 * page:]
        s_t = _mm(qrows, wkt_ref[...].astype(BF16)) + tail_bias
        m = jnp.maximum(jnp.max(s_w, axis=-1, keepdims=True), jnp.max(s_t, axis=-1, keepdims=True))
        p_w = jnp.exp(s_w - m)
        p_t = jnp.exp(s_t - m)
        den = jnp.sum(p_w, axis=-1, keepdims=True) + jnp.sum(p_t, axis=-1, keepdims=True)
        o_win = (_mm_nt(p_w.astype(BF16), wvs_ref[...].astype(BF16))
                 + _mm_nt(p_t.astype(BF16), wvt_ref[...].astype(BF16))) / den
        o_cmp = ocmp_ref[...]
        gn = gn_ref[...]
        pieces = []
        for hh in range(N_HB):
            g = hh // R_B
            rs = slice(hh * dec_seq, (hh + 1) * dec_seq)
            o = (gn[:, 3 * hh:3 * hh + 1] * o_cmp[rs] + gn[:, 3 * hh + 1:3 * hh + 2] * o_sel[rs]
                 + gn[:, 3 * hh + 2:3 * hh + 3] * o_win[rs])
            pieces.append(o[:, g * HD_B:(g + 1) * HD_B])
        o_ref[...] = jnp.concatenate(pieces, axis=1) * gs_ref[...]
        wb = wks_ref.shape[1]
        for state_ref, tail_ref, out in ((wks_ref, wkt_ref, swk_o), (wvs_ref, wvt_ref, swv_o)):
            out[...] = jnp.concatenate([state_ref[...], tail_ref[...]], axis=1)[:, dec_seq:dec_seq + wb]


def _nsa_decode(pt, pool_kt, pool_vt, kt_tail, vt_tail, win_kt, win_vt, wkt_tail, wvt_tail, nq, sel, ocmp,
                gn, gs, bias_d, l, next_win):
    db, n_pages = pt.shape
    page = pool_kt.shape[3]
    dec_seq = nq.shape[0] // db
    n_pg = min(SEL_PAGES_PER_STEP, n_pages)
    rows = N_HB * dec_seq
    wd = bias_d.shape[1]
    wb = win_kt.shape[3]
    nsp = sel.shape[1]
    per_b = pl.BlockSpec((None, LANE, page), lambda b, c, pt: (b, 0, 0))
    win = pl.BlockSpec((None, None, LANE, wb), lambda b, c, pt: (l, b, 0, 0))
    grid_spec = pltpu.PrefetchScalarGridSpec(
        num_scalar_prefetch=1,
        grid=(db, n_pages // n_pg),
        in_specs=_page_specs(n_pg, l, (LANE, page)) + _page_specs(n_pg, l, (LANE, page)) + [
            per_b, per_b, win, win, per_b, per_b,
            pl.BlockSpec((dec_seq, 512), lambda b, c, pt: (b, 0)),
            pl.BlockSpec((rows, nsp), lambda b, c, pt: (b, 0)),
            pl.BlockSpec((rows, LANE), lambda b, c, pt: (b, 0)),
            pl.BlockSpec((dec_seq, LANE), lambda b, c, pt: (b, 0)),
            pl.BlockSpec((dec_seq, 512), lambda b, c, pt: (b, 1)),
            pl.BlockSpec((rows, wd), lambda b, c, pt: (0, 0))]
                 + [pl.BlockSpec(memory_space=pl.ANY)] * len(next_win),
        out_specs=[pl.BlockSpec((dec_seq, 512), lambda b, c, pt: (b, 0)), win, win],
        scratch_shapes=[pltpu.VMEM((rows, 1), F32), pltpu.VMEM((rows, 1), F32),
                        pltpu.VMEM((rows, LANE), F32)])
    n_in = 1 + 2 * n_pg + 12
    return pl.pallas_call(
        functools.partial(_nsa_decode_body, n_pg=n_pg, n_pages=n_pages, dec_seq=dec_seq),
        grid_spec=grid_spec,
        out_shape=[jax.ShapeDtypeStruct((db * dec_seq, N_HB * HD_B), F32),
                   jax.ShapeDtypeStruct(win_kt.shape, F32), jax.ShapeDtypeStruct(win_vt.shape, F32)],
        input_output_aliases={n_in + k: 1 + k for k in range(len(next_win))},
        compiler_params=_params("arbitrary", "arbitrary"),
        name="nsa_decode",
    )(pt, *([pool_kt] * n_pg), *([pool_vt] * n_pg), kt_tail, vt_tail, win_kt, win_vt, wkt_tail, wvt_tail,
      nq, sel, ocmp, gn, gs, bias_d, *next_win)


def _prepare(norm_g, w_in, b_in, diff_q_norm, diff_k_norm, diff_lambda, diff_subln, nsa_q_norm,
             nsa_k_norm, cmp_pos, cmp_w1, cmp_b1, cmp_w2, cmp_b2, w_branch_a, w_branch_b, w_out, page):
    depth, d, _ = w_in.shape
    c_gm = C_GN + N_GN
    pad = LANE - N_GN

    def reorder(a):
        z = jnp.zeros(a.shape[:-1] + (pad,), a.dtype)
        return jnp.concatenate([a[..., :C_GN], a[..., c_gm:c_gm + 2 * d], a[..., C_GN:c_gm], z], axis=-1)

    def t_rows(a):
        return jnp.concatenate([a[..., C_DK:C_DV], a[..., C_KV:C_GA]], axis=-1)

    scale_a = HD_A ** -0.5
    scale_b = HD_B ** -0.5
    gains = jnp.zeros((depth, 8, 512), F32)
    gains = gains.at[:, 0].set(jnp.tile(diff_q_norm, (1, 8)) * scale_a)
    gains = gains.at[:, 1].set(jnp.tile(diff_k_norm, (1, 8)))
    gains = gains.at[:, 2].set(jnp.tile(nsa_q_norm, (1, 8)) * scale_b)
    gains = gains.at[:, 3, 0:128].set(jnp.tile(nsa_k_norm[:, 1], (1, 2)))
    gains = gains.at[:, 3, 128:256].set(jnp.tile(nsa_k_norm[:, 2], (1, 2)))
    gains_t = jnp.concatenate([jnp.tile(diff_k_norm, (1, 8)), jnp.tile(nsa_k_norm[:, 1], (1, 2)),
                               jnp.tile(nsa_k_norm[:, 2], (1, 2)), jnp.zeros((depth, N_T - 768), F32)], axis=-1)
    zero = jnp.zeros_like(cmp_w2)
    w2bd = jnp.concatenate([jnp.concatenate([cmp_w2, zero], axis=-1),
                            jnp.concatenate([zero, cmp_w2], axis=-1)], axis=-2)
    return {
        "norm_g": norm_g.reshape(depth, 1, d),
        "w_in": reorder(w_in).astype(BF16),
        "b_in": reorder(b_in).reshape(depth, 1, -1),
        "w_t": jnp.swapaxes(t_rows(w_in), 1, 2).astype(BF16),
        "b_t": t_rows(b_in).reshape(depth, N_T, 1),
        "gains": gains,
        "gains_t": gains_t.reshape(depth, N_T, 1),
        "bd512": jnp.asarray(_blockdiag_np(512, HD_A), BF16),
        "bd128": jnp.asarray(_blockdiag_np(LANE, HD_B), BF16),
        "perm": jnp.asarray(_chunk_perm_np(page), BF16),
        "lam": diff_lambda,
        "subln": diff_subln.reshape(depth, 1, VD_A),
        "w1": cmp_w1.astype(BF16),
        "pos": cmp_pos.reshape(depth, 2, 1, CMP_BLOCK * HD_B),
        "b1": cmp_b1.reshape(depth, 2, 1, -1),
        "w2bd": w2bd.astype(BF16),
        "b2": jnp.tile(cmp_b2, (1, 1, 2)).reshape(depth, 2, 1, LANE),
        "cmp_kn": jnp.tile(nsa_k_norm[:, 0], (1, 2)).reshape(depth, 1, LANE),
        "w_a": w_branch_a.astype(BF16),
        "w_b": w_branch_b.astype(BF16),
        "w_out": w_out.astype(BF16),
    }


def kernel(x_prompt, x_sample, cache_diff_k, cache_diff_v, cache_cmp_k, cache_cmp_v, cache_sel_k, cache_sel_v,
           state_win_k, state_win_v, page_table, rel_bias, norm_g, w_in, b_in, diff_q_norm, diff_k_norm,
           diff_lambda, diff_subln, nsa_q_norm, nsa_k_norm, cmp_pos, cmp_w1, cmp_b1, cmp_w2, cmp_b2,
           w_branch_a, w_branch_b, w_out):
    bsz, t, d = x_prompt.shape
    db, dec_seq, _ = x_sample.shape
    depth, n_pool, page = cache_diff_k.shape[:3]
    n_pages = page_table.shape[1]
    past = n_pages * page
    wb = state_win_k.shape[2]
    assert t % TQ == 0 and WINDOW % TQ == 0 and page == LANE and wb == WINDOW and past % SEL_BLOCK == 0
    assert dec_seq <= CMP_STRIDE and dec_seq % 8 == 0 and min(WINDOW, t) % TQ == 0
    assert -(-t // SEL_BLOCK) <= HD_B

    prep = _prepare(norm_g, w_in, b_in, diff_q_norm, diff_k_norm, diff_lambda, diff_subln, nsa_q_norm,
                    nsa_k_norm, cmp_pos, cmp_w1, cmp_b1, cmp_w2, cmp_b2, w_branch_a, w_branch_b, w_out, page)
    bias_p, bias_d = _bias_tiles(rel_bias, jnp.asarray(_prompt_bucket_tiles()),
                                 jnp.asarray(_decode_bucket_tiles(dec_seq, page, wb)))
    wd = bias_d.shape[-1]
    bias_d_diff = jnp.repeat(bias_d[:N_HA], 2, axis=0).reshape(2 * N_HA * dec_seq, wd)
    bias_d_nsa = bias_d[N_HA:].reshape(N_HB * dec_seq, wd)
    bias_p_nsa = bias_p[N_HA:]

    pool_dkt = jnp.transpose(cache_diff_k, (0, 1, 3, 4, 5, 2)).reshape(depth, n_pool, 512, page)
    pool_dv = cache_diff_v.reshape(depth, n_pool, page * N_HA, VD_A)
    def pool_t(a):
        return jnp.transpose(a, (0, 1, 3, 4, 2)).reshape(depth, a.shape[1], LANE, a.shape[2])
    pool_ckt, pool_cvt, pool_skt, pool_svt = (pool_t(a) for a in (cache_cmp_k, cache_cmp_v, cache_sel_k,
                                                                 cache_sel_v))
    win_kt, win_vt = pool_t(state_win_k), pool_t(state_win_v)

    def tail_t(a):
        a = jnp.swapaxes(a.reshape(db, dec_seq, a.shape[-1]), 1, 2)
        return jnp.pad(a, ((0, 0), (0, 0), (0, page - dec_seq)))

    xp = x_prompt.reshape(bsz * t, d)
    xs = x_sample.reshape(db * dec_seq, d)
    stacked = None
    next_win = []
    rows_s = []
    for l in range(depth):
        lam_init = 0.8 - 0.6 * math.exp(-0.3 * l)

        outs = _in_proj_prompt(xp, l, prep, bsz, t, depth, stacked)
        dq_b, nq_b, dv_b, ck_n, cv_n, gs, gm, gn, ktb, kvtb = outs[:10]
        stacked = tuple(outs[10:])
        oa = _diff_prompt(dq_b, ktb, dv_b, gs, bias_p, l, prep, bsz, t, lam_init)
        ckc, cvc = _compress_prompt(ck_n, cv_n, l, prep, bsz, t)
        ob = _nsa_prompt(nq_b, ckc, cvc, kvtb, gn, gs, bias_p_nsa, l, bsz, t)
        xp = _out_proj(xp, oa, ob, gm, l, prep)

        dq, dk, dv, nq, kv6, gs, gm, gn = _in_proj_sample(xs, l, prep)
        ck, cv, sk, sv, wk, wv = (kv6[:, LANE * k:LANE * (k + 1)] for k in range(6))
        v_tail = jnp.pad(dv.reshape(db, dec_seq * N_HA, VD_A), ((0, 0), (0, (page - dec_seq) * N_HA), (0, 0)))
        oa = _diff_decode(page_table, pool_dkt, pool_dv, tail_t(dk), v_tail, dq, gs, bias_d_diff, l, prep,
                          lam_init)
        ocmp, sel = _nsa_select_decode(page_table, pool_ckt, pool_cvt, nq, l, prep, past, dec_seq)
        ob, *next_win = _nsa_decode(page_table, pool_skt, pool_svt, tail_t(sk), tail_t(sv), win_kt, win_vt,
                                    tail_t(wk), tail_t(wv), nq, sel, ocmp, gn, gs, bias_d_nsa, l, next_win)
        xs = _out_proj(xs, oa, ob, gm, l, prep)
        rows_s.append((dk.reshape(db, dec_seq, N_HA, 2, HD_A), dv.reshape(db, dec_seq, N_HA, VD_A),
                       ck.reshape(db, dec_seq, N_KV, HD_B), cv.reshape(db, dec_seq, N_KV, HD_B),
                       sk.reshape(db, dec_seq, N_KV, HD_B), sv.reshape(db, dec_seq, N_KV, HD_B)))

    dkt, dvr, ckt, cvt, skt, svt, wkt, wvt = stacked

    def untranspose(a):
        return jnp.transpose(a.reshape(depth, a.shape[1], N_KV, HD_B, a.shape[-1]), (0, 1, 4, 2, 3))

    outs_p = [jnp.transpose(dkt.reshape(depth, bsz, N_HA, 2, HD_A, t), (0, 1, 5, 2, 3, 4)),
              dvr.reshape(depth, bsz, t, N_HA, VD_A),
              untranspose(ckt), untranspose(cvt), untranspose(skt), untranspose(svt),
              untranspose(wkt), untranspose(wvt)]
    outs_s = [jnp.stack(r, axis=0) for r in zip(*rows_s)] + [untranspose(a) for a in next_win]
    return (xp.reshape(bsz, t, d), xs.reshape(db, dec_seq, d), *outs_p, *outs_s)
```

```python
import functools
import math

import numpy as np
import jax
import jax.numpy as jnp
from jax import lax
from jax.experimental import pallas as pl
from jax.experimental.pallas import tpu as pltpu

F32 = jnp.float32
BF16 = jnp.bfloat16

N_HA, HD_A, VD_A = 4, 64, 128
N_HB, N_KV, HD_B = 8, 2, 64
R_B = N_HB // N_KV
CMP_STRIDE, CMP_BLOCK = 16, 32
SEL_BLOCK, SEL_TOPN, WINDOW = 64, 16, 512
NUM_BUCKETS, MAX_DISTANCE = 32, 128
EPS, NEG, FORCE = 1e-6, -1e30, 1e4
MASKED_BUCKET = NUM_BUCKETS
SEL_SHIFT = SEL_BLOCK.bit_length() - 1
HD_SHIFT = HD_A.bit_length() - 1

LANE = 128
TQ = 256
FAR_CHUNK = 512
DIFF_HEADS_PER_STEP = 4
VMEM_LIMIT = 56 * 1024 * 1024
DIFF_PAGES_PER_STEP = 32
SEL_PAGES_PER_STEP = 64

C_DK, C_DV, C_NQ, C_KV, C_GA, C_GN = 512, 1024, 1536, 2048, 2816, 3840
N_GN = N_HB * 3
N_T = 512 + 768
KV_ROWS = 640


def _mm(a, b):
    return jnp.dot(a, b, preferred_element_type=F32)


def _mm_nt(a, b):
    return lax.dot_general(a, b, (((1,), (1,)), ((), ())), preferred_element_type=F32)


def _split(x):
    hi = x.astype(BF16)
    lo = (x - hi.astype(F32)).astype(BF16)
    return hi, lo


def _sigmoid(z):
    return 1.0 / (1.0 + jnp.exp(-z))


def _group_rms(z, bd, gain):
    hi, lo = _split(z * z)
    ss = _mm(hi, bd) + _mm(lo, bd)
    return z * lax.rsqrt(ss * (1.0 / HD_A) + EPS) * gain


def _group_rms_t(z, gain):
    r, n = z.shape
    z3 = z.reshape(r // HD_A, HD_A, n)
    ss = jnp.sum(z3 * z3, axis=1, keepdims=True)
    return (z3 * lax.rsqrt(ss * (1.0 / HD_A) + EPS)).reshape(r, n) * gain


def _params(*sem):
    return pltpu.CompilerParams(dimension_semantics=sem, vmem_limit_bytes=VMEM_LIMIT)


def _lane_fold(x, op):
    out = x[:, 0:LANE]
    for k in range(1, x.shape[1] // LANE):
        out = op(out, x[:, k * LANE:(k + 1) * LANE])
    return out


def _key_spans(lo_tile, i):
    spans = []
    far_end = max(i - 1, lo_tile) * TQ
    c0 = lo_tile * TQ
    while c0 < far_end:
        spans.append((c0, min(c0 + FAR_CHUNK, far_end)))
        c0 = spans[-1][1]
    spans += [(j * TQ, (j + 1) * TQ) for j in range(max(i - 1, lo_tile), i + 1)]
    return spans


def _two_pass_softmax(score, value, spans, s_sc):
    mp = None
    for c0, c1 in spans:
        s = score(c0, c1)
        s_sc[:, c0:c1] = s
        f = _lane_fold(s, jnp.maximum)
        mp = f if mp is None else jnp.maximum(mp, f)
    m = jnp.max(mp, axis=-1, keepdims=True)
    lp = None
    acc = None
    for c0, c1 in spans:
        p = jnp.exp(s_sc[:, c0:c1] - m)
        f = _lane_fold(p, jnp.add)
        lp = f if lp is None else lp + f
        pv = value(c0, c1, p.astype(BF16))
        acc = pv if acc is None else acc + pv
    return acc * (1.0 / jnp.sum(lp, axis=-1, keepdims=True))


def _flash_update(s, pv, m_ref, l_ref, acc_ref):
    m_old = m_ref[...]
    m_new = jnp.maximum(m_old, jnp.max(s, axis=-1, keepdims=True))
    alpha = jnp.exp(m_old - m_new)
    p = jnp.exp(s - m_new)
    l_ref[...] = alpha * l_ref[...] + jnp.sum(p, axis=-1, keepdims=True)
    acc_ref[...] = alpha * acc_ref[...] + pv(p.astype(BF16))
    m_ref[...] = m_new


def _flash_init(m_ref, l_ref, acc_ref):
    m_ref[...] = jnp.full(m_ref.shape, -3e38, F32)
    l_ref[...] = jnp.zeros(l_ref.shape, F32)
    acc_ref[...] = jnp.zeros(acc_ref.shape, F32)


def _bucket_np(rel):
    n = np.maximum(rel, 0)
    exact = NUM_BUCKETS // 2
    nf = np.maximum(n, 1).astype(np.float32)
    large = exact + (np.log(nf / np.float32(exact)) / np.float32(math.log(MAX_DISTANCE / exact))
                     * np.float32(NUM_BUCKETS - exact)).astype(np.int32)
    large = np.minimum(large, NUM_BUCKETS - 1)
    return np.where(n < exact, n, large).astype(np.int32)


def _prompt_bucket_tiles():
    r = np.arange(TQ)[:, None]
    c = np.arange(TQ)[None, :]
    d0 = np.where(r - c >= 0, _bucket_np(r - c), MASKED_BUCKET)
    d1 = _bucket_np(TQ + r - c)
    return np.stack([d0, d1]).astype(np.int32)


def _decode_bucket_tiles(dec_seq, page, wb):
    q = np.arange(dec_seq)[:, None]
    c = np.arange(page)[None, :]
    last = _bucket_np(page + q - c)
    tail = np.where((c < dec_seq) & (c <= q), _bucket_np(q - c), MASKED_BUCKET)
    i = np.arange(wb)[None, :]
    relw = wb + q - i
    win = np.where(relw < WINDOW, _bucket_np(relw), MASKED_BUCKET)
    return np.concatenate([last, tail, win], axis=1).astype(np.int32)


def _overlap_np(n_rows, n_cols, n_cmp, n_sel):
    n = np.arange(n_rows)[:, None]
    j = np.arange(n_cols)[None, :]
    ov = (n * CMP_STRIDE < j * SEL_BLOCK + SEL_BLOCK) & (n * CMP_STRIDE + CMP_BLOCK > j * SEL_BLOCK)
    ov &= (n < n_cmp) & (j < n_sel)
    return ov.astype(np.float32)


def _blockdiag_np(n, group):
    i = np.arange(n)
    return (i[:, None] // group == i[None, :] // group).astype(np.float32)


def _chunk_perm_np(page):
    rows = np.arange(page)
    tok = (rows % (page // CMP_STRIDE)) * CMP_STRIDE + rows // (page // CMP_STRIDE)
    return (tok[:, None] == np.arange(page)[None, :]).astype(np.float32)


def _bias_body(tbl_ref, idxp_ref, idxd_ref, p_ref, d_ref):
    h = pl.program_id(0)
    far = tbl_ref[h, NUM_BUCKETS - 1]

    def expand(idx):
        acc = jnp.zeros(idx.shape, F32)
        for b in range(NUM_BUCKETS - 1):
            acc = jnp.where(idx == b, tbl_ref[h, b] - far, acc)
        return jnp.where(idx == MASKED_BUCKET, NEG, acc)

    p_ref[0] = expand(idxp_ref[0])
    p_ref[1] = expand(idxp_ref[1])
    d_ref[...] = expand(idxd_ref[...])


def _bias_tiles(rel_bias, idx_p, idx_d):
    nh = rel_bias.shape[0]
    dq, wd = idx_d.shape
    return pl.pallas_call(
        _bias_body,
        grid=(nh,),
        in_specs=[pl.BlockSpec(memory_space=pltpu.SMEM),
                  pl.BlockSpec((2, TQ, TQ), lambda h: (0, 0, 0)),
                  pl.BlockSpec((dq, wd), lambda h: (0, 0))],
        out_specs=[pl.BlockSpec((None, 2, TQ, TQ), lambda h: (h, 0, 0, 0)),
                   pl.BlockSpec((None, dq, wd), lambda h: (h, 0, 0))],
        out_shape=[jax.ShapeDtypeStruct((nh, 2, TQ, TQ), F32),
                   jax.ShapeDtypeStruct((nh, dq, wd), F32)],
        compiler_params=_params("arbitrary"),
        name="bias_tiles",
    )(rel_bias, idx_p, idx_d)


def _normed_input(x_ref, g_ref):
    x = x_ref[...]
    ms = jnp.mean(x * x, axis=-1, keepdims=True)
    return (x * lax.rsqrt(ms + EPS) * g_ref[...]).astype(BF16)


def _in_proj_sample_body(x_ref, g_ref, w_ref, b_ref, bd_ref, gain_ref,
                         dq_o, dk_o, dv_o, nq_o, kv_o, gs_o, gm_o, gn_o, *, d_model):
    h = _normed_input(x_ref, g_ref)

    def seg(a, b):
        return _mm(h, w_ref[:, a:b]) + b_ref[:, a:b]

    bd = bd_ref[...]
    bd2 = bd[0:LANE, 0:LANE]
    dq_o[...] = _group_rms(seg(0, C_DK), bd, gain_ref[0:1, :])
    dk_o[...] = _group_rms(seg(C_DK, C_DV), bd, gain_ref[1:2, :])
    dv_o[...] = seg(C_DV, C_NQ)
    nq_o[...] = _group_rms(seg(C_NQ, C_KV), bd, gain_ref[2:3, :])
    z = seg(C_KV, C_GA)
    kv_o[:, 0:256] = z[:, 0:256]
    kv_o[:, 256:384] = _group_rms(z[:, 256:384], bd2, gain_ref[3:4, 0:128])
    kv_o[:, 384:512] = z[:, 384:512]
    kv_o[:, 512:640] = _group_rms(z[:, 512:640], bd2, gain_ref[3:4, 128:256])
    kv_o[:, 640:768] = z[:, 640:768]
    z = seg(C_GA, C_GN)
    gs_o[...] = z * _sigmoid(z)
    gm_o[...] = _sigmoid(seg(C_GN, C_GN + 2 * d_model))
    gn_o[...] = _sigmoid(seg(C_GN + 2 * d_model, C_GN + 2 * d_model + LANE))


def _in_proj_sample(x2d, l, prep):
    n, d = x2d.shape
    tm = min(256, n)
    c = prep["w_in"].shape[-1]

    def row(w):
        return pl.BlockSpec((tm, w), lambda i: (i, 0))

    def lay(*shape):
        return pl.BlockSpec((None,) + shape, lambda i: (l,) + (0,) * len(shape))

    widths = [512, 512, 512, 512, 768, 1024, 2 * d, LANE]
    return pl.pallas_call(
        functools.partial(_in_proj_sample_body, d_model=d),
        grid=(n // tm,),
        in_specs=[row(d), lay(1, d), lay(d, c), lay(1, c),
                  pl.BlockSpec((512, 512), lambda i: (0, 0)), lay(8, 512)],
        out_specs=[row(w) for w in widths],
        out_shape=[jax.ShapeDtypeStruct((n, w), F32) for w in widths],
        compiler_params=_params("arbitrary"),
        name="in_proj_sample",
    )(x2d, prep["norm_g"], prep["w_in"], prep["b_in"], prep["bd512"], prep["gains"])


N_STACKED = 8


def _in_proj_prompt_body(*refs, d_model, n_alias, n_tiles):
    x_ref, g_ref, w_ref, b_ref, wt_ref, bt_ref, bd_ref, gain_ref, gaint_ref = refs[0:9]
    (dq_o, nq_o, dvb_o, ckn_o, cvn_o, gs_o, gm_o, gn_o, ktb_o, kvtb_o,
     dkt_o, dv_o, ckt_o, cvt_o, skt_o, svt_o, wkt_o, wvt_o) = refs[9 + n_alias:]
    h = _normed_input(x_ref, g_ref)
    tm = h.shape[0]

    def seg(a, b):
        return _mm(h, w_ref[:, a:b]) + b_ref[:, a:b]

    def seg_t(a, b):
        return _mm_nt(wt_ref[a:b, :], h) + bt_ref[a:b, :]

    bd = bd_ref[...]
    dq_o[...] = _group_rms(seg(0, C_DK), bd, gain_ref[0:1, :]).astype(BF16)
    nq_o[...] = _group_rms(seg(C_NQ, C_KV), bd, gain_ref[2:3, :]).astype(BF16)
    y = seg(C_DV, C_NQ)
    dvb_o[...] = y.astype(BF16)
    for hh in range(N_HA):
        dv_o[pl.ds(hh, tm, stride=N_HA), :] = y[:, hh * VD_A:(hh + 1) * VD_A]
    z = seg(C_KV, C_KV + 256)
    ckn_o[...] = z[:, 0:128]
    cvn_o[...] = z[:, 128:256]
    z = seg(C_GA, C_GN)
    gs_o[...] = z * _sigmoid(z)
    gm_o[...] = _sigmoid(seg(C_GN, C_GN + 2 * d_model))
    gn_o[...] = _sigmoid(seg(C_GN + 2 * d_model, C_GN + 2 * d_model + LANE))

    yt = _group_rms_t(seg_t(0, 512), gaint_ref[0:512, :])
    dkt_o[...] = yt
    ktb_o[...] = yt.astype(BF16)
    zt = seg_t(512, N_T)
    ckt_o[...] = zt[0:128]
    cvt_o[...] = zt[128:256]
    sk = _group_rms_t(zt[256:384], gaint_ref[512:640, :])
    sv = zt[384:512]
    wk = _group_rms_t(zt[512:640], gaint_ref[640:768, :])
    wv = zt[640:768]
    skt_o[...] = sk
    svt_o[...] = sv
    wkt_o[...] = wk
    wvt_o[...] = wv
    tok = (pl.program_id(0) % n_tiles) * tm + lax.broadcasted_iota(jnp.int32, (HD_B, tm), 1)
    blk = lax.broadcasted_iota(jnp.int32, (HD_B, tm), 0)
    unsel_rows = jnp.where(blk == (tok >> SEL_SHIFT), NEG, 0.0).astype(BF16)
    skb = sk.astype(BF16)
    kvtb_o[0:HD_B, :] = skb[0:HD_B]
    kvtb_o[HD_B:LANE, :] = unsel_rows
    kvtb_o[LANE:LANE + HD_B, :] = skb[HD_B:LANE]
    kvtb_o[LANE + HD_B:2 * LANE, :] = unsel_rows
    kvtb_o[256:384, :] = sv.astype(BF16)
    kvtb_o[384:512, :] = wk.astype(BF16)
    kvtb_o[512:640, :] = wv.astype(BF16)


def _in_proj_prompt(x2d, l, prep, bsz, t, depth, stacked):
    n, d = x2d.shape
    tm = TQ
    nt = t // tm
    c = prep["w_in"].shape[-1]
    nw = min(WINDOW, t)
    nwt = nw // tm

    def row(w):
        return pl.BlockSpec((tm, w), lambda i: (i, 0))

    def lay(*shape):
        return pl.BlockSpec((None,) + shape, lambda i: (l,) + (0,) * len(shape))

    def tile_t(rows):
        return pl.BlockSpec((None, rows, tm), lambda i: (i // nt, 0, i % nt))

    def stack_t(rows):
        return pl.BlockSpec((None, None, rows, tm), lambda i: (l, i // nt, 0, i % nt))

    win_t = pl.BlockSpec((None, None, LANE, tm),
                         lambda i: (l, i // nt, 0, jnp.maximum(i % nt - (nt - nwt), 0)))
    out_specs = [row(512), row(512), row(512), row(LANE), row(LANE), row(1024), row(2 * d), row(LANE),
                 tile_t(512), tile_t(KV_ROWS),
                 stack_t(512), pl.BlockSpec((None, tm * N_HA, VD_A), lambda i: (l, i, 0)),
                 stack_t(LANE), stack_t(LANE), stack_t(LANE), stack_t(LANE), win_t, win_t]
    kv_t = jax.ShapeDtypeStruct((depth, bsz, LANE, t), F32)
    win_shape = jax.ShapeDtypeStruct((depth, bsz, LANE, nw), F32)
    out_shape = [jax.ShapeDtypeStruct((n, 512), BF16), jax.ShapeDtypeStruct((n, 512), BF16),
                 jax.ShapeDtypeStruct((n, 512), BF16), jax.ShapeDtypeStruct((n, LANE), F32),
                 jax.ShapeDtypeStruct((n, LANE), F32), jax.ShapeDtypeStruct((n, 1024), F32),
                 jax.ShapeDtypeStruct((n, 2 * d), F32), jax.ShapeDtypeStruct((n, LANE), F32),
                 jax.ShapeDtypeStruct((bsz, 512, t), BF16), jax.ShapeDtypeStruct((bsz, KV_ROWS, t), BF16),
                 jax.ShapeDtypeStruct((depth, bsz, 512, t), F32),
                 jax.ShapeDtypeStruct((depth, n * N_HA, VD_A), F32),
                 kv_t, kv_t, kv_t, kv_t, win_shape, win_shape]
    n_alias = 0 if stacked is None else N_STACKED
    n_in = 9
    n_plain = len(out_shape) - N_STACKED
    aliases = {n_in + k: n_plain + k for k in range(n_alias)}
    return pl.pallas_call(
        functools.partial(_in_proj_prompt_body, d_model=d, n_alias=n_alias, n_tiles=nt),
        grid=(n // tm,),
        in_specs=[row(d), lay(1, d), lay(d, c), lay(1, c), lay(N_T, d), lay(N_T, 1),
                  pl.BlockSpec((512, 512), lambda i: (0, 0)), lay(8, 512), lay(N_T, 1)]
                 + [pl.BlockSpec(memory_space=pl.ANY)] * n_alias,
        out_specs=out_specs,
        out_shape=out_shape,
        input_output_aliases=aliases,
        compiler_params=_params("arbitrary"),
        name="in_proj_prompt",
    )(x2d, prep["norm_g"], prep["w_in"], prep["b_in"], prep["w_t"], prep["b_t"], prep["bd512"],
      prep["gains"], prep["gains_t"], *(stacked or ()))


def _out_proj_body(x_ref, oa_ref, ob_ref, gm_ref, wa_ref, wb_ref, wo_ref, y_ref, *, d_model):
    ya = _mm(oa_ref[...].astype(BF16), wa_ref[...])
    yb = _mm(ob_ref[...].astype(BF16), wb_ref[...])
    gm = gm_ref[...]
    m = gm[:, 0:d_model] * ya + gm[:, d_model:2 * d_model] * yb
    y_ref[...] = x_ref[...] + _mm(m.astype(BF16), wo_ref[...])


def _out_proj(x2d, oa, ob, gm, l, prep):
    n, d = x2d.shape
    tm = min(512, n)

    def row(w):
        return pl.BlockSpec((tm, w), lambda i: (i, 0))

    def lay(*shape):
        return pl.BlockSpec((None,) + shape, lambda i: (l,) + (0,) * len(shape))

    return pl.pallas_call(
        functools.partial(_out_proj_body, d_model=d),
        grid=(n // tm,),
        in_specs=[row(d), row(512), row(512), row(2 * d), lay(512, d), lay(512, d), lay(d, d)],
        out_specs=row(d),
        out_shape=jax.ShapeDtypeStruct((n, d), F32),
        compiler_params=_params("arbitrary"),
        name="out_proj",
    )(x2d, oa, ob, gm, prep["w_a"], prep["w_b"], prep["w_out"])


def _lambda(lam_ref, lam_init):
    lv = lam_ref[...]
    a = jnp.sum(lv[0:1] * lv[1:2], axis=-1, keepdims=True)
    b = jnp.sum(lv[2:3] * lv[3:4], axis=-1, keepdims=True)
    return jnp.exp(a) - jnp.exp(b) + lam_init


def _diff_finish(o1, o2, lam, sub, lam_init):
    o = o1 - lam * o2
    ms = jnp.mean(o * o, axis=-1, keepdims=True)
    return o * lax.rsqrt(ms + EPS) * sub * (1.0 - lam_init)


def _diff_prompt_body(*refs, i, lam_init):
    q_ref, kt_ref, v_ref, gs_ref, bias_ref, lam_ref, sub_ref = refs[0:7]
    o_ref, s_sc = refs[-2:]
    lam = _lambda(lam_ref, lam_init)
    for hh in range(DIFF_HEADS_PER_STEP):
        cols = slice(hh * LANE, (hh + 1) * LANE)
        q = q_ref[:, cols]
        lane = lax.broadcasted_iota(jnp.int32, q.shape, 1)
        zero = jnp.zeros_like(q)
        qq = jnp.concatenate([jnp.where(lane < HD_A, q, zero), jnp.where(lane >= HD_A, q, zero)], axis=0)

        def score(c0, c1, hh=hh, cols=cols, qq=qq):
            s = _mm(qq, kt_ref[cols, c0:c1])
            if c0 >= (i - 1) * TQ:
                b = bias_ref[hh, 0 if c0 == i * TQ else 1]
                s = s + jnp.concatenate([b, b], axis=0)
            return s

        def value(c0, c1, p, cols=cols):
            return _mm(p, v_ref[c0:c1, cols])

        o = _two_pass_softmax(score, value, _key_spans(0, i), s_sc.at[hh])
        y = _diff_finish(o[0:TQ], o[TQ:2 * TQ], lam, sub_ref[...], lam_init)
        o_ref[:, cols] = (y * gs_ref[:, cols]).astype(BF16)


def _diff_prompt(dq_b, ktb, dv_b, gs, bias_p, l, prep, bsz, t, lam_init):
    nt = t // TQ
    hs = DIFF_HEADS_PER_STEP
    out = None
    for i in range(nt):
        w = (i + 1) * TQ
        alias = [] if out is None else [out]
        out = pl.pallas_call(
            functools.partial(_diff_prompt_body, i=i, lam_init=lam_init),
            grid=(bsz, N_HA // hs),
            in_specs=[pl.BlockSpec((TQ, hs * LANE), lambda b, h, i=i: (b * nt + i, h)),
                      pl.BlockSpec((None, hs * LANE, w), lambda b, h: (b, h, 0)),
                      pl.BlockSpec((t, hs * LANE), lambda b, h: (b, h)),
                      pl.BlockSpec((TQ, hs * LANE), lambda b, h, i=i: (b * nt + i, h)),
                      pl.BlockSpec((hs, 2, TQ, TQ), lambda b, h: (h, 0, 0, 0)),
                      pl.BlockSpec((None, 4, HD_A), lambda b, h: (l, 0, 0)),
                      pl.BlockSpec((None, 1, VD_A), lambda b, h: (l, 0, 0))]
                     + [pl.BlockSpec(memory_space=pl.ANY)] * len(alias),
            out_specs=pl.BlockSpec((TQ, hs * LANE), lambda b, h, i=i: (b * nt + i, h)),
            out_shape=jax.ShapeDtypeStruct((bsz * t, N_HA * VD_A), BF16),
            input_output_aliases={7: 0} if alias else {},
            scratch_shapes=[pltpu.VMEM((hs, 2 * TQ, w), F32)],
            compiler_params=_params("arbitrary", "arbitrary"),
            name=f"diff_prompt_q{i}",
        )(dq_b, ktb, dv_b, gs, bias_p, prep["lam"], prep["subln"], *alias)
    return out


def _deinterleave(a):
    lane = lax.broadcasted_iota(jnp.int32, (a.shape[0], LANE), 1)
    low = lane < HD_B
    g0, g1 = [], []
    for k in range(CMP_STRIDE // 2):
        xe = a[:, 2 * LANE * k:2 * LANE * k + LANE]
        xo = a[:, 2 * LANE * k + LANE:2 * LANE * (k + 1)]
        g0.append(jnp.where(low, xe, pltpu.roll(xo, HD_B, 1)))
        g1.append(jnp.where(low, pltpu.roll(xe, HD_B, 1), xo))
    return jnp.concatenate(g0, axis=1).astype(BF16), jnp.concatenate(g1, axis=1).astype(BF16)


def _compress_partial(a, w1_ref):
    a0, a1 = _deinterleave(a)
    half = CMP_STRIDE * HD_B
    w1a = w1_ref[0:half, :]
    w1b = w1_ref[half:2 * half, :]
    return _mm(a0, w1a), _mm(a1, w1a), _mm(a0, w1b), _mm(a1, w1b)


def _compress_finish(parts, w1_ref, pos_ref, b1_ref, w2_ref, b2_ref):
    ha0, ha1, hb0, hb1 = parts
    m = ha0.shape[0]
    pos = jnp.broadcast_to(pos_ref[...], (8, pos_ref.shape[-1])).astype(BF16)
    c = _mm(pos, w1_ref[...])[0:1] + b1_ref[...]
    h0 = ha0 + pltpu.roll(hb0, m - 1, 0) + c
    h1 = ha1 + pltpu.roll(hb1, m - 1, 0) + c
    hid = jnp.concatenate([h0 * _sigmoid(h0), h1 * _sigmoid(h1)], axis=1).astype(BF16)
    return _mm(hid, w2_ref[...]) + b2_ref[...]


def _compress_prompt_body(ak_ref, av_ref, w1_ref, pos_ref, b1_ref, w2_ref, b2_ref, kn_ref, bd_ref,
                          ck_o, cv_o):
    outs = []
    for kv, a_ref in enumerate((ak_ref, av_ref)):
        parts = _compress_partial(a_ref[...], w1_ref.at[kv])
        outs.append(_compress_finish(parts, w1_ref.at[kv], pos_ref.at[kv], b1_ref.at[kv],
                                     w2_ref.at[kv], b2_ref.at[kv]))
    ck_o[...] = _group_rms(outs[0], bd_ref[...], kn_ref[...])
    cv_o[...] = outs[1]


def _cmp_weight_specs(l):
    def lay(*shape):
        return pl.BlockSpec((None,) + shape, lambda *a: (l,) + (0,) * len(shape))
    flat = CMP_BLOCK * HD_B
    return [lay(2, flat, 256), lay(2, 1, flat), lay(2, 1, 256), lay(2, 512, LANE), lay(2, 1, LANE),
            lay(1, LANE), pl.BlockSpec((LANE, LANE), lambda *a: (0, 0))]


def _cmp_weights(prep):
    return (prep["w1"], prep["pos"], prep["b1"], prep["w2bd"], prep["b2"], prep["cmp_kn"], prep["bd128"])


def _compress_prompt(ck, cv, l, prep, bsz, t):
    nch = t // CMP_STRIDE
    wdt = CMP_STRIDE * LANE
    a_spec = pl.BlockSpec((nch, wdt), lambda b: (b, 0))
    o_spec = pl.BlockSpec((nch, LANE), lambda b: (b, 0))
    return pl.pallas_call(
        _compress_prompt_body,
        grid=(bsz,),
        in_specs=[a_spec, a_spec] + _cmp_weight_specs(l),
        out_specs=[o_spec, o_spec],
        out_shape=[jax.ShapeDtypeStruct((bsz * nch, LANE), F32)] * 2,
        compiler_params=_params("arbitrary"),
        name="compress_prompt",
    )(ck.reshape(bsz * nch, wdt), cv.reshape(bsz * nch, wdt), *_cmp_weights(prep))


def _nsa_q_rows(qf, g):
    m = qf.shape[0]
    zero = jnp.zeros((m, HD_B), F32)
    rows = []
    for r in range(R_B):
        c0 = (g * R_B + r) * HD_B
        piece = qf[:, c0:c0 + HD_B]
        rows.append(jnp.concatenate([piece, zero] if g == 0 else [zero, piece], axis=1))
    return jnp.concatenate(rows, axis=0).astype(BF16)


def _cmp_attention(qrows, ckc, cvc, valid, anyv):
    hi, lo = _split(ckc)
    s = _mm_nt(qrows, hi) + _mm_nt(qrows, lo)
    s = jnp.where(valid, s, NEG)
    m = jnp.max(s, axis=-1, keepdims=True)
    e = jnp.exp(s - m)
    ncp = e.shape[1]
    ones = jnp.ones((ncp, LANE), BF16)
    e_hi, e_lo = _split(e)
    den = _mm(e_hi, ones) + _mm(e_lo, ones)
    p = e / jnp.concatenate([den] * (ncp // LANE), axis=1) * anyv
    return p, _mm(p.astype(BF16), cvc.astype(BF16))


def _select_blocks(imp_sum, ov, own, n_sel, top):
    hi, lo = _split(imp_sum)
    imp = _mm(hi, ov) + _mm(lo, ov)
    j = lax.broadcasted_iota(jnp.int32, imp.shape, 1)
    imp = jnp.where(j > own, -1.0, jnp.where((j == own) | (j == 0), FORCE, imp))
    imp = jnp.where(j >= n_sel, -2.0, imp)
    jf = j.astype(F32)
    rank = jnp.zeros(imp.shape, F32)
    for k in range(n_sel):
        col = imp[:, k:k + 1]
        rank = rank + jnp.where(col > imp, 1.0, jnp.where(col == imp, jnp.where(jf > k, 1.0, 0.0), 0.0))
    return jnp.where(rank < top, 1.0, 0.0)


def _select_blocks_t(imp_sum, ov_t, own_row, n_sel, top):
    m = imp_sum.shape[0]
    nsp = ov_t.shape[0]
    nsr = -(-n_sel // 8) * 8
    hi, lo = _split(imp_sum)
    imp = (_mm_nt(ov_t, hi) + _mm_nt(ov_t, lo))[0:nsr]
    j = lax.broadcasted_iota(jnp.int32, imp.shape, 0)
    imp = jnp.where(j > own_row, -1.0, jnp.where((j == own_row) | (j == 0), FORCE, imp))
    imp = jnp.where(j >= n_sel, -2.0, imp)
    jf = j.astype(F32)
    rank = jnp.zeros(imp.shape, F32)
    for k in range(n_sel):
        row = imp[k:k + 1, :]
        rank = rank + jnp.where(row > imp, 1.0, jnp.where(row == imp, jnp.where(jf > k, 1.0, 0.0), 0.0))
    sel_t = jnp.where(rank < top, 1.0, 0.0)
    if nsp > nsr:
        sel_t = jnp.concatenate([sel_t, jnp.zeros((nsp - nsr, m), F32)], axis=0)
    r = lax.broadcasted_iota(jnp.int32, (m, m), 0)
    c = lax.broadcasted_iota(jnp.int32, (m, m), 1)
    eye = jnp.where(r == c, 1.0, 0.0).astype(BF16)
    return _mm_nt(eye, sel_t.astype(BF16)).astype(BF16)


def _block_mask(sel_bf16, first_block, n_blocks):
    nb = sel_bf16.shape[1]
    width = n_blocks * SEL_BLOCK
    j = lax.broadcasted_iota(jnp.int32, (nb, width), 0)
    c = lax.broadcasted_iota(jnp.int32, (nb, width), 1)
    e = jnp.where(j == first_block + (c >> SEL_SHIFT), 1.0, 0.0).astype(BF16)
    return jnp.where(_mm(sel_bf16, e) > 0.5, 0.0, NEG)


def _nsa_prompt_body(*refs, i, n_cmp, n_sel, top):
    q_ref, ckc_ref, cvc_ref, kv_ref, gn_ref, gs_ref, pb_ref, ov_ref = refs[0:8]
    o_ref, s_sc = refs[-2:]
    rows = R_B * TQ
    qf = q_ref[...].astype(F32)
    gn = gn_ref[...]
    ckc = ckc_ref[...]
    cvc = cvc_ref[...]
    ncp = ckc.shape[0]
    qp = i * TQ + lax.broadcasted_iota(jnp.int32, (TQ, 1), 0)
    qp4 = i * TQ + (lax.broadcasted_iota(jnp.int32, (rows, 1), 0) & (TQ - 1))
    nidx = lax.broadcasted_iota(jnp.int32, (rows, ncp), 1)
    valid = (nidx * CMP_STRIDE + (CMP_BLOCK - 1) <= qp4) & (nidx < n_cmp)
    anyv = jnp.where(qp4 >= CMP_BLOCK - 1, 1.0, 0.0) * float(n_cmp > 0)
    own_row = (i * TQ + lax.broadcasted_iota(jnp.int32, (1, TQ), 1)) >> SEL_SHIFT
    rr = lax.broadcasted_iota(jnp.int32, (rows, TQ), 0) & (TQ - 1)
    cc = lax.broadcasted_iota(jnp.int32, (rows, TQ), 1)
    anti = jnp.where(cc > rr, 0.0, NEG)
    wt = WINDOW // TQ
    bpt = TQ // SEL_BLOCK
    pieces = []
    for g in range(N_KV):
        qrows = _nsa_q_rows(qf, g)

        def bias(d, g=g):
            return jnp.concatenate([pb_ref[g * R_B + r, d] for r in range(R_B)], axis=0)

        p_c, o_cmp = _cmp_attention(qrows, ckc, cvc, valid, anyv)
        imp_sum = p_c[0:TQ]
        for r in range(1, R_B):
            imp_sum = imp_sum + p_c[r * TQ:(r + 1) * TQ]
        sel = _select_blocks_t(imp_sum, ov_ref[...], own_row, n_sel, top)

        def near_bias(s, c0, bias=bias):
            if c0 == i * TQ:
                return s + bias(0)
            if c0 == (i - 1) * TQ:
                return s + bias(1)
            return s

        unsel = (1.0 - sel.astype(F32))[:, 0:HD_B]
        qsel = jnp.concatenate(
            [jnp.concatenate([qf[:, (g * R_B + r) * HD_B:(g * R_B + r + 1) * HD_B], unsel], axis=1)
             for r in range(R_B)], axis=0).astype(BF16)

        def sel_score(c0, c1, g=g, qsel=qsel, near_bias=near_bias):
            return near_bias(_mm(qsel, kv_ref[g * LANE:(g + 1) * LANE, c0:c1]), c0)

        def sel_value(c0, c1, p):
            return _mm_nt(p, kv_ref[2 * LANE:3 * LANE, c0:c1])

        o_sel = _two_pass_softmax(sel_score, sel_value, _key_spans(0, i), s_sc)

        def win_score(c0, c1, qrows=qrows, near_bias=near_bias):
            s = _mm(qrows, kv_ref[3 * LANE:4 * LANE, c0:c1])
            if c0 == (i - wt) * TQ:
                s = s + anti
            return near_bias(s, c0)

        def win_value(c0, c1, p):
            return _mm_nt(p, kv_ref[4 * LANE:5 * LANE, c0:c1])

        win_spans = [(j * TQ, (j + 1) * TQ) for j in range(max(i - wt, 0), i + 1)]
        o_win = _two_pass_softmax(win_score, win_value, win_spans, s_sc)

        for r in range(R_B):
            hh = g * R_B + r
            rs = slice(r * TQ, (r + 1) * TQ)
            o = (gn[:, 3 * hh:3 * hh + 1] * o_cmp[rs] + gn[:, 3 * hh + 1:3 * hh + 2] * o_sel[rs]
                 + gn[:, 3 * hh + 2:3 * hh + 3] * o_win[rs])
            pieces.append(o[:, g * HD_B:(g + 1) * HD_B])
    o_ref[...] = (jnp.concatenate(pieces, axis=1) * gs_ref[...]).astype(BF16)


def _nsa_prompt(nq_b, ckc, cvc, kvtb, gn, gs, bias_p, l, bsz, t):
    nt = t // TQ
    nch = t // CMP_STRIDE
    n_cmp = nch - 1
    n_sel = -(-t // SEL_BLOCK)
    top = min(SEL_TOPN, n_sel)
    nsp = -(-n_sel // LANE) * LANE
    ov = jnp.asarray(_overlap_np(nch, nsp, n_cmp, n_sel).T, BF16)
    rows = R_B * TQ
    out = None
    for i in range(nt):
        w = (i + 1) * TQ
        alias = [] if out is None else [out]
        out = pl.pallas_call(
            functools.partial(_nsa_prompt_body, i=i, n_cmp=n_cmp, n_sel=n_sel, top=top),
            grid=(bsz,),
            in_specs=[pl.BlockSpec((TQ, 512), lambda b, i=i: (b * nt + i, 0)),
                      pl.BlockSpec((nch, LANE), lambda b: (b, 0)),
                      pl.BlockSpec((nch, LANE), lambda b: (b, 0)),
                      pl.BlockSpec((None, KV_ROWS, w), lambda b: (b, 0, 0)),
                      pl.BlockSpec((TQ, LANE), lambda b, i=i: (b * nt + i, 0)),
                      pl.BlockSpec((TQ, 512), lambda b, i=i: (b * nt + i, 1)),
                      pl.BlockSpec((N_HB, 2, TQ, TQ), lambda b: (0, 0, 0, 0)),
                      pl.BlockSpec((nsp, nch), lambda b: (0, 0))]
                     + [pl.BlockSpec(memory_space=pl.ANY)] * len(alias),
            out_specs=pl.BlockSpec((TQ, 512), lambda b, i=i: (b * nt + i, 0)),
            out_shape=jax.ShapeDtypeStruct((bsz * t, N_HB * HD_B), BF16),
            input_output_aliases={8: 0} if alias else {},
            scratch_shapes=[pltpu.VMEM((rows, w), F32)],
            compiler_params=_params("arbitrary"),
            name=f"nsa_prompt_q{i}",
        )(nq_b, ckc, cvc, kvtb, gn, gs, bias_p, ov, *alias)
    return out


def _diff_decode_body(pt_ref, *refs, n_pg, dec_seq, lam_init):
    kt_pages = refs[0:n_pg]
    v_pages = refs[n_pg:2 * n_pg]
    ktt_ref, vt_ref, q_ref, gs_ref, bias_ref, lam_ref, sub_ref, o_ref, m_sc, l_sc, acc_sc = refs[2 * n_pg:]
    c = pl.program_id(1)
    last = pl.num_programs(1) - 1
    page = kt_pages[0].shape[1]
    hrows = 2 * dec_seq
    q = q_ref[...]
    blk = lax.broadcasted_iota(jnp.int32, q.shape, 1) >> HD_SHIFT
    qrows = jnp.concatenate([jnp.where(blk == hm, q, 0.0) for hm in range(2 * N_HA)], axis=0).astype(BF16)

    def values(refs_):
        def pv(p):
            outs = []
            for h in range(N_HA):
                vh = jnp.concatenate([r[pl.ds(h, page, stride=N_HA), :].astype(BF16) for r in refs_], axis=0)
                outs.append(_mm(p[h * hrows:(h + 1) * hrows], vh))
            return jnp.concatenate(outs, axis=0)
        return pv

    @pl.when(c == 0)
    def _():
        _flash_init(m_sc, l_sc, acc_sc)

    parts = [_mm(qrows, r[...].astype(BF16)) for r in kt_pages]
    parts[-1] = parts[-1] + jnp.where(c == last, bias_ref[:, 0:page], 0.0)
    _flash_update(jnp.concatenate(parts, axis=1), values(v_pages), m_sc, l_sc, acc_sc)

    @pl.when(c == last)
    def _():
        s = _mm(qrows, ktt_ref[...].astype(BF16)) + bias_ref[:, page:2 * page]
        _flash_update(s, values([vt_ref]), m_sc, l_sc, acc_sc)
        o = acc_sc[...] / l_sc[...]
        lam = _lambda(lam_ref, lam_init)
        outs = []
        for h in range(N_HA):
            r0 = h * hrows
            outs.append(_diff_finish(o[r0:r0 + dec_seq], o[r0 + dec_seq:r0 + hrows], lam, sub_ref[...],
                                     lam_init))
        o_ref[...] = jnp.concatenate(outs, axis=1) * gs_ref[...]


def _page_specs(n_pg, l, shape):
    return [pl.BlockSpec((None, None) + shape, (lambda b, c, pt, k=k: (l, pt[b, c * n_pg + k], 0, 0)))
            for k in range(n_pg)]


def _diff_decode(pt, pool_kt, pool_v, kt_tail, v_tail, dq, gs, bias_d, l, prep, lam_init):
    db, n_pages = pt.shape
    page = pool_kt.shape[3]
    dec_seq = dq.shape[0] // db
    n_pg = min(DIFF_PAGES_PER_STEP, n_pages)
    rows = 2 * N_HA * dec_seq
    wd = bias_d.shape[1]
    grid_spec = pltpu.PrefetchScalarGridSpec(
        num_scalar_prefetch=1,
        grid=(db, n_pages // n_pg),
        in_specs=_page_specs(n_pg, l, (512, page)) + _page_specs(n_pg, l, (page * N_HA, VD_A)) + [
            pl.BlockSpec((None, 512, page), lambda b, c, pt: (b, 0, 0)),
            pl.BlockSpec((None, page * N_HA, VD_A), lambda b, c, pt: (b, 0, 0)),
            pl.BlockSpec((dec_seq, 512), lambda b, c, pt: (b, 0)),
            pl.BlockSpec((dec_seq, 512), lambda b, c, pt: (b, 0)),
            pl.BlockSpec((rows, wd), lambda b, c, pt: (0, 0)),
            pl.BlockSpec((None, 4, HD_A), lambda b, c, pt: (l, 0, 0)),
            pl.BlockSpec((None, 1, VD_A), lambda b, c, pt: (l, 0, 0))],
        out_specs=pl.BlockSpec((dec_seq, 512), lambda b, c, pt: (b, 0)),
        scratch_shapes=[pltpu.VMEM((rows, 1), F32), pltpu.VMEM((rows, 1), F32),
                        pltpu.VMEM((rows, VD_A), F32)])
    return pl.pallas_call(
        functools.partial(_diff_decode_body, n_pg=n_pg, dec_seq=dec_seq, lam_init=lam_init),
        grid_spec=grid_spec,
        out_shape=jax.ShapeDtypeStruct((db * dec_seq, N_HA * VD_A), F32),
        compiler_params=_params("arbitrary", "arbitrary"),
        name="diff_decode",
    )(pt, *([pool_kt] * n_pg), *([pool_v] * n_pg), kt_tail, v_tail, dq, gs, bias_d,
      prep["lam"], prep["subln"])


def _chunk_rows(pages, perm):
    cpp = perm.shape[0] // CMP_STRIDE
    rows = []
    for k in range(0, len(pages), 2):
        x2 = jnp.concatenate([pages[k][...], pages[k + 1][...]], axis=0).astype(BF16)
        t2 = _mm_nt(perm, x2)
        for half in range(2):
            t = t2[:, half * LANE:(half + 1) * LANE]
            rows.append(jnp.concatenate([t[cpp * r:cpp * (r + 1)] for r in range(CMP_STRIDE)], axis=1))
    return jnp.concatenate(rows, axis=0)


def _nsa_select_decode_body(pt_ref, *refs, n_pg, n_cmp, n_sel, top, pos0, dec_seq):
    page_sets = (refs[0:n_pg], refs[n_pg:2 * n_pg])
    (perm_ref, w1_ref, pos_ref, b1_ref, w2_ref, b2_ref, kn_ref, bd_ref, q_ref, ov_ref,
     ocmp_o, sel_o) = refs[2 * n_pg:]
    outs = []
    for kv, pages in enumerate(page_sets):
        parts = _compress_partial(_chunk_rows(pages, perm_ref[...]), w1_ref.at[kv])
        outs.append(_compress_finish(parts, w1_ref.at[kv], pos_ref.at[kv], b1_ref.at[kv],
                                     w2_ref.at[kv], b2_ref.at[kv]))
    ckc = _group_rms(outs[0], bd_ref[...], kn_ref[...])
    cvc = outs[1]
    ncp = ckc.shape[0]
    qf = q_ref[...]
    rows = R_B * dec_seq
    qp = pos0 + lax.broadcasted_iota(jnp.int32, (dec_seq, 1), 0)
    qp4 = pos0 + (lax.broadcasted_iota(jnp.int32, (rows, 1), 0) & (dec_seq - 1))
    nidx = lax.broadcasted_iota(jnp.int32, (rows, ncp), 1)
    valid = (nidx * CMP_STRIDE + (CMP_BLOCK - 1) <= qp4) & (nidx < n_cmp)
    own = qp >> SEL_SHIFT
    anyv = jnp.where(qp4 >= CMP_BLOCK - 1, 1.0, 0.0) * float(n_cmp > 0)
    for g in range(N_KV):
        qrows = _nsa_q_rows(qf, g)
        p_c, o_cmp = _cmp_attention(qrows, ckc, cvc, valid, anyv)
        imp_sum = p_c[0:dec_seq]
        for r in range(1, R_B):
            imp_sum = imp_sum + p_c[r * dec_seq:(r + 1) * dec_seq]
        sel = _select_blocks(imp_sum, ov_ref[...], own, n_sel, top)
        ocmp_o[g * rows:(g + 1) * rows, :] = o_cmp
        sel_o[g * rows:(g + 1) * rows, :] = jnp.concatenate([sel] * R_B, axis=0)


def _nsa_select_decode(pt, pool_kt, pool_vt, nq, l, prep, pos0, dec_seq):
    db, n_pages = pt.shape
    page = pool_kt.shape[3]
    nch = n_pages * (page // CMP_STRIDE)
    n_cmp = nch - 1
    n_sel = -(-(pos0 + dec_seq) // SEL_BLOCK)
    top = min(SEL_TOPN, n_sel)
    nsp = -(-n_sel // LANE) * LANE
    ov = jnp.asarray(_overlap_np(nch, nsp, n_cmp, n_sel), BF16)
    rows = N_HB * dec_seq
    pages = [pl.BlockSpec((None, None, LANE, page), (lambda b, pt, k=k: (l, pt[b, k], 0, 0)))
             for k in range(n_pages)]
    grid_spec = pltpu.PrefetchScalarGridSpec(
        num_scalar_prefetch=1,
        grid=(db,),
        in_specs=pages + pages + [pl.BlockSpec((page, page), lambda b, pt: (0, 0))] + _cmp_weight_specs(l) + [
            pl.BlockSpec((dec_seq, 512), lambda b, pt: (b, 0)),
            pl.BlockSpec((nch, nsp), lambda b, pt: (0, 0))],
        out_specs=[pl.BlockSpec((rows, LANE), lambda b, pt: (b, 0)),
                   pl.BlockSpec((rows, nsp), lambda b, pt: (b, 0))])
    return pl.pallas_call(
        functools.partial(_nsa_select_decode_body, n_pg=n_pages, n_cmp=n_cmp, n_sel=n_sel, top=top,
                          pos0=pos0, dec_seq=dec_seq),
        grid_spec=grid_spec,
        out_shape=[jax.ShapeDtypeStruct((db * rows, LANE), F32),
                   jax.ShapeDtypeStruct((db * rows, nsp), F32)],
        compiler_params=_params("arbitrary"),
        name="nsa_select_decode",
    )(pt, *([pool_kt] * n_pages), *([pool_vt] * n_pages), prep["perm"], *_cmp_weights(prep), nq, ov)


def _nsa_decode_body(pt_ref, *refs, n_pg, n_pages, dec_seq):
    kt_pages = refs[0:n_pg]
    vt_pages = refs[n_pg:2 * n_pg]
    (ktt_ref, vtt_ref, wks_ref, wvs_ref, wkt_ref, wvt_ref, q_ref, sel_ref, ocmp_ref, gn_ref, gs_ref,
     bias_ref) = refs[2 * n_pg:2 * n_pg + 12]
    o_ref, swk_o, swv_o, m_sc, l_sc, acc_sc = refs[-6:]
    c = pl.program_id(1)
    last = pl.num_programs(1) - 1
    page = kt_pages[0].shape[1]
    bpp = page // SEL_BLOCK
    qf = q_ref[...]
    qrows = jnp.concatenate([_nsa_q_rows(qf, g) for g in range(N_KV)], axis=0)
    sel = sel_ref[...].astype(BF16)

    def values(refs_):
        def pv(p):
            out = _mm_nt(p[:, 0:page], refs_[0][...].astype(BF16))
            for k in range(1, len(refs_)):
                out = out + _mm_nt(p[:, k * page:(k + 1) * page], refs_[k][...].astype(BF16))
            return out
        return pv

    @pl.when(c == 0)
    def _():
        _flash_init(m_sc, l_sc, acc_sc)

    parts = [_mm(qrows, r[...].astype(BF16)) for r in kt_pages]
    parts[-1] = parts[-1] + jnp.where(c == last, bias_ref[:, 0:page], 0.0)
    s = jnp.concatenate(parts, axis=1) + _block_mask(sel, c * (n_pg * bpp), n_pg * bpp)
    _flash_update(s, values(vt_pages), m_sc, l_sc, acc_sc)

    @pl.when(c == last)
    def _():
        tail_bias = bias_ref[:, page:2 * page]
        s = _mm(qrows, ktt_ref[...].astype(BF16)) + _block_mask(sel, n_pages * bpp, bpp) + tail_bias
        _flash_update(s, values([vtt_ref]), m_sc, l_sc, acc_sc)
        o_sel = acc_sc[...] / l_sc[...]
        s_w = _mm(qrows, wks_ref[...].astype(BF16)) + bias_ref[:, 2 * page:]
        s_t = _mm(qrows, wkt_ref[...].astype(BF16)) + tail_bias
        m = jnp.maximum(jnp.max(s_w, axis=-1, keepdims=True), jnp.max(s_t, axis=-1, keepdims=True))
        p_w = jnp.exp(s_w - m)
        p_t = jnp.exp(s_t - m)
        den = jnp.sum(p_w, axis=-1, keepdims=True) + jnp.sum(p_t, axis=-1, keepdims=True)
        o_win = (_mm_nt(p_w.astype(BF16), wvs_ref[...].astype(BF16))
                 + _mm_nt(p_t.astype(BF16), wvt_ref[...].astype(BF16))) / den
        o_cmp = ocmp_ref[...]
        gn = gn_ref[...]
        pieces = []
        for hh in range(N_HB):
            g = hh // R_B
            rs = slice(hh * dec_seq, (hh + 1) * dec_seq)
            o = (gn[:, 3 * hh:3 * hh + 1] * o_cmp[rs] + gn[:, 3 * hh + 1:3 * hh + 2] * o_sel[rs]
                 + gn[:, 3 * hh + 2:3 * hh + 3] * o_win[rs])
            pieces.append(o[:, g * HD_B:(g + 1) * HD_B])
        o_ref[...] = jnp.concatenate(pieces, axis=1) * gs_ref[...]
        wb = wks_ref.shape[1]
        for state_ref, tail_ref, out in ((wks_ref, wkt_ref, swk_o), (wvs_ref, wvt_ref, swv_o)):
            out[...] = jnp.concatenate([state_ref[...], tail_ref[...]], axis=1)[:, dec_seq:dec_seq + wb]


def _nsa_decode(pt, pool_kt, pool_vt, kt_tail, vt_tail, win_kt, win_vt, wkt_tail, wvt_tail, nq, sel, ocmp,
                gn, gs, bias_d, l, next_win):
    db, n_pages = pt.shape
    page = pool_kt.shape[3]
    dec_seq = nq.shape[0] // db
    n_pg = min(SEL_PAGES_PER_STEP, n_pages)
    rows = N_HB * dec_seq
    wd = bias_d.shape[1]
    wb = win_kt.shape[3]
    nsp = sel.shape[1]
    per_b = pl.BlockSpec((None, LANE, page), lambda b, c, pt: (b, 0, 0))
    win = pl.BlockSpec((None, None, LANE, wb), lambda b, c, pt: (l, b, 0, 0))
    grid_spec = pltpu.PrefetchScalarGridSpec(
        num_scalar_prefetch=1,
        grid=(db, n_pages // n_pg),
        in_specs=_page_specs(n_pg, l, (LANE, page)) + _page_specs(n_pg, l, (LANE, page)) + [
            per_b, per_b, win, win, per_b, per_b,
            pl.BlockSpec((dec_seq, 512), lambda b, c, pt: (b, 0)),
            pl.BlockSpec((rows, nsp), lambda b, c, pt: (b, 0)),
            pl.BlockSpec((rows, LANE), lambda b, c, pt: (b, 0)),
            pl.BlockSpec((dec_seq, LANE), lambda b, c, pt: (b, 0)),
            pl.BlockSpec((dec_seq, 512), lambda b, c, pt: (b, 1)),
            pl.BlockSpec((rows, wd), lambda b, c, pt: (0, 0))]
                 + [pl.BlockSpec(memory_space=pl.ANY)] * len(next_win),
        out_specs=[pl.BlockSpec((dec_seq, 512), lambda b, c, pt: (b, 0)), win, win],
        scratch_shapes=[pltpu.VMEM((rows, 1), F32), pltpu.VMEM((rows, 1), F32),
                        pltpu.VMEM((rows, LANE), F32)])
    n_in = 1 + 2 * n_pg + 12
    return pl.pallas_call(
        functools.partial(_nsa_decode_body, n_pg=n_pg, n_pages=n_pages, dec_seq=dec_seq),
        grid_spec=grid_spec,
        out_shape=[jax.ShapeDtypeStruct((db * dec_seq, N_HB * HD_B), F32),
                   jax.ShapeDtypeStruct(win_kt.shape, F32), jax.ShapeDtypeStruct(win_vt.shape, F32)],
        input_output_aliases={n_in + k: 1 + k for k in range(len(next_win))},
        compiler_params=_params("arbitrary", "arbitrary"),
        name="nsa_decode",
    )(pt, *([pool_kt] * n_pg), *([pool_vt] * n_pg), kt_tail, vt_tail, win_kt, win_vt, wkt_tail, wvt_tail,
      nq, sel, ocmp, gn, gs, bias_d, *next_win)


def _prepare(norm_g, w_in, b_in, diff_q_norm, diff_k_norm, diff_lambda, diff_subln, nsa_q_norm,
             nsa_k_norm, cmp_pos, cmp_w1, cmp_b1, cmp_w2, cmp_b2, w_branch_a, w_branch_b, w_out, page):
    depth, d, _ = w_in.shape
    c_gm = C_GN + N_GN
    pad = LANE - N_GN

    def reorder(a):
        z = jnp.zeros(a.shape[:-1] + (pad,), a.dtype)
        return jnp.concatenate([a[..., :C_GN], a[..., c_gm:c_gm + 2 * d], a[..., C_GN:c_gm], z], axis=-1)

    def t_rows(a):
        return jnp.concatenate([a[..., C_DK:C_DV], a[..., C_KV:C_GA]], axis=-1)

    scale_a = HD_A ** -0.5
    scale_b = HD_B ** -0.5
    gains = jnp.zeros((depth, 8, 512), F32)
    gains = gains.at[:, 0].set(jnp.tile(diff_q_norm, (1, 8)) * scale_a)
    gains = gains.at[:, 1].set(jnp.tile(diff_k_norm, (1, 8)))
    gains = gains.at[:, 2].set(jnp.tile(nsa_q_norm, (1, 8)) * scale_b)
    gains = gains.at[:, 3, 0:128].set(jnp.tile(nsa_k_norm[:, 1], (1, 2)))
    gains = gains.at[:, 3, 128:256].set(jnp.tile(nsa_k_norm[:, 2], (1, 2)))
    gains_t = jnp.concatenate([jnp.tile(diff_k_norm, (1, 8)), jnp.tile(nsa_k_norm[:, 1], (1, 2)),
                               jnp.tile(nsa_k_norm[:, 2], (1, 2)), jnp.zeros((depth, N_T - 768), F32)], axis=-1)
    zero = jnp.zeros_like(cmp_w2)
    w2bd = jnp.concatenate([jnp.concatenate([cmp_w2, zero], axis=-1),
                            jnp.concatenate([zero, cmp_w2], axis=-1)], axis=-2)
    return {
        "norm_g": norm_g.reshape(depth, 1, d),
        "w_in": reorder(w_in).astype(BF16),
        "b_in": reorder(b_in).reshape(depth, 1, -1),
        "w_t": jnp.swapaxes(t_rows(w_in), 1, 2).astype(BF16),
        "b_t": t_rows(b_in).reshape(depth, N_T, 1),
        "gains": gains,
        "gains_t": gains_t.reshape(depth, N_T, 1),
        "bd512": jnp.asarray(_blockdiag_np(512, HD_A), BF16),
        "bd128": jnp.asarray(_blockdiag_np(LANE, HD_B), BF16),
        "perm": jnp.asarray(_chunk_perm_np(page), BF16),
        "lam": diff_lambda,
        "subln": diff_subln.reshape(depth, 1, VD_A),
        "w1": cmp_w1.astype(BF16),
        "pos": cmp_pos.reshape(depth, 2, 1, CMP_BLOCK * HD_B),
        "b1": cmp_b1.reshape(depth, 2, 1, -1),
        "w2bd": w2bd.astype(BF16),
        "b2": jnp.tile(cmp_b2, (1, 1, 2)).reshape(depth, 2, 1, LANE),
        "cmp_kn": jnp.tile(nsa_k_norm[:, 0], (1, 2)).reshape(depth, 1, LANE),
        "w_a": w_branch_a.astype(BF16),
        "w_b": w_branch_b.astype(BF16),
        "w_out": w_out.astype(BF16),
    }


def kernel(x_prompt, x_sample, cache_diff_k, cache_diff_v, cache_cmp_k, cache_cmp_v, cache_sel_k, cache_sel_v,
           state_win_k, state_win_v, page_table, rel_bias, norm_g, w_in, b_in, diff_q_norm, diff_k_norm,
           diff_lambda, diff_subln, nsa_q_norm, nsa_k_norm, cmp_pos, cmp_w1, cmp_b1, cmp_w2, cmp_b2,
           w_branch_a, w_branch_b, w_out):
    bsz, t, d = x_prompt.shape
    db, dec_seq, _ = x_sample.shape
    depth, n_pool, page = cache_diff_k.shape[:3]
    n_pages = page_table.shape[1]
    past = n_pages * page
    wb = state_win_k.shape[2]
    assert t % TQ == 0 and WINDOW % TQ == 0 and page == LANE and wb == WINDOW and past % SEL_BLOCK == 0
    assert dec_seq <= CMP_STRIDE and dec_seq % 8 == 0 and min(WINDOW, t) % TQ == 0
    assert -(-t // SEL_BLOCK) <= HD_B

    prep = _prepare(norm_g, w_in, b_in, diff_q_norm, diff_k_norm, diff_lambda, diff_subln, nsa_q_norm,
                    nsa_k_norm, cmp_pos, cmp_w1, cmp_b1, cmp_w2, cmp_b2, w_branch_a, w_branch_b, w_out, page)
    bias_p, bias_d = _bias_tiles(rel_bias, jnp.asarray(_prompt_bucket_tiles()),
                                 jnp.asarray(_decode_bucket_tiles(dec_seq, page, wb)))
    wd = bias_d.shape[-1]
    bias_d_diff = jnp.repeat(bias_d[:N_HA], 2, axis=0).reshape(2 * N_HA * dec_seq, wd)
    bias_d_nsa = bias_d[N_HA:].reshape(N_HB * dec_seq, wd)
    bias_p_nsa = bias_p[N_HA:]

    pool_dkt = jnp.transpose(cache_diff_k, (0, 1, 3, 4, 5, 2)).reshape(depth, n_pool, 512, page)
    pool_dv = cache_diff_v.reshape(depth, n_pool, page * N_HA, VD_A)
    def pool_t(a):
        return jnp.transpose(a, (0, 1, 3, 4, 2)).reshape(depth, a.shape[1], LANE, a.shape[2])
    pool_ckt, pool_cvt, pool_skt, pool_svt = (pool_t(a) for a in (cache_cmp_k, cache_cmp_v, cache_sel_k,
                                                                 cache_sel_v))
    win_kt, win_vt = pool_t(state_win_k), pool_t(state_win_v)

    def tail_t(a):
        a = jnp.swapaxes(a.reshape(db, dec_seq, a.shape[-1]), 1, 2)
        return jnp.pad(a, ((0, 0), (0, 0), (0, page - dec_seq)))

    xp = x_prompt.reshape(bsz * t, d)
    xs = x_sample.reshape(db * dec_seq, d)
    stacked = None
    next_win = []
    rows_s = []
    for l in range(depth):
        lam_init = 0.8 - 0.6 * math.exp(-0.3 * l)

        outs = _in_proj_prompt(xp, l, prep, bsz, t, depth, stacked)
        dq_b, nq_b, dv_b, ck_n, cv_n, gs, gm, gn, ktb, kvtb = outs[:10]
        stacked = tuple(outs[10:])
        oa = _diff_prompt(dq_b, ktb, dv_b, gs, bias_p, l, prep, bsz, t, lam_init)
        ckc, cvc = _compress_prompt(ck_n, cv_n, l, prep, bsz, t)
        ob = _nsa_prompt(nq_b, ckc, cvc, kvtb, gn, gs, bias_p_nsa, l, bsz, t)
        xp = _out_proj(xp, oa, ob, gm, l, prep)

        dq, dk, dv, nq, kv6, gs, gm, gn = _in_proj_sample(xs, l, prep)
        ck, cv, sk, sv, wk, wv = (kv6[:, LANE * k:LANE * (k + 1)] for k in range(6))
        v_tail = jnp.pad(dv.reshape(db, dec_seq * N_HA, VD_A), ((0, 0), (0, (page - dec_seq) * N_HA), (0, 0)))
        oa = _diff_decode(page_table, pool_dkt, pool_dv, tail_t(dk), v_tail, dq, gs, bias_d_diff, l, prep,
                          lam_init)
        ocmp, sel = _nsa_select_decode(page_table, pool_ckt, pool_cvt, nq, l, prep, past, dec_seq)
        ob, *next_win = _nsa_decode(page_table, pool_skt, pool_svt, tail_t(sk), tail_t(sv), win_kt, win_vt,
                                    tail_t(wk), tail_t(wv), nq, sel, ocmp, gn, gs, bias_d_nsa, l, next_win)
        xs = _out_proj(xs, oa, ob, gm, l, prep)
        rows_s.append((dk.reshape(db, dec_seq, N_HA, 2, HD_A), dv.reshape(db, dec_seq, N_HA, VD_A),
                       ck.reshape(db, dec_seq, N_KV, HD_B), cv.reshape(db, dec_seq, N_KV, HD_B),
                       sk.reshape(db, dec_seq, N_KV, HD_B), sv.reshape(db, dec_seq, N_KV, HD_B)))

    dkt, dvr, ckt, cvt, skt, svt, wkt, wvt = stacked

    def untranspose(a):
        return jnp.transpose(a.reshape(depth, a.shape[1], N_KV, HD_B, a.shape[-1]), (0, 1, 4, 2, 3))

    outs_p = [jnp.transpose(dkt.reshape(depth, bsz, N_HA, 2, HD_A, t), (0, 1, 5, 2, 3, 4)),
              dvr.reshape(depth, bsz, t, N_HA, VD_A),
              untranspose(ckt), untranspose(cvt), untranspose(skt), untranspose(svt),
              untranspose(wkt), untranspose(wvt)]
    outs_s = [jnp.stack(r, axis=0) for r in zip(*rows_s)] + [untranspose(a) for a in next_win]
    return (xp.reshape(bsz, t, d), xs.reshape(db, dec_seq, d), *outs_p, *outs_s)
```
